```python
import math
import jax
import jax.numpy as jnp
from jax import lax
import numpy as np

D_MODEL = 4096
BATCH = 8
SEQ = 2048
DEPTH = 4

GRID_W = 64
CTX_LEN = 256
N_DIR = 2
S5_WIDTH = D_MODEL // 2
S5_GROUP = 16
S5_GROUPS = S5_WIDTH // S5_GROUP
S5_STATE = 64
RW_WIDTH = D_MODEL // 2
RW_HEAD = 64
RW_HEADS = RW_WIDTH // RW_HEAD
LORA_W = 96
LORA_A = 96
LORA_G = 256
GATE_RANK = 256
ADA_RANK = 128
D_FF = 4 * D_MODEL
RW_COLS = 3 * RW_WIDTH + N_DIR * LORA_W + N_DIR * LORA_A + LORA_G
IN_COLS = S5_WIDTH + RW_COLS + GATE_RANK
RW_SPLITS = (RW_WIDTH, 2 * RW_WIDTH, 3 * RW_WIDTH,
             3 * RW_WIDTH + N_DIR * LORA_W,
             3 * RW_WIDTH + N_DIR * (LORA_W + LORA_A))
NORM_EPS = 1e-6
GN_EPS = 64e-5
A_RE_MAX = -1e-4

kernel_name = "hybrid_s5_rwkv7_diffusion_trunk"


def rms_norm(x, gain):
    xf = x.astype(jnp.float32)
    xf = xf * lax.rsqrt(jnp.mean(xf * xf, axis=-1, keepdims=True) + NORM_EPS)
    return xf.astype(x.dtype) * gain


def modulate(h, gain, shift, scale):
    return rms_norm(h, gain) * (1.0 + scale) + shift


def ada_modulation(cond, w_down, w_up, bias):
    m = (jax.nn.silu(cond) @ w_down) @ w_up + bias
    return jnp.split(m, 6, axis=-1)


def grid_qshift(z):
    bsz, length, ch = z.shape
    rows = length // GRID_W
    g = z.reshape(bsz, rows, GRID_W, ch)
    q = ch // 4
    col_pad = ((0, 0), (0, 0), (1, 1), (0, 0))
    row_pad = ((0, 0), (1, 1), (0, 0), (0, 0))
    left = jnp.pad(g[..., :q], col_pad)[:, :, :-2]
    right = jnp.pad(g[..., q:2 * q], col_pad)[:, :, 2:]
    up = jnp.pad(g[..., 2 * q:3 * q], row_pad)[:, :-2]
    down = jnp.pad(g[..., 3 * q:], row_pad)[:, 2:]
    return jnp.concatenate([left, right, up, down], axis=-1).reshape(bsz, length, ch)


def seq_shift(z):
    h = z.shape[-1] // 2
    prev = jnp.pad(z[..., :h], ((0, 0), (1, 0), (0, 0)))[:, :-1]
    nxt = jnp.pad(z[..., h:], ((0, 0), (0, 1), (0, 0)))[:, 1:]
    return jnp.concatenate([prev, nxt], axis=-1)


def _linear_recurrence(e1, e2):
    a1, b1 = e1
    a2, b2 = e2
    return a1 * a2, a2 * b1 + b2


def s5_discretize(a_re, a_im, log_dt, b_re, b_im):
    lam = lax.complex(jnp.minimum(a_re.astype(jnp.float32), A_RE_MAX), a_im.astype(jnp.float32))
    dt = jnp.exp(log_dt.astype(jnp.float32))[:, None]
    lam_bar = jnp.exp(lam * dt)
    b = lax.complex(b_re.astype(jnp.float32), b_im.astype(jnp.float32))
    b_bar = ((lam_bar - 1.0) / lam)[..., None] * b
    return lam_bar, b_bar


def s5_scan(u, lam_bar, b_bar, s0, reverse):
    bu = jnp.einsum("gpc,blgc->blgp", b_bar, u.astype(jnp.complex64))
    if reverse:
        bu = jnp.flip(bu, axis=1)
    bu = bu.at[:, 0].add(lam_bar * s0)
    a = jnp.broadcast_to(lam_bar, (1,) + bu.shape[1:])
    _, s = lax.associative_scan(_linear_recurrence, (a, bu), axis=1)
    final = s[:, -1]
    if reverse:
        s = jnp.flip(s, axis=1)
    return s, final


def s5_branch(u_ctx, u_lat, lp, emit_ctx):
    dtype = u_lat.dtype
    bsz = u_lat.shape[0]
    uc = u_ctx.astype(jnp.float32)
    ul = u_lat.astype(jnp.float32)
    ucg = uc.reshape(bsz, uc.shape[1], S5_GROUPS, S5_GROUP)
    ulg = ul.reshape(bsz, ul.shape[1], S5_GROUPS, S5_GROUP)
    s0 = jnp.zeros((bsz, S5_GROUPS, S5_STATE), jnp.complex64)
    y_ctx, y_lat = [], []
    for d in range(N_DIR):
        lam_bar, b_bar = s5_discretize(lp["s5_a_re"][d], lp["s5_a_im"][d], lp["s5_log_dt"][d],
                                       lp["s5_b_re"][d], lp["s5_b_im"][d])
        c_mat = lax.complex(lp["s5_c_re"][d].astype(jnp.float32),
                            lp["s5_c_im"][d].astype(jnp.float32))
        s_c, fin_c = s5_scan(ucg, lam_bar, b_bar, s0, d == 1)
        s_l, _ = s5_scan(ulg, lam_bar, b_bar, fin_c, d == 1)
        y_lat.append(jnp.einsum("gcp,blgp->blgc", c_mat, s_l).real)
        if emit_ctx:
            y_ctx.append(jnp.einsum("gcp,blgp->blgc", c_mat, s_c).real)

    def finish(ys, u):
        y = (ys[0] + ys[1]).reshape(u.shape) + lp["s5_d"] * u
        z = jax.nn.gelu(y)
        return (z * jax.nn.sigmoid(z @ lp["s5_glu_w"] + lp["s5_glu_b"])).astype(dtype)

    out_lat = finish(y_lat, ul)
    out_ctx = finish(y_ctx, uc) if emit_ctx else None
    return out_ctx, out_lat


def to_heads(t):
    return t.reshape(t.shape[:-1] + (RW_HEADS, RW_HEAD))


def rwkv_inputs(z, shift_fn, lp):
    z = z.astype(jnp.float32)
    z = z + (shift_fn(z) - z) * lp["rw_mu"]
    r, k, v, wd, ad, gd = jnp.split(z, RW_SPLITS, axis=-1)
    bsz, length = z.shape[:2]
    wd = wd.reshape(bsz, length, N_DIR, LORA_W)
    ad = ad.reshape(bsz, length, N_DIR, LORA_A)
    w = -jax.nn.softplus(-(lp["rw_w0"] + jnp.einsum("bldr,drc->bldc", jnp.tanh(wd), lp["rw_w_up"]))) - 0.5
    a = jax.nn.sigmoid(lp["rw_a0"] + jnp.einsum("bldr,drc->bldc", ad, lp["rw_a_up"]))
    g = jax.nn.sigmoid(gd) @ lp["rw_g_up"]
    kk = to_heads(k * lp["rw_k_k"])
    kk = kk * lax.rsqrt(jnp.maximum(jnp.sum(kk * kk, axis=-1, keepdims=True), 1e-24))
    k_dir = k[:, :, None, :] * (1.0 + (a - 1.0) * lp["rw_k_a"])
    return r, k_dir, kk, v, w, a, g


def rwkv_scan(r, w, k, kk, a, v, s0, reverse, emit):
    decay = jnp.exp(-jnp.exp(w))
    xs = tuple(jnp.moveaxis(t, 1, 0) for t in (r, decay, k, kk, a, v))

    def step(S, inp):
        r_t, w_t, k_t, kk_t, a_t, v_t = inp
        sa = jnp.einsum("bhvk,bhk->bhv", S, kk_t)
        S = (S * w_t[:, :, None, :]
             - sa[..., None] * (kk_t * a_t)[:, :, None, :]
             + v_t[..., None] * k_t[:, :, None, :])
        y = jnp.einsum("bhvk,bhk->bhv", S, r_t) if emit else None
        return S, y

    s_fin, ys = lax.scan(step, s0, xs, reverse=reverse)
    return (jnp.moveaxis(ys, 0, 1) if emit else None), s_fin


def rwkv_direction(inp, d, s0, emit):
    r, k_dir, kk, v, w, a, _ = inp
    return rwkv_scan(to_heads(r), to_heads(w[:, :, d]), to_heads(k_dir[:, :, d]), kk,
                     to_heads(a[:, :, d]), to_heads(v), s0, d == 1, emit)


def rwkv_readout(y, inp, lp, dtype):
    r, k_dir, _, v, _, _, g = inp
    mu = jnp.mean(y, axis=-1, keepdims=True)
    var = jnp.mean(jnp.square(y - mu), axis=-1, keepdims=True)
    yn = ((y - mu) * lax.rsqrt(var + GN_EPS)).reshape(g.shape) * lp["rw_ln_w"] + lp["rw_ln_b"]
    bonus = jnp.sum(to_heads(r * jnp.mean(k_dir, axis=2)) * lp["rw_r_k"], axis=-1, keepdims=True) * to_heads(v)
    return ((yn + bonus.reshape(g.shape)) * g).astype(dtype)


def rwkv_branch(z_ctx, z_lat, lp, emit_ctx):
    dtype = z_lat.dtype
    ctx_in = rwkv_inputs(z_ctx, seq_shift, lp)
    lat_in = rwkv_inputs(z_lat, grid_qshift, lp)
    bsz = z_lat.shape[0]
    s0 = jnp.zeros((bsz, RW_HEADS, RW_HEAD, RW_HEAD), jnp.float32)
    y_ctx, y_lat = [], []
    for d in range(N_DIR):
        y_c, fin_c = rwkv_direction(ctx_in, d, s0, emit_ctx)
        y_l, _ = rwkv_direction(lat_in, d, fin_c, True)
        y_lat.append(y_l)
        if emit_ctx:
            y_ctx.append(y_c)
    out_lat = rwkv_readout(y_lat[0] + y_lat[1], lat_in, lp, dtype)
    out_ctx = rwkv_readout(y_ctx[0] + y_ctx[1], ctx_in, lp, dtype) if emit_ctx else None
    return out_ctx, out_lat


def merge_branches(ya, yb, gz, lp):
    pa = ya @ lp["w_proj_a"]
    pb = yb @ lp["w_proj_b"]
    ga = jax.nn.sigmoid(gz @ lp["gate_up"][0] + lp["gate_b"][0])
    gb = jax.nn.sigmoid(gz @ lp["gate_up"][1] + lp["gate_b"][1])
    return (ga * pa + gb * pb) @ lp["w_out"]


def token_mixer(n_ctx, n_lat, lp, emit_ctx):
    cut = (S5_WIDTH, S5_WIDTH + RW_COLS)
    u_ctx, rz_ctx, gz_ctx = jnp.split(n_ctx @ lp["w_in"], cut, axis=-1)
    u_lat, rz_lat, gz_lat = jnp.split(n_lat @ lp["w_in"], cut, axis=-1)
    ya_ctx, ya_lat = s5_branch(u_ctx, u_lat, lp, emit_ctx)
    yb_ctx, yb_lat = rwkv_branch(rz_ctx, rz_lat, lp, emit_ctx)
    out_lat = merge_branches(ya_lat, yb_lat, gz_lat, lp)
    out_ctx = merge_branches(ya_ctx, yb_ctx, gz_ctx, lp) if emit_ctx else None
    return out_ctx, out_lat


def sq_relu_mlp(h, w1, w2):
    return jnp.square(jax.nn.relu(h @ w1)) @ w2


def setup_inputs(seed: int = 0) -> dict:
    key = jax.random.key(seed)
    keys = iter(jax.random.split(key, 40))
    f32 = jnp.float32

    def normal(shape, scale):
        return jax.random.normal(next(keys), shape, f32) * scale

    def uniform(shape, lo, hi):
        return jax.random.uniform(next(keys), shape, f32, minval=lo, maxval=hi)

    L = DEPTH
    s5_shape = (L, N_DIR, S5_GROUPS, S5_STATE)
    return {
        "x": normal((BATCH, SEQ, D_MODEL), 1.0),
        "c": normal((BATCH, D_MODEL), 1.0),
        "ctx": normal((BATCH, CTX_LEN, D_MODEL), 1.0),
        "c_ctx": normal((D_MODEL,), 1.0),
        "ada_down": normal((L, D_MODEL, ADA_RANK), D_MODEL ** -0.5),
        "ada_up": normal((L, ADA_RANK, 6 * D_MODEL), 0.5 * ADA_RANK ** -0.5),
        "ada_b": normal((L, 6 * D_MODEL), 0.02),
        "norm1": 1.0 + normal((L, D_MODEL), 0.02),
        "w_in": normal((L, D_MODEL, IN_COLS), D_MODEL ** -0.5),
        "s5_a_re": -0.5 + normal(s5_shape, 0.01),
        "s5_a_im": math.pi * jnp.arange(S5_STATE, dtype=f32) + normal(s5_shape, 0.01),
        "s5_log_dt": uniform((L, N_DIR, S5_GROUPS), math.log(1e-3), math.log(1e-1)),
        "s5_b_re": normal((L, N_DIR, S5_GROUPS, S5_STATE, S5_GROUP), (2 * S5_GROUP) ** -0.5),
        "s5_b_im": normal((L, N_DIR, S5_GROUPS, S5_STATE, S5_GROUP), (2 * S5_GROUP) ** -0.5),
        "s5_c_re": normal((L, N_DIR, S5_GROUPS, S5_GROUP, S5_STATE), (2 * S5_STATE) ** -0.5),
        "s5_c_im": normal((L, N_DIR, S5_GROUPS, S5_GROUP, S5_STATE), (2 * S5_STATE) ** -0.5),
        "s5_d": normal((L, S5_WIDTH), 1.0),
        "s5_glu_w": normal((L, S5_WIDTH, S5_WIDTH), S5_WIDTH ** -0.5),
        "s5_glu_b": normal((L, S5_WIDTH), 0.02),
        "rw_mu": uniform((L, RW_COLS), 0.0, 1.0),
        "rw_w0": uniform((L, N_DIR, RW_WIDTH), -6.0, 0.0),
        "rw_w_up": normal((L, N_DIR, LORA_W, RW_WIDTH), 0.1 * LORA_W ** -0.5),
        "rw_a0": normal((L, N_DIR, RW_WIDTH), 0.1),
        "rw_a_up": normal((L, N_DIR, LORA_A, RW_WIDTH), 0.1 * LORA_A ** -0.5),
        "rw_g_up": normal((L, LORA_G, RW_WIDTH), LORA_G ** -0.5),
        "rw_k_k": 0.85 + normal((L, RW_WIDTH), 0.02),
        "rw_k_a": 1.0 + normal((L, RW_WIDTH), 0.02),
        "rw_r_k": normal((L, RW_HEADS, RW_HEAD), 0.1),
        "rw_ln_w": 1.0 + normal((L, RW_WIDTH), 0.02),
        "rw_ln_b": normal((L, RW_WIDTH), 0.02),
        "w_proj_a": normal((L, S5_WIDTH, D_MODEL), S5_WIDTH ** -0.5),
        "w_proj_b": normal((L, RW_WIDTH, D_MODEL), RW_WIDTH ** -0.5),
        "gate_up": normal((L, 2, GATE_RANK, D_MODEL), GATE_RANK ** -0.5),
        "gate_b": normal((L, 2, D_MODEL), 0.1),
        "w_out": normal((L, D_MODEL, D_MODEL), D_MODEL ** -0.5),
        "norm2": 1.0 + normal((L, D_MODEL), 0.02),
        "mlp_w1": normal((L, D_MODEL, D_FF), D_MODEL ** -0.5),
        "mlp_w2": normal((L, D_FF, D_MODEL), D_FF ** -0.5),
        "norm_f": 1.0 + normal((D_MODEL,), 0.02),
    }


def reference(x, c, ctx, c_ctx, ada_down, ada_up, ada_b, norm1, w_in,
              s5_a_re, s5_a_im, s5_log_dt, s5_b_re, s5_b_im, s5_c_re, s5_c_im,
              s5_d, s5_glu_w, s5_glu_b,
              rw_mu, rw_w0, rw_w_up, rw_a0, rw_a_up, rw_g_up, rw_k_k, rw_k_a,
              rw_r_k, rw_ln_w, rw_ln_b,
              w_proj_a, w_proj_b, gate_up, gate_b, w_out,
              norm2, mlp_w1, mlp_w2, norm_f):
    h_lat = x
    h_ctx = ctx
    for l in range(DEPTH):
        last = l == DEPTH - 1
        lp = {
            "w_in": w_in[l],
            "s5_a_re": s5_a_re[l], "s5_a_im": s5_a_im[l], "s5_log_dt": s5_log_dt[l],
            "s5_b_re": s5_b_re[l], "s5_b_im": s5_b_im[l],
            "s5_c_re": s5_c_re[l], "s5_c_im": s5_c_im[l],
            "s5_d": s5_d[l], "s5_glu_w": s5_glu_w[l], "s5_glu_b": s5_glu_b[l],
            "rw_mu": rw_mu[l], "rw_w0": rw_w0[l], "rw_w_up": rw_w_up[l],
            "rw_a0": rw_a0[l], "rw_a_up": rw_a_up[l], "rw_g_up": rw_g_up[l],
            "rw_k_k": rw_k_k[l], "rw_k_a": rw_k_a[l], "rw_r_k": rw_r_k[l],
            "rw_ln_w": rw_ln_w[l], "rw_ln_b": rw_ln_b[l],
            "w_proj_a": w_proj_a[l], "w_proj_b": w_proj_b[l],
            "gate_up": gate_up[l], "gate_b": gate_b[l], "w_out": w_out[l],
        }
        sh1, sc1, g1, sh2, sc2, g2 = ada_modulation(c[:, None, :], ada_down[l], ada_up[l], ada_b[l])
        sh1c, sc1c, g1c, sh2c, sc2c, g2c = ada_modulation(c_ctx, ada_down[l], ada_up[l], ada_b[l])

        n_lat = modulate(h_lat, norm1[l], sh1, sc1)
        n_ctx = modulate(h_ctx, norm1[l], sh1c, sc1c)
        m_ctx, m_lat = token_mixer(n_ctx, n_lat, lp, not last)

        h_lat = h_lat + g1 * m_lat
        h_lat = h_lat + g2 * sq_relu_mlp(modulate(h_lat, norm2[l], sh2, sc2), mlp_w1[l], mlp_w2[l])
        if not last:
            h_ctx = h_ctx + g1c * m_ctx
            h_ctx = h_ctx + g2c * sq_relu_mlp(modulate(h_ctx, norm2[l], sh2c, sc2c), mlp_w1[l], mlp_w2[l])
    return rms_norm(h_lat, norm_f)
```

```python
import functools

import jax
import jax.numpy as jnp
from jax import lax
from jax.experimental import pallas as pl
from jax.experimental.pallas import tpu as pltpu

F32 = jnp.float32
BF16 = jnp.bfloat16

GRID_W = 64
NORM_EPS = 1e-6
GN_EPS = 64e-5
A_RE_MAX = -1e-4
S5_CHUNK = 16
RW_CHUNK = 64
RW_PACK = 4
VMEM_LIMIT = 56 * 1024 * 1024


def _params(sem):
    return pltpu.CompilerParams(dimension_semantics=sem, vmem_limit_bytes=VMEM_LIMIT)


def _pick(n, pref):
    t = pref
    while n % t:
        t //= 2
    return t


def _dot(a, b):
    return jnp.dot(a, b, preferred_element_type=F32)


def _split(x):
    hi = x.astype(BF16)
    lo = (x - hi.astype(F32)).astype(BF16)
    return hi, lo


def _dot3(a, b):
    ah, al = _split(a)
    bh, bl = _split(b)
    return _dot(ah, bh) + (_dot(ah, bl) + _dot(al, bh))


def _dot_exact_rhs(a, b_bf16):
    ah, al = _split(a)
    return _dot(ah, b_bf16) + _dot(al, b_bf16)


def _sigmoid(x):
    return 1.0 / (1.0 + jnp.exp(-x))


def _gelu_tanh(x):
    c = 0.7978845608028654
    return 0.5 * x * (1.0 + jnp.tanh(c * (x + 0.044715 * (x * x * x))))


def _ada_kernel(cond_ref, wd_ref, wu_ref, b_ref, o_ref, mid_scr):
    @pl.when(pl.program_id(1) == 0)
    def _():
        c = cond_ref[...]
        mid_scr[...] = _dot3(c * _sigmoid(c), wd_ref[...])

    o_ref[...] = _dot3(mid_scr[...], wu_ref[...]) + b_ref[...]


def _ada_modulation(cond, w_down, w_up, bias):
    nl, d, rank = w_down.shape
    r = cond.shape[0]
    n_out = w_up.shape[-1]
    tn = _pick(n_out, 2048)
    return pl.pallas_call(
        _ada_kernel,
        grid=(nl, n_out // tn),
        in_specs=[
            pl.BlockSpec((r, d), lambda l, j: (0, 0)),
            pl.BlockSpec((None, d, rank), lambda l, j: (l, 0, 0)),
            pl.BlockSpec((None, rank, tn), lambda l, j: (l, 0, j)),
            pl.BlockSpec((None, 1, tn), lambda l, j: (l, 0, j)),
        ],
        out_specs=pl.BlockSpec((None, r, tn), lambda l, j: (l, 0, j)),
        out_shape=jax.ShapeDtypeStruct((nl, r, n_out), F32),
        scratch_shapes=[pltpu.VMEM((r, rank), F32)],
        compiler_params=_params(("arbitrary", "arbitrary")),
        name="ada_modulation",
    )(cond, w_down, w_up, bias.reshape(nl, 1, n_out))


def _norm_matmul_kernel(h_ref, gain_ref, shift_ref, scale_ref, w_ref, o_ref, n_scr, *, sq_relu):
    @pl.when(pl.program_id(1) == 0)
    def _():
        x = h_ref[...]
        ms = jnp.mean(x * x, axis=-1, keepdims=True)
        xn = x * lax.rsqrt(ms + NORM_EPS) * gain_ref[...]
        n_scr[...] = (xn * (1.0 + scale_ref[...]) + shift_ref[...]).astype(BF16)

    acc = _dot(n_scr[...], w_ref[...])
    if sq_relu:
        acc = jnp.square(jnp.maximum(acc, 0.0))
    o_ref[...] = acc.astype(o_ref.dtype)


def _norm_matmul(h, gain, mods, which_shift, w, row_of_block, *, tm, tn, row0, out_dtype, sq_relu, name):
    n, d = h.shape
    n_out = w.shape[1]
    b0 = row0 // tm
    grid = ((n - row0) // tm, n_out // tn)
    kern = functools.partial(_norm_matmul_kernel, sq_relu=sq_relu)
    return pl.pallas_call(
        kern,
        grid=grid,
        in_specs=[
            pl.BlockSpec((tm, d), lambda i, j: (i + b0, 0)),
            pl.BlockSpec((1, d), lambda i, j: (0, 0)),
            pl.BlockSpec((None, None, 1, d), lambda i, j: (which_shift, row_of_block(i + b0), 0, 0)),
            pl.BlockSpec((None, None, 1, d), lambda i, j: (which_shift + 1, row_of_block(i + b0), 0, 0)),
            pl.BlockSpec((d, tn), lambda i, j: (0, j)),
        ],
        out_specs=pl.BlockSpec((tm, tn), lambda i, j: (i + b0, j)),
        out_shape=jax.ShapeDtypeStruct((n, n_out), out_dtype),
        scratch_shapes=[pltpu.VMEM((tm, d), BF16)],
        compiler_params=_params(("arbitrary", "arbitrary")),
        name=name,
    )(h, gain.reshape(1, d), mods, mods, w)


def _res_matmul_kernel(x_ref, w_ref, h_ref, g_ref, o_ref):
    k = pl.program_id(1)
    part = _dot(x_ref[...], w_ref[...])

    @pl.when(k == 0)
    def _():
        o_ref[...] = part

    @pl.when(k > 0)
    def _():
        o_ref[...] += part

    @pl.when(k == pl.num_programs(1) - 1)
    def _():
        o_ref[...] = h_ref[...] + g_ref[...] * o_ref[...]


def _res_matmul(x, w, h, mods, which_gate, row_of_block, *, tm, tk, row0, name):
    n, d = h.shape
    kdim = x.shape[1]
    b0 = row0 // tm
    grid = ((n - row0) // tm, kdim // tk)
    return pl.pallas_call(
        _res_matmul_kernel,
        grid=grid,
        in_specs=[
            pl.BlockSpec((tm, tk), lambda i, k: (i + b0, k)),
            pl.BlockSpec((tk, d), lambda i, k: (k, 0)),
            pl.BlockSpec((tm, d), lambda i, k: (i + b0, 0)),
            pl.BlockSpec((None, None, 1, d), lambda i, k: (which_gate, row_of_block(i + b0), 0, 0)),
        ],
        out_specs=pl.BlockSpec((tm, d), lambda i, k: (i + b0, 0)),
        out_shape=jax.ShapeDtypeStruct((n, d), F32),
        input_output_aliases={2: 0},
        compiler_params=_params(("arbitrary", "arbitrary")),
        name=name,
    )(x, w, h, mods)


def _s5_matrices(a_re, a_im, log_dt, b_re, b_im, c_re, c_im, d_skip):
    t_len = S5_CHUNK
    hp = lax.Precision.HIGHEST
    lam_re = jnp.minimum(a_re.astype(F32), A_RE_MAX)
    lam_im = a_im.astype(F32)
    dt = jnp.exp(log_dt.astype(F32))[..., None]
    j = jnp.arange(t_len + 1, dtype=F32)[:, None, None, None]
    mag = jnp.exp(j * (lam_re * dt))
    ang = j * (lam_im * dt)
    pw_re = mag * jnp.cos(ang)
    pw_im = mag * jnp.sin(ang)
    nr, ni = pw_re[1] - 1.0, pw_im[1]
    den = lam_re * lam_re + lam_im * lam_im
    qr = (nr * lam_re + ni * lam_im) / den
    qi = (ni * lam_re - nr * lam_im) / den
    bbr = qr[..., None] * b_re - qi[..., None] * b_im
    bbi = qr[..., None] * b_im + qi[..., None] * b_re
    n_dir, g, p, c = bbr.shape
    clr = c_re[None] * pw_re[:t_len, :, :, None, :] - c_im[None] * pw_im[:t_len, :, :, None, :]
    cli = c_re[None] * pw_im[:t_len, :, :, None, :] + c_im[None] * pw_re[:t_len, :, :, None, :]
    kj = (jnp.einsum("jdgop,dgpi->jdgoi", clr, bbr, precision=hp)
          - jnp.einsum("jdgop,dgpi->jdgoi", cli, bbi, precision=hp))
    s_idx = jnp.arange(t_len)[:, None]
    t_idx = jnp.arange(t_len)[None, :]
    lag_f = t_idx - s_idx
    lag_b = s_idx - t_idx

    def toeplitz(kd, lag):
        m = kd[jnp.clip(lag, 0, t_len - 1)]
        m = jnp.where((lag >= 0)[:, :, None, None, None], m, 0.0)
        return jnp.transpose(m, (2, 0, 4, 1, 3))

    km = toeplitz(kj[:, 0], lag_f) + toeplitz(kj[:, 1], lag_b)
    eye_t = jnp.eye(t_len, dtype=F32)
    eye_c = jnp.eye(c, dtype=F32)
    skip = d_skip.astype(F32).reshape(g, c)
    km = km + (eye_t[None, :, None, :, None] * eye_c[None, None, :, None, :]) * skip[:, None, :, None, None]
    km = km.reshape(g, t_len * c, t_len * c)
    sf = (t_len - 1) - jnp.arange(t_len)
    sb = jnp.arange(t_len)

    def state_in(d, e):
        pr, pi = pw_re[e, d], pw_im[e, d]
        re = pr[..., None] * bbr[d][None] - pi[..., None] * bbi[d][None]
        im = pr[..., None] * bbi[d][None] + pi[..., None] * bbr[d][None]
        tr = lambda z: jnp.transpose(z, (1, 0, 3, 2)).reshape(g, t_len * c, p)
        return tr(re), tr(im)

    fr, fi = state_in(0, sf)
    br, bi = state_in(1, sb)
    bm = jnp.concatenate([fr, br, fi, bi], axis=-1)
    ef = jnp.arange(t_len) + 1
    eb = t_len - jnp.arange(t_len)

    def state_out(d, e):
        pr, pi = pw_re[e, d], pw_im[e, d]
        re = c_re[d][None] * pr[:, :, None, :] - c_im[d][None] * pi[:, :, None, :]
        im = c_re[d][None] * pi[:, :, None, :] + c_im[d][None] * pr[:, :, None, :]
        tr = lambda z: jnp.transpose(z, (1, 3, 0, 2)).reshape(g, p, t_len * c)
        return tr(re), -tr(im)

    cfr, cfi = state_out(0, ef)
    cbr, cbi = state_out(1, eb)
    cm = jnp.concatenate([cfr, cbr, cfi, cbi], axis=1)
    lam = jnp.stack([jnp.concatenate([pw_re[t_len, 0], pw_re[t_len, 1]], axis=-1),
                     jnp.concatenate([pw_im[t_len, 0], pw_im[t_len, 1]], axis=-1)], axis=1)
    return km.astype(BF16), bm.astype(BF16), cm.astype(BF16), lam


def _s5_kernel(u_ref, bm_ref, km_ref, cm_ref, lam_ref, y_ref, xin_scr, fwd_scr, bwd_scr, *,
               n_ctx, n_chunks, bsz, p_state):
    u = u_ref[...]
    xin_scr[...] = _dot(u, bm_ref[...])
    two_p = 2 * p_state
    lam_r = jnp.broadcast_to(lam_ref[0:1, :], (bsz, two_p))
    lam_i = jnp.broadcast_to(lam_ref[1:2, :], (bsz, two_p))
    is_fwd = lax.broadcasted_iota(jnp.int32, (bsz, two_p), 1) < p_state

    def body(i, carry):
        re, im = carry
        pb = jnp.where(i < n_ctx, n_ctx - 1 - i, n_chunks + n_ctx - 1 - i)
        rf = pl.multiple_of(i * bsz, bsz)
        rb = pl.multiple_of(pb * bsz, bsz)
        xf = xin_scr[pl.ds(rf, bsz), :]
        xb = xin_scr[pl.ds(rb, bsz), :]
        x_re = jnp.where(is_fwd, xf[:, :two_p], xb[:, :two_p])
        x_im = jnp.where(is_fwd, xf[:, two_p:], xb[:, two_p:])
        state = jnp.concatenate([re, im], axis=1)
        fwd_scr[pl.ds(rf, bsz), :] = state
        bwd_scr[pl.ds(rb, bsz), :] = state
        return (re * lam_r - im * lam_i + x_re, re * lam_i + im * lam_r + x_im)

    zero = jnp.zeros((bsz, two_p), F32)
    lax.fori_loop(0, n_chunks, body, (zero, zero))
    lane = lax.broadcasted_iota(jnp.int32, (1, 2 * two_p), 1)
    fwd_lane = (lane % two_p) < p_state
    xs = jnp.where(fwd_lane, fwd_scr[...], bwd_scr[...]).astype(BF16)
    y_ref[...] = _dot(u, km_ref[...]) + _dot(xs, cm_ref[...])


def _s5_scan(u_g, km, bm, cm, lam, *, n_ctx, n_chunks, bsz):
    g, rows, tc = u_g.shape
    p4 = bm.shape[-1]
    kern = functools.partial(_s5_kernel, n_ctx=n_ctx, n_chunks=n_chunks, bsz=bsz, p_state=p4 // 4)
    return pl.pallas_call(
        kern,
        grid=(g,),
        in_specs=[
            pl.BlockSpec((None, rows, tc), lambda i: (i, 0, 0)),
            pl.BlockSpec((None, tc, p4), lambda i: (i, 0, 0)),
            pl.BlockSpec((None, tc, tc), lambda i: (i, 0, 0)),
            pl.BlockSpec((None, p4, tc), lambda i: (i, 0, 0)),
            pl.BlockSpec((None, 2, p4 // 2), lambda i: (i, 0, 0)),
        ],
        out_specs=pl.BlockSpec((None, rows, tc), lambda i: (i, 0, 0)),
        out_shape=jax.ShapeDtypeStruct((g, rows, tc), F32),
        scratch_shapes=[pltpu.VMEM((rows, p4), F32), pltpu.VMEM((rows, p4), F32), pltpu.VMEM((rows, p4), F32)],
        compiler_params=_params(("arbitrary",)),
        name="s5_scan",
    )(u_g, bm, km, cm, lam)


def _glu_kernel(y_ref, w_ref, b_ref, o_ref):
    z = _gelu_tanh(y_ref[...])
    o_ref[...] = (z * _sigmoid(_dot(z.astype(BF16), w_ref[...]) + b_ref[...])).astype(o_ref.dtype)


def _s5_glu(y, w, b, *, tm):
    n, c = y.shape
    return pl.pallas_call(
        _glu_kernel,
        grid=(n // tm,),
        in_specs=[
            pl.BlockSpec((tm, c), lambda i: (i, 0)),
            pl.BlockSpec((c, c), lambda i: (0, 0)),
            pl.BlockSpec((1, c), lambda i: (0, 0)),
        ],
        out_specs=pl.BlockSpec((tm, c), lambda i: (i, 0)),
        out_shape=jax.ShapeDtypeStruct((n, c), BF16),
        compiler_params=_params(("arbitrary",)),
        name="s5_glu",
    )(y, w, b.reshape(1, c))


def _head_ones(width, head):
    i = jnp.arange(width) // head
    return (i[:, None] == i[None, :]).astype(BF16)


def _rw_prep_kernel(zr_ref, zk_ref, zv_ref, sr_ref, sk_ref, sv_ref, zs_ref, ss_ref,
                    mur_ref, muk_ref, muv_ref, mus_ref,
                    w0_ref, wup_ref, a0_ref, aup_ref, gup_ref, kk_ref, ka_ref, rk_ref, ones_ref,
                    r_o, kk_o, v_o, lw_o, kd_o, beta_o, g_o, bonus_o, *, lora_w, lora_a, small0):
    def mix(z, zs, mu):
        return z + (zs - z) * mu

    r = mix(zr_ref[...], sr_ref[...], mur_ref[...])
    k = mix(zk_ref[...], sk_ref[...], muk_ref[...])
    v = mix(zv_ref[...], sv_ref[...], muv_ref[...])
    sm = mix(zs_ref[:, small0:], ss_ref[...], mus_ref[...])
    o_a = 2 * lora_w
    o_g = o_a + 2 * lora_a
    ones = ones_ref[...]
    g_o[...] = _dot(_sigmoid(sm[:, o_g:]).astype(BF16), gup_ref[...])
    kx = k * kk_ref[...]
    ss = _dot_exact_rhs(kx * kx, ones)
    kk = kx * lax.rsqrt(jnp.maximum(ss, 1e-24))
    k_sum = jnp.zeros_like(k)
    for d in range(2):
        wd = jnp.tanh(sm[:, d * lora_w:(d + 1) * lora_w])
        w = w0_ref[d] + _dot3(wd, wup_ref[d])
        w = -(jnp.maximum(-w, 0.0) + jnp.log(1.0 + jnp.exp(-jnp.abs(w)))) - 0.5
        lw_o[d] = -jnp.exp(w)
        ad = sm[:, o_a + d * lora_a:o_a + (d + 1) * lora_a]
        a = _sigmoid(a0_ref[d] + _dot3(ad, aup_ref[d]))
        kd = k * (1.0 + (a - 1.0) * ka_ref[...])
        kd_o[d] = kd
        beta_o[d] = kk * a
        k_sum = k_sum + kd
    r_o[...] = r
    kk_o[...] = kk
    v_o[...] = v
    bonus_o[...] = _dot_exact_rhs(r * (k_sum * 0.5) * rk_ref[...], ones) * v


def _rw_prep(z_big, zs_big, z_small, zs_small, lp, *, c_rw, head, tm, col0):
    n = z_big.shape[0]
    tn = RW_PACK * head
    nj = c_rw // tn
    jb = col0 // tn
    n_small_all = z_small.shape[1]
    n_small = zs_small.shape[1]
    lora_w = lp["rw_w_up"].shape[1]
    lora_a = lp["rw_a_up"].shape[1]
    lora_g = lp["rw_g_up"].shape[0]
    mu = lp["rw_mu"].reshape(1, -1)
    mu_big = mu[:, :3 * c_rw]
    mu_small = mu[:, 3 * c_rw:]
    row = lambda a: a.reshape(1, c_rw)
    tile = lambda off: pl.BlockSpec((tm, tn), lambda i, j: (i, j + off))
    vec = lambda off: pl.BlockSpec((1, tn), lambda i, j: (0, j + off))
    full = lambda c: pl.BlockSpec((tm, c), lambda i, j: (i, 0))
    dirvec = pl.BlockSpec((2, 1, tn), lambda i, j: (0, 0, j))
    out_t = pl.BlockSpec((tm, tn), lambda i, j: (i, j))
    out_d = pl.BlockSpec((2, tm, tn), lambda i, j: (0, i, j))
    sds = jax.ShapeDtypeStruct
    kern = functools.partial(_rw_prep_kernel, lora_w=lora_w, lora_a=lora_a, small0=n_small_all - n_small)
    return pl.pallas_call(
        kern,
        grid=(n // tm, nj),
        in_specs=[
            tile(jb), tile(jb + nj), tile(jb + 2 * nj),
            tile(0), tile(nj), tile(2 * nj),
            full(n_small_all), full(n_small),
            vec(0), vec(nj), vec(2 * nj),
            pl.BlockSpec((1, n_small), lambda i, j: (0, 0)),
            dirvec,
            pl.BlockSpec((2, lora_w, tn), lambda i, j: (0, 0, j)),
            dirvec,
            pl.BlockSpec((2, lora_a, tn), lambda i, j: (0, 0, j)),
            pl.BlockSpec((lora_g, tn), lambda i, j: (0, j)),
            vec(0), vec(0), vec(0),
            pl.BlockSpec((tn, tn), lambda i, j: (0, 0)),
        ],
        out_specs=[out_t, out_t, out_t, out_d, out_d, out_d, out_t, out_t],
        out_shape=[sds((n, c_rw), F32), sds((n, c_rw), F32), sds((n, c_rw), F32),
                   sds((2, n, c_rw), F32), sds((2, n, c_rw), F32), sds((2, n, c_rw), F32),
                   sds((n, c_rw), F32), sds((n, c_rw), F32)],
        compiler_params=_params(("arbitrary", "arbitrary")),
        name="rwkv_prep",
    )(z_big, z_big, z_big, zs_big, zs_big, zs_big, z_small, zs_small,
      mu_big, mu_big, mu_big, mu_small,
      lp["rw_w0"].reshape(2, 1, c_rw), lp["rw_w_up"], lp["rw_a0"].reshape(2, 1, c_rw), lp["rw_a_up"],
      lp["rw_g_up"].astype(BF16), row(lp["rw_k_k"]), row(lp["rw_k_a"]), row(lp["rw_r_k"]),
      _head_ones(tn, head))


def _rw_scan_kernel(r_ref, kk_ref, v_ref, lw_ref, kd_ref, beta_ref, y_ref, st_scr, *, head):
    t_len, width = r_ref.shape
    n_pack = width // head
    rev = pl.program_id(0) == 1

    @pl.when(pl.program_id(3) == 0)
    def _():
        st_scr[...] = jnp.zeros_like(st_scr)

    lane_head = lax.broadcasted_iota(jnp.int32, (1, width), 1) // head
    head_masks = [lane_head == h for h in range(n_pack)]

    def block_diag(x):
        xb = x.astype(BF16)
        zero = jnp.zeros_like(xb)
        return jnp.concatenate([jnp.where(m, xb, zero) for m in head_masks], axis=0)

    def mm(a, bd):
        return _dot(a.astype(BF16), bd)

    def mm_nt(a, bd):
        return lax.dot_general(a.astype(BF16), bd, (((1,), (1,)), ((), ())), preferred_element_type=F32)

    def diag_blocks_tn(a, b):
        full = lax.dot_general(a.astype(BF16), b.astype(BF16), (((0,), (0,)), ((), ())),
                               preferred_element_type=F32)
        out = jnp.where(head_masks[0], full[0:head], 0.0)
        for h in range(1, n_pack):
            out = out + jnp.where(head_masks[h], full[h * head:(h + 1) * head], 0.0)
        return out

    sgn = jnp.where(rev, -1, 1)
    tt = lax.broadcasted_iota(jnp.int32, (t_len, width), 0)
    ss = lax.broadcasted_iota(jnp.int32, (t_len, width), 1) % head
    order = (tt - ss) * sgn
    strict = order > 0
    incl = order >= 0
    eye = jnp.where(tt == ss, 1.0, 0.0)
    ti = lax.broadcasted_iota(jnp.int32, (t_len, t_len), 0)
    si = lax.broadcasted_iota(jnp.int32, (t_len, t_len), 1)
    tri = jnp.where((ti - si) * sgn >= 0, 1.0, 0.0).astype(BF16)

    r = r_ref[...]
    kk = kk_ref[...]
    v = v_ref[...]
    lw = lw_ref[...]
    kd = kd_ref[...]
    beta = beta_ref[...]

    lw_h, lw_l = _split(lw)
    log_p = _dot(tri, lw_h) + _dot(tri, lw_l)
    log_pt = jnp.sum(lw, axis=0, keepdims=True)
    p_inv = jnp.exp(-log_p)
    a_bar = kk * jnp.exp(log_p - lw)
    r_bar = r * jnp.exp(log_p)
    b_til = beta * p_inv
    k_til = kd * p_inv
    p_rest = jnp.exp(log_pt - log_p)
    b_end = beta * p_rest
    k_end = kd * p_rest

    ar = jnp.concatenate([a_bar, r_bar], axis=0)
    gram_b = mm_nt(ar, block_diag(b_til))
    gram_k = mm_nt(ar, block_diag(k_til))
    l_b = jnp.where(strict, gram_b[:t_len], 0.0)
    m_b = jnp.where(incl, gram_b[t_len:], 0.0)
    l_k = jnp.where(strict, gram_k[:t_len], 0.0)
    m_k = jnp.where(incl, gram_k[t_len:], 0.0)

    t_inv = eye - l_b
    l_pow = l_b
    span = 2
    while span < t_len:
        l_pow = mm(l_pow, block_diag(l_pow))
        t_inv = t_inv + mm(t_inv, block_diag(l_pow))
        span *= 2

    bd_v = block_diag(v)
    w_til = mm(t_inv, block_diag(a_bar))
    u_til = mm(t_inv, block_diag(mm(l_k, bd_v)))
    q_hat = r_bar - mm(m_b, block_diag(w_til))
    y_hat = mm(m_k, bd_v) - mm(m_b, block_diag(u_til))
    g_mat = eye * jnp.exp(log_pt) - diag_blocks_tn(b_end, w_til)
    h_mat = diag_blocks_tn(jnp.concatenate([k_end, b_end], axis=0), jnp.concatenate([v, -u_til], axis=0))

    bd_s = block_diag(st_scr[...])
    y_ref[...] = mm(q_hat, bd_s) + y_hat
    st_scr[...] = mm(g_mat, bd_s) + h_mat


def _rw_scan(r, kk, v, lw, kd, beta, *, bsz, l_ctx, l_lat, head):
    n, c_rw = r.shape
    t_len = RW_CHUNK
    width = RW_PACK * head
    nc = l_ctx // t_len
    nl = l_lat // t_len
    ctx_blocks = bsz * nc

    def blk(d, b, c):
        cc = jnp.where(d == 1, nc - 1 - c, c)
        cl = jnp.where(d == 1, nl - 1 - (c - nc), c - nc)
        return jnp.where(c < nc, b * nc + cc, ctx_blocks + b * nl + cl)

    shared = pl.BlockSpec((t_len, width), lambda d, b, g, c: (blk(d, b, c), g))
    per_dir = pl.BlockSpec((None, t_len, width), lambda d, b, g, c: (d, blk(d, b, c), g))
    kern = functools.partial(_rw_scan_kernel, head=head)
    return pl.pallas_call(
        kern,
        grid=(2, bsz, c_rw // width, nc + nl),
        in_specs=[shared, shared, shared, per_dir, per_dir, per_dir],
        out_specs=per_dir,
        out_shape=jax.ShapeDtypeStruct((2, n, c_rw), F32),
        scratch_shapes=[pltpu.VMEM((head, width), F32)],
        compiler_params=_params(("arbitrary", "arbitrary", "arbitrary", "arbitrary")),
        name="rwkv_scan",
    )(r, kk, v, lw, kd, beta)


def _rw_readout_kernel(y_ref, g_ref, bonus_ref, lnw_ref, lnb_ref, ones_ref, o_ref, *, head):
    y = y_ref[0] + y_ref[1]
    ones = ones_ref[...]
    inv = 1.0 / head
    mu = _dot_exact_rhs(y, ones) * inv
    yc = y - mu
    var = _dot_exact_rhs(yc * yc, ones) * inv
    yn = yc * lax.rsqrt(var + GN_EPS) * lnw_ref[...] + lnb_ref[...]
    o_ref[...] = ((yn + bonus_ref[...]) * g_ref[...]).astype(o_ref.dtype)


def _rw_readout(y2, g, bonus, ln_w, ln_b, *, head, tm):
    _, n, c_rw = y2.shape
    tn = RW_PACK * head
    tile = pl.BlockSpec((tm, tn), lambda i, j: (i, j))
    vec = pl.BlockSpec((1, tn), lambda i, j: (0, j))
    kern = functools.partial(_rw_readout_kernel, head=head)
    return pl.pallas_call(
        kern,
        grid=(n // tm, c_rw // tn),
        in_specs=[pl.BlockSpec((2, tm, tn), lambda i, j: (0, i, j)), tile, tile, vec, vec,
                  pl.BlockSpec((tn, tn), lambda i, j: (0, 0))],
        out_specs=tile,
        out_shape=jax.ShapeDtypeStruct((n, c_rw), BF16),
        compiler_params=_params(("arbitrary", "arbitrary")),
        name="rwkv_readout",
    )(y2, g, bonus, ln_w.reshape(1, c_rw), ln_b.reshape(1, c_rw), _head_ones(tn, head))


def _merge_kernel(ya_ref, yb_ref, gz_ref, wa_ref, wb_ref, ga_ref, gb_ref, ba_ref, bb_ref, o_ref):
    gz = gz_ref[...].astype(BF16)
    ga = _sigmoid(_dot(gz, ga_ref[...]) + ba_ref[...])
    gb = _sigmoid(_dot(gz, gb_ref[...]) + bb_ref[...])
    o_ref[...] = (ga * _dot(ya_ref[...], wa_ref[...]) + gb * _dot(yb_ref[...], wb_ref[...])).astype(o_ref.dtype)


def _merge(ya, yb, z_small, wa, wb, gate_up, gate_b, *, tm, tn, row0, gz_block):
    n, ca = ya.shape
    cb = yb.shape[1]
    d = wa.shape[1]
    rank = gate_up.shape[1]
    b0 = row0 // tm
    return pl.pallas_call(
        _merge_kernel,
        grid=((n - row0) // tm, d // tn),
        in_specs=[
            pl.BlockSpec((tm, ca), lambda i, j: (i + b0, 0)),
            pl.BlockSpec((tm, cb), lambda i, j: (i + b0, 0)),
            pl.BlockSpec((tm, rank), lambda i, j: (i + b0, gz_block)),
            pl.BlockSpec((ca, tn), lambda i, j: (0, j)),
            pl.BlockSpec((cb, tn), lambda i, j: (0, j)),
            pl.BlockSpec((None, rank, tn), lambda i, j: (0, 0, j)),
            pl.BlockSpec((None, rank, tn), lambda i, j: (1, 0, j)),
            pl.BlockSpec((None, 1, tn), lambda i, j: (0, 0, j)),
            pl.BlockSpec((None, 1, tn), lambda i, j: (1, 0, j)),
        ],
        out_specs=pl.BlockSpec((tm, tn), lambda i, j: (i + b0, j)),
        out_shape=jax.ShapeDtypeStruct((n, d), BF16),
        compiler_params=_params(("arbitrary", "arbitrary")),
        name="merge_branches",
    )(ya, yb, z_small, wa, wb, gate_up, gate_up, gate_b.reshape(2, 1, d), gate_b.reshape(2, 1, d))


def _final_norm_kernel(h_ref, gain_ref, o_ref):
    x = h_ref[...]
    ms = jnp.mean(x * x, axis=-1, keepdims=True)
    o_ref[...] = x * lax.rsqrt(ms + NORM_EPS) * gain_ref[...]


def _final_norm(h, gain, *, tm, row0):
    n, d = h.shape
    b0 = row0 // tm
    return pl.pallas_call(
        _final_norm_kernel,
        grid=((n - row0) // tm,),
        in_specs=[pl.BlockSpec((tm, d), lambda i: (i + b0, 0)), pl.BlockSpec((1, d), lambda i: (0, 0))],
        out_specs=pl.BlockSpec((tm, d), lambda i: (i, 0)),
        out_shape=jax.ShapeDtypeStruct((n - row0, d), F32),
        compiler_params=_params(("arbitrary",)),
        name="final_norm",
    )(h, gain.reshape(1, d))


def _grid_qshift(z):
    bsz, length, ch = z.shape
    rows = length // GRID_W
    g = z.reshape(bsz, rows, GRID_W, ch)
    q = ch // 4
    col_pad = ((0, 0), (0, 0), (1, 1), (0, 0))
    row_pad = ((0, 0), (1, 1), (0, 0), (0, 0))
    left = jnp.pad(g[..., :q], col_pad)[:, :, :-2]
    right = jnp.pad(g[..., q:2 * q], col_pad)[:, :, 2:]
    up = jnp.pad(g[..., 2 * q:3 * q], row_pad)[:, :-2]
    down = jnp.pad(g[..., 3 * q:], row_pad)[:, 2:]
    return jnp.concatenate([left, right, up, down], axis=-1).reshape(bsz, length, ch)


def _seq_shift(z):
    h = z.shape[-1] // 2
    prev = jnp.pad(z[..., :h], ((0, 0), (1, 0), (0, 0)))[:, :-1]
    nxt = jnp.pad(z[..., h:], ((0, 0), (0, 1), (0, 0)))[:, 1:]
    return jnp.concatenate([prev, nxt], axis=-1)


def _shifted(rz, bsz, l_ctx, l_lat):
    n_ctx = bsz * l_ctx
    c = rz.shape[1]
    sc = _seq_shift(rz[:n_ctx].reshape(bsz, l_ctx, c)).reshape(n_ctx, c)
    sl = _grid_qshift(rz[n_ctx:].reshape(bsz, l_lat, c)).reshape(bsz * l_lat, c)
    return jnp.concatenate([sc, sl], axis=0)


def _to_s5_layout(u, bsz, l_ctx, l_lat, groups, gch):
    t = S5_CHUNK
    n_ctx = bsz * l_ctx
    uc = u[:n_ctx].reshape(bsz, l_ctx // t, t, groups, gch)
    ul = u[n_ctx:].reshape(bsz, l_lat // t, t, groups, gch)
    x = jnp.concatenate([uc, ul], axis=1).astype(BF16)
    x = jnp.transpose(x, (3, 1, 0, 2, 4))
    return x.reshape(groups, -1, t * gch)


def _from_s5_layout(y, bsz, l_ctx, l_lat, groups, gch):
    t = S5_CHUNK
    chunks = (l_ctx + l_lat) // t
    x = y.reshape(groups, chunks, bsz, t, gch)
    x = jnp.transpose(x, (2, 1, 3, 0, 4))
    nc = l_ctx // t
    yc = x[:, :nc].reshape(bsz * l_ctx, groups * gch)
    yl = x[:, nc:].reshape(bsz * l_lat, groups * gch)
    return jnp.concatenate([yc, yl], axis=0)


def kernel(x, c, ctx, c_ctx, ada_down, ada_up, ada_b, norm1, w_in, s5_a_re, s5_a_im, s5_log_dt, s5_b_re, s5_b_im, s5_c_re, s5_c_im, s5_d, s5_glu_w, s5_glu_b, rw_mu, rw_w0, rw_w_up, rw_a0, rw_a_up, rw_g_up, rw_k_k, rw_k_a, rw_r_k, rw_ln_w, rw_ln_b, w_proj_a, w_proj_b, gate_up, gate_b, w_out, norm2, mlp_w1, mlp_w2, norm_f):
    bsz, l_lat, d = x.shape
    l_ctx = ctx.shape[1]
    depth = w_in.shape[0]
    n_ctx = bsz * l_ctx
    n_lat = bsz * l_lat
    s5_groups, s5_state = s5_a_re.shape[2], s5_a_re.shape[3]
    s5_gch = s5_b_re.shape[-1]
    c_s5 = s5_groups * s5_gch
    rw_heads, head = rw_r_k.shape[1], rw_r_k.shape[2]
    c_rw = rw_heads * head
    lora_w, lora_a, lora_g = rw_w_up.shape[2], rw_a_up.shape[2], rw_g_up.shape[1]
    gate_rank = gate_up.shape[2]
    n_small_rw = 2 * lora_w + 2 * lora_a + lora_g
    c_big = c_s5 + 3 * c_rw
    assert w_in.shape[2] == c_big + n_small_rw + gate_rank
    assert bsz % 8 == 0 and l_ctx % RW_CHUNK == 0 and l_lat % RW_CHUNK == 0 and l_lat % GRID_W == 0
    assert s5_gch * S5_CHUNK == 256 and RW_PACK * head == 256 and c_rw % 256 == 0 and gate_rank % 128 == 0
    assert c_s5 % (RW_PACK * head) == 0

    tm = _pick(n_ctx, 512)
    tm = _pick(l_lat, tm)

    def row_of_block(i):
        ctx_blocks = n_ctx // tm
        return jnp.where(i < ctx_blocks, bsz, (i - ctx_blocks) // (l_lat // tm))

    n_cond = ((bsz + 1 + 7) // 8) * 8
    cond = jnp.zeros((n_cond, d), F32).at[:bsz].set(c).at[bsz].set(c_ctx)
    mods_all = _ada_modulation(cond, ada_down, ada_up, ada_b)
    mods_all = jnp.transpose(mods_all.reshape(depth, n_cond, 6, 1, d), (0, 2, 1, 3, 4))

    h = jnp.concatenate([ctx.reshape(n_ctx, d), x.reshape(n_lat, d)], axis=0)

    for l in range(depth):
        last = l == depth - 1
        row0 = n_ctx if last else 0
        mods = mods_all[l]
        lp = {"rw_mu": rw_mu[l], "rw_w0": rw_w0[l], "rw_w_up": rw_w_up[l], "rw_a0": rw_a0[l],
              "rw_a_up": rw_a_up[l], "rw_g_up": rw_g_up[l], "rw_k_k": rw_k_k[l], "rw_k_a": rw_k_a[l],
              "rw_r_k": rw_r_k[l]}
        w_big = w_in[l, :, :c_big].astype(BF16)
        w_small = jnp.concatenate([w_in[l, :, c_big + n_small_rw:], w_in[l, :, c_big:c_big + n_small_rw]],
                                  axis=1).astype(BF16)
        z_big = _norm_matmul(h, norm1[l], mods, 0, w_big, row_of_block, tm=tm, tn=_pick(c_big, 512), row0=0,
                             out_dtype=F32, sq_relu=False, name="in_proj_big")
        z_small = _norm_matmul(h, norm1[l], mods, 0, w_small, row_of_block, tm=tm, tn=w_small.shape[1], row0=0,
                               out_dtype=F32, sq_relu=False, name="in_proj_small")

        km, bm, cm, lam = _s5_matrices(s5_a_re[l], s5_a_im[l], s5_log_dt[l], s5_b_re[l], s5_b_im[l],
                                       s5_c_re[l], s5_c_im[l], s5_d[l])
        u_g = _to_s5_layout(z_big[:, :c_s5], bsz, l_ctx, l_lat, s5_groups, s5_gch)
        y_g = _s5_scan(u_g, km, bm, cm, lam, n_ctx=l_ctx // S5_CHUNK, n_chunks=(l_ctx + l_lat) // S5_CHUNK, bsz=bsz)
        y_s5 = _from_s5_layout(y_g, bsz, l_ctx, l_lat, s5_groups, s5_gch)
        ya = _s5_glu(y_s5, s5_glu_w[l].astype(BF16), s5_glu_b[l], tm=tm)

        rz = jnp.concatenate([z_big[:, c_s5:], z_small[:, gate_rank:]], axis=1)
        rzs = _shifted(rz, bsz, l_ctx, l_lat)
        r, kk, v, lw, kd, beta, g, bonus = _rw_prep(z_big, rzs[:, :3 * c_rw], z_small, rzs[:, 3 * c_rw:], lp,
                                                    c_rw=c_rw, head=head, tm=_pick(tm, 256), col0=c_s5)
        y2 = _rw_scan(r, kk, v, lw, kd, beta, bsz=bsz, l_ctx=l_ctx, l_lat=l_lat, head=head)
        yb = _rw_readout(y2, g, bonus, rw_ln_w[l], rw_ln_b[l], head=head, tm=tm)

        mixed = _merge(ya, yb, z_small, w_proj_a[l].astype(BF16), w_proj_b[l].astype(BF16),
                       gate_up[l].astype(BF16), gate_b[l], tm=tm, tn=_pick(d, 512), row0=row0,
                       gz_block=0)
        h = _res_matmul(mixed, w_out[l].astype(BF16), h, mods, 2, row_of_block, tm=tm, tk=_pick(d, 512),
                        row0=row0, name="out_proj")
        hid = _norm_matmul(h, norm2[l], mods, 3, mlp_w1[l].astype(BF16), row_of_block, tm=tm,
                           tn=_pick(mlp_w1.shape[2], 512), row0=row0, out_dtype=BF16, sq_relu=True, name="mlp_up")
        h = _res_matmul(hid, mlp_w2[l].astype(BF16), h, mods, 5, row_of_block, tm=tm, tk=_pick(mlp_w2.shape[1], 512),
                        row0=row0, name="mlp_down")

    out = _final_norm(h, norm_f, tm=tm, row0=n_ctx)
    return out.reshape(bsz, l_lat, d)
```

```python
import functools

import jax
import jax.numpy as jnp
from jax import lax
from jax.experimental import pallas as pl
from jax.experimental.pallas import tpu as pltpu

F32 = jnp.float32
BF16 = jnp.bfloat16

GRID_W = 64
NORM_EPS = 1e-6
GN_EPS = 64e-5
A_RE_MAX = -1e-4
S5_CHUNK = 16
RW_CHUNK = 64
RW_PACK = 4
RW_UNITS = 4
TM = 512
VMEM_LIMIT = 56 * 1024 * 1024


def _params(sem):
    return pltpu.CompilerParams(dimension_semantics=sem, vmem_limit_bytes=VMEM_LIMIT)


def _pick(n, pref):
    t = pref
    while n % t:
        t //= 2
    return t


def _dot(a, b):
    return jnp.dot(a, b, preferred_element_type=F32)


def _split(x):
    hi = x.astype(BF16)
    lo = (x - hi.astype(F32)).astype(BF16)
    return hi, lo


def _dot3(a, b):
    ah, al = _split(a)
    bh, bl = _split(b)
    return _dot(ah, bh) + (_dot(ah, bl) + _dot(al, bh))


def _dot_exact_rhs(a, b_bf16):
    ah, al = _split(a)
    return _dot(ah, b_bf16) + _dot(al, b_bf16)


def _sigmoid(x):
    return 1.0 / (1.0 + jnp.exp(-x))


def _gelu_tanh(x):
    c = 0.7978845608028654
    return 0.5 * x * (1.0 + jnp.tanh(c * (x + 0.044715 * (x * x * x))))


def _ada_kernel(cond_ref, wd_ref, wu_ref, b_ref, o_ref, mid_scr):
    @pl.when(pl.program_id(1) == 0)
    def _():
        c = cond_ref[...]
        mid_scr[...] = _dot3(c * _sigmoid(c), wd_ref[...])

    o_ref[...] = _dot3(mid_scr[...], wu_ref[...]) + b_ref[...]


def _ada_modulation(cond, w_down, w_up, bias):
    nl, d, rank = w_down.shape
    r = cond.shape[0]
    n_out = w_up.shape[-1]
    tn = _pick(n_out, 2048)
    return pl.pallas_call(
        _ada_kernel,
        grid=(nl, n_out // tn),
        in_specs=[
            pl.BlockSpec((r, d), lambda l, j: (0, 0)),
            pl.BlockSpec((None, d, rank), lambda l, j: (l, 0, 0)),
            pl.BlockSpec((None, rank, tn), lambda l, j: (l, 0, j)),
            pl.BlockSpec((None, 1, tn), lambda l, j: (l, 0, j)),
        ],
        out_specs=pl.BlockSpec((None, r, tn), lambda l, j: (l, 0, j)),
        out_shape=jax.ShapeDtypeStruct((nl, r, n_out), F32),
        scratch_shapes=[pltpu.VMEM((r, rank), F32)],
        compiler_params=_params(("arbitrary", "arbitrary")),
        name="ada_modulation",
    )(cond, w_down, w_up, bias.reshape(nl, 1, n_out))


def _norm_matmul_kernel(h_ref, gain_ref, shift_ref, scale_ref, w_ref, o_ref, n_scr):
    @pl.when(pl.program_id(1) == 0)
    def _():
        x = h_ref[...]
        ms = jnp.mean(x * x, axis=-1, keepdims=True)
        xn = x * lax.rsqrt(ms + NORM_EPS) * gain_ref[...]
        n_scr[...] = (xn * (1.0 + scale_ref[...]) + shift_ref[...]).astype(BF16)

    o_ref[...] = _dot(n_scr[...], w_ref[...]).astype(o_ref.dtype)


def _norm_matmul(h, gain, mods, w, row_of_block, *, tm, tn, out_dtype, name):
    n, d = h.shape
    n_out = w.shape[1]
    mod = lambda which: pl.BlockSpec((None, None, 1, d), lambda i, j: (which, row_of_block(i), 0, 0))
    return pl.pallas_call(
        _norm_matmul_kernel,
        grid=(n // tm, n_out // tn),
        in_specs=[
            pl.BlockSpec((tm, d), lambda i, j: (i, 0)),
            pl.BlockSpec((1, d), lambda i, j: (0, 0)),
            mod(0), mod(1),
            pl.BlockSpec((d, tn), lambda i, j: (0, j)),
        ],
        out_specs=pl.BlockSpec((tm, tn), lambda i, j: (i, j)),
        out_shape=jax.ShapeDtypeStruct((n, n_out), out_dtype),
        scratch_shapes=[pltpu.VMEM((tm, d), BF16)],
        compiler_params=_params(("arbitrary", "arbitrary")),
        name=name,
    )(h, gain.reshape(1, d), mods, mods, w)


def _mlp_kernel(h_ref, gain_ref, shift_ref, scale_ref, gate_ref, w1_ref, w2_ref, o_ref, n_scr):
    j = pl.program_id(1)

    @pl.when(j == 0)
    def _():
        x = h_ref[...]
        ms = jnp.mean(x * x, axis=-1, keepdims=True)
        xn = x * lax.rsqrt(ms + NORM_EPS) * gain_ref[...]
        n_scr[...] = (xn * (1.0 + scale_ref[...]) + shift_ref[...]).astype(BF16)
        o_ref[...] = jnp.zeros_like(o_ref)

    hid = jnp.square(jnp.maximum(_dot(n_scr[...], w1_ref[...]), 0.0)).astype(BF16)
    o_ref[...] += _dot(hid, w2_ref[...])

    @pl.when(j == pl.num_programs(1) - 1)
    def _():
        o_ref[...] = h_ref[...] + gate_ref[...] * o_ref[...]


def _mlp(h, gain, mods, w1, w2, row_of_block, *, tm, row0):
    n, d = h.shape
    d_ff = w1.shape[1]
    tf = _pick(d_ff, 512)
    b0 = row0 // tm
    mod = lambda which: pl.BlockSpec((None, None, 1, d), lambda i, j: (which, row_of_block(i + b0), 0, 0))
    return pl.pallas_call(
        _mlp_kernel,
        grid=((n - row0) // tm, d_ff // tf),
        in_specs=[
            pl.BlockSpec((tm, d), lambda i, j: (i + b0, 0), pipeline_mode=pl.Buffered(1)),
            pl.BlockSpec((1, d), lambda i, j: (0, 0)),
            mod(3), mod(4), mod(5),
            pl.BlockSpec((d, tf), lambda i, j: (0, j)),
            pl.BlockSpec((tf, d), lambda i, j: (j, 0)),
        ],
        out_specs=pl.BlockSpec((tm, d), lambda i, j: (i + b0, 0)),
        out_shape=jax.ShapeDtypeStruct((n, d), F32),
        scratch_shapes=[pltpu.VMEM((tm, d), BF16)],
        input_output_aliases={0: 0},
        compiler_params=_params(("arbitrary", "arbitrary")),
        name="mlp",
    )(h, gain.reshape(1, d), mods, mods, mods, w1, w2)


def _merge_out_kernel(ya_ref, yb_ref, gz_ref, wa_ref, wb_ref, ga_ref, gb_ref, ba_ref, bb_ref, wo_ref, h_ref,
                      gate_ref, o_ref):
    j = pl.program_id(1)

    @pl.when(j == 0)
    def _():
        o_ref[...] = jnp.zeros_like(o_ref)

    gz = gz_ref[...].astype(BF16)
    ga = _sigmoid(_dot(gz, ga_ref[...]) + ba_ref[...])
    gb = _sigmoid(_dot(gz, gb_ref[...]) + bb_ref[...])
    mixed = (ga * _dot(ya_ref[...], wa_ref[...]) + gb * _dot(yb_ref[...], wb_ref[...])).astype(BF16)
    o_ref[...] += _dot(mixed, wo_ref[...])

    @pl.when(j == pl.num_programs(1) - 1)
    def _():
        o_ref[...] = h_ref[...] + gate_ref[...] * o_ref[...]


def _merge_out(ya, yb, z_small, wa, wb, gate_up, gate_b, w_out, h, mods, row_of_block, *, tm, row0):
    n, d = h.shape
    ca, cb = ya.shape[1], yb.shape[1]
    d_mid = wa.shape[1]
    rank = gate_up.shape[1]
    tn = _pick(d_mid, 512)
    b0 = row0 // tm
    once = pl.Buffered(1)
    return pl.pallas_call(
        _merge_out_kernel,
        grid=((n - row0) // tm, d_mid // tn),
        in_specs=[
            pl.BlockSpec((tm, ca), lambda i, j: (i + b0, 0), pipeline_mode=once),
            pl.BlockSpec((tm, cb), lambda i, j: (i + b0, 0), pipeline_mode=once),
            pl.BlockSpec((tm, rank), lambda i, j: (i + b0, 0)),
            pl.BlockSpec((ca, tn), lambda i, j: (0, j)),
            pl.BlockSpec((cb, tn), lambda i, j: (0, j)),
            pl.BlockSpec((None, rank, tn), lambda i, j: (0, 0, j)),
            pl.BlockSpec((None, rank, tn), lambda i, j: (1, 0, j)),
            pl.BlockSpec((None, 1, tn), lambda i, j: (0, 0, j)),
            pl.BlockSpec((None, 1, tn), lambda i, j: (1, 0, j)),
            pl.BlockSpec((tn, d), lambda i, j: (j, 0)),
            pl.BlockSpec((tm, d), lambda i, j: (i + b0, 0), pipeline_mode=once),
            pl.BlockSpec((None, None, 1, d), lambda i, j: (2, row_of_block(i + b0), 0, 0)),
        ],
        out_specs=pl.BlockSpec((tm, d), lambda i, j: (i + b0, 0)),
        out_shape=jax.ShapeDtypeStruct((n, d), F32),
        input_output_aliases={10: 0},
        compiler_params=_params(("arbitrary", "arbitrary")),
        name="merge_out_proj",
    )(ya, yb, z_small, wa, wb, gate_up, gate_up, gate_b.reshape(2, 1, d_mid), gate_b.reshape(2, 1, d_mid),
      w_out, h, mods)


def _s5_matrices(a_re, a_im, log_dt, b_re, b_im, c_re, c_im, d_skip):
    t_len = S5_CHUNK
    hp = lax.Precision.HIGHEST
    lam_re = jnp.minimum(a_re.astype(F32), A_RE_MAX)
    lam_im = a_im.astype(F32)
    dt = jnp.exp(log_dt.astype(F32))[..., None]
    j = jnp.arange(t_len + 1, dtype=F32)[:, None, None, None]
    mag = jnp.exp(j * (lam_re * dt))
    ang = j * (lam_im * dt)
    pw_re = mag * jnp.cos(ang)
    pw_im = mag * jnp.sin(ang)
    nr, ni = pw_re[1] - 1.0, pw_im[1]
    den = lam_re * lam_re + lam_im * lam_im
    qr = (nr * lam_re + ni * lam_im) / den
    qi = (ni * lam_re - nr * lam_im) / den
    bbr = qr[..., None] * b_re - qi[..., None] * b_im
    bbi = qr[..., None] * b_im + qi[..., None] * b_re
    n_dir, g, p, c = bbr.shape
    clr = c_re[None] * pw_re[:t_len, :, :, None, :] - c_im[None] * pw_im[:t_len, :, :, None, :]
    cli = c_re[None] * pw_im[:t_len, :, :, None, :] + c_im[None] * pw_re[:t_len, :, :, None, :]
    kj = (jnp.einsum("jdgop,dgpi->jdgoi", clr, bbr, precision=hp)
          - jnp.einsum("jdgop,dgpi->jdgoi", cli, bbi, precision=hp))
    s_idx = jnp.arange(t_len)[:, None]
    t_idx = jnp.arange(t_len)[None, :]
    lag_f = t_idx - s_idx
    lag_b = s_idx - t_idx

    def toeplitz(kd, lag):
        m = kd[jnp.clip(lag, 0, t_len - 1)]
        m = jnp.where((lag >= 0)[:, :, None, None, None], m, 0.0)
        return jnp.transpose(m, (2, 0, 4, 1, 3))

    km = toeplitz(kj[:, 0], lag_f) + toeplitz(kj[:, 1], lag_b)
    eye_t = jnp.eye(t_len, dtype=F32)
    eye_c = jnp.eye(c, dtype=F32)
    skip = d_skip.astype(F32).reshape(g, c)
    km = km + (eye_t[None, :, None, :, None] * eye_c[None, None, :, None, :]) * skip[:, None, :, None, None]
    km = km.reshape(g, t_len * c, t_len * c)
    sf = (t_len - 1) - jnp.arange(t_len)
    sb = jnp.arange(t_len)

    def state_in(d, e):
        pr, pi = pw_re[e, d], pw_im[e, d]
        re = pr[..., None] * bbr[d][None] - pi[..., None] * bbi[d][None]
        im = pr[..., None] * bbi[d][None] + pi[..., None] * bbr[d][None]
        tr = lambda z: jnp.transpose(z, (1, 0, 3, 2)).reshape(g, t_len * c, p)
        return tr(re), tr(im)

    fr, fi = state_in(0, sf)
    br, bi = state_in(1, sb)
    bm = jnp.concatenate([fr, br, fi, bi], axis=-1)
    ef = jnp.arange(t_len) + 1
    eb = t_len - jnp.arange(t_len)

    def state_out(d, e):
        pr, pi = pw_re[e, d], pw_im[e, d]
        re = c_re[d][None] * pr[:, :, None, :] - c_im[d][None] * pi[:, :, None, :]
        im = c_re[d][None] * pi[:, :, None, :] + c_im[d][None] * pr[:, :, None, :]
        tr = lambda z: jnp.transpose(z, (1, 3, 0, 2)).reshape(g, p, t_len * c)
        return tr(re), -tr(im)

    cfr, cfi = state_out(0, ef)
    cbr, cbi = state_out(1, eb)
    cm = jnp.concatenate([cfr, cbr, cfi, cbi], axis=1)
    lam = jnp.stack([jnp.concatenate([pw_re[t_len, 0], pw_re[t_len, 1]], axis=-1),
                     jnp.concatenate([pw_im[t_len, 0], pw_im[t_len, 1]], axis=-1)], axis=1)
    return km.astype(BF16), bm.astype(BF16), cm.astype(BF16), lam


def _s5_kernel(u_ref, bm_ref, km_ref, cm_ref, lam_ref, y_ref, xin_scr, fwd_scr, bwd_scr, *,
               n_ctx, n_chunks, bsz, p_state):
    u = u_ref[...]
    xin_scr[...] = _dot(u, bm_ref[...])
    two_p = 2 * p_state
    lam_r = jnp.broadcast_to(lam_ref[0:1, :], (bsz, two_p))
    lam_i = jnp.broadcast_to(lam_ref[1:2, :], (bsz, two_p))
    is_fwd = lax.broadcasted_iota(jnp.int32, (bsz, two_p), 1) < p_state

    def body(i, carry):
        re, im = carry
        pb = jnp.where(i < n_ctx, n_ctx - 1 - i, n_chunks + n_ctx - 1 - i)
        rf = pl.multiple_of(i * bsz, bsz)
        rb = pl.multiple_of(pb * bsz, bsz)
        xf = xin_scr[pl.ds(rf, bsz), :]
        xb = xin_scr[pl.ds(rb, bsz), :]
        x_re = jnp.where(is_fwd, xf[:, :two_p], xb[:, :two_p])
        x_im = jnp.where(is_fwd, xf[:, two_p:], xb[:, two_p:])
        state = jnp.concatenate([re, im], axis=1)
        fwd_scr[pl.ds(rf, bsz), :] = state
        bwd_scr[pl.ds(rb, bsz), :] = state
        return (re * lam_r - im * lam_i + x_re, re * lam_i + im * lam_r + x_im)

    zero = jnp.zeros((bsz, two_p), F32)
    lax.fori_loop(0, n_chunks, body, (zero, zero))
    lane = lax.broadcasted_iota(jnp.int32, (1, 2 * two_p), 1)
    fwd_lane = (lane % two_p) < p_state
    xs = jnp.where(fwd_lane, fwd_scr[...], bwd_scr[...]).astype(BF16)
    y_ref[...] = (_dot(u, km_ref[...]) + _dot(xs, cm_ref[...])).astype(y_ref.dtype)


def _s5_scan(u_g, km, bm, cm, lam, *, n_ctx, n_chunks, bsz):
    g, rows, tc = u_g.shape
    p4 = bm.shape[-1]
    kern = functools.partial(_s5_kernel, n_ctx=n_ctx, n_chunks=n_chunks, bsz=bsz, p_state=p4 // 4)
    return pl.pallas_call(
        kern,
        grid=(g,),
        in_specs=[
            pl.BlockSpec((None, rows, tc), lambda i: (i, 0, 0)),
            pl.BlockSpec((None, tc, p4), lambda i: (i, 0, 0)),
            pl.BlockSpec((None, tc, tc), lambda i: (i, 0, 0)),
            pl.BlockSpec((None, p4, tc), lambda i: (i, 0, 0)),
            pl.BlockSpec((None, 2, p4 // 2), lambda i: (i, 0, 0)),
        ],
        out_specs=pl.BlockSpec((None, rows, tc), lambda i: (i, 0, 0)),
        out_shape=jax.ShapeDtypeStruct((g, rows, tc), BF16),
        scratch_shapes=[pltpu.VMEM((rows, p4), F32), pltpu.VMEM((rows, p4), F32), pltpu.VMEM((rows, p4), F32)],
        compiler_params=_params(("arbitrary",)),
        name="s5_scan",
    )(u_g, bm, km, cm, lam)


def _glu_kernel(y_ref, w_ref, b_ref, o_ref):
    z = _gelu_tanh(y_ref[...].astype(F32))
    o_ref[...] = (z * _sigmoid(_dot(z.astype(BF16), w_ref[...]) + b_ref[...])).astype(o_ref.dtype)


def _s5_glu(y, w, b, *, tm):
    n, c = y.shape
    return pl.pallas_call(
        _glu_kernel,
        grid=(n // tm,),
        in_specs=[
            pl.BlockSpec((tm, c), lambda i: (i, 0)),
            pl.BlockSpec((c, c), lambda i: (0, 0)),
            pl.BlockSpec((1, c), lambda i: (0, 0)),
        ],
        out_specs=pl.BlockSpec((tm, c), lambda i: (i, 0)),
        out_shape=jax.ShapeDtypeStruct((n, c), BF16),
        compiler_params=_params(("arbitrary",)),
        name="s5_glu",
    )(y, w, b.reshape(1, c))


def _head_ones(width, head):
    i = jnp.arange(width) // head
    return (i[:, None] == i[None, :]).astype(BF16)


def _rw_prep_kernel(zr_ref, zrp_ref, zrn_ref, zk_ref, zkp_ref, zkn_ref, zv_ref, zvp_ref, zvn_ref,
                    zs_ref, zsp_ref, zsn_ref, mur_ref, muk_ref, muv_ref, mus_ref,
                    w0_ref, wup_ref, a0_ref, aup_ref, gup_ref, kk_ref, ka_ref, rk_ref, ones_ref,
                    r_o, kk_o, v_o, lw_o, kd_o, beta_o, g_o, bonus_o, *,
                    lora_w, lora_a, small0, n_ctx, l_ctx, l_lat, c_rw, quarter):
    tm, tn = zr_ref.shape
    halo = zrp_ref.shape[0]
    g0 = pl.program_id(0) * tm
    is_ctx = g0 < n_ctx
    pos = jnp.where(is_ctx, g0 % l_ctx, (g0 - n_ctx) % l_lat) + lax.broadcasted_iota(jnp.int32, (tm, 1), 0)
    row = lax.broadcasted_iota(jnp.int32, (tm, 1), 0)
    ok_m1 = jnp.where(is_ctx, pos, pos % GRID_W) >= 1
    ok_p1 = jnp.where(is_ctx, l_ctx - 1 - pos, GRID_W - 1 - pos % GRID_W) >= 1
    ok_mw = jnp.where(is_ctx, 0, pos) >= GRID_W
    ok_pw = jnp.where(is_ctx, 0, l_lat - GRID_W - pos) >= 1

    def mix(cur, prev, nxt, mu, col0):
        width = cur.shape[1]
        col = col0 + lax.broadcasted_iota(jnp.int32, (1, width), 1)
        cq = col // quarter
        cls = jnp.where(is_ctx, cq // 2, cq)
        m1 = jnp.where(row == 0, prev[halo - 1:halo, :], pltpu.roll(cur, 1, axis=0))
        p1 = jnp.where(row == tm - 1, nxt[0:1, :], pltpu.roll(cur, tm - 1, axis=0))
        mw = jnp.concatenate([prev, cur[:tm - halo]], axis=0)
        pw = jnp.concatenate([cur[halo:], nxt], axis=0)
        zs = jnp.where(cls == 0, jnp.where(ok_m1, m1, 0.0),
                       jnp.where(cls == 1, jnp.where(ok_p1, p1, 0.0),
                                 jnp.where(cls == 2, jnp.where(ok_mw, mw, 0.0), jnp.where(ok_pw, pw, 0.0))))
        return cur + (zs - cur) * mu

    j0 = pl.program_id(1) * tn
    r = mix(zr_ref[...], zrp_ref[...], zrn_ref[...], mur_ref[...], j0)
    k = mix(zk_ref[...], zkp_ref[...], zkn_ref[...], muk_ref[...], c_rw + j0)
    v = mix(zv_ref[...], zvp_ref[...], zvn_ref[...], muv_ref[...], 2 * c_rw + j0)
    sm = mix(zs_ref[:, small0:], zsp_ref[:, small0:], zsn_ref[:, small0:], mus_ref[...], 3 * c_rw)
    o_a = 2 * lora_w
    o_g = o_a + 2 * lora_a
    ones = ones_ref[...]
    g_o[...] = _dot(_sigmoid(sm[:, o_g:]).astype(BF16), gup_ref[...])
    kx = k * kk_ref[...]
    ss = _dot_exact_rhs(kx * kx, ones)
    kk = kx * lax.rsqrt(jnp.maximum(ss, 1e-24))
    k_sum = jnp.zeros_like(k)
    for d in range(2):
        wd = jnp.tanh(sm[:, d * lora_w:(d + 1) * lora_w])
        w = w0_ref[d] + _dot3(wd, wup_ref[d])
        w = -(jnp.maximum(-w, 0.0) + jnp.log(1.0 + jnp.exp(-jnp.abs(w)))) - 0.5
        lw_o[d] = -jnp.exp(w)
        ad = sm[:, o_a + d * lora_a:o_a + (d + 1) * lora_a]
        a = _sigmoid(a0_ref[d] + _dot3(ad, aup_ref[d]))
        kd = k * (1.0 + (a - 1.0) * ka_ref[...])
        kd_o[d] = kd
        beta_o[d] = kk * a
        k_sum = k_sum + kd
    r_o[...] = r
    kk_o[...] = kk
    v_o[...] = v
    bonus_o[...] = _dot_exact_rhs(r * (k_sum * 0.5) * rk_ref[...], ones) * v


def _rw_prep(z_rkv, z_small, lp, *, bsz, l_ctx, l_lat, c_rw, head, gate_rank):
    n = z_rkv.shape[0]
    tn = RW_PACK * head
    nj = c_rw // tn
    halo = GRID_W
    tm = _pick(l_lat, _pick(l_ctx, 512))
    assert tm % halo == 0 and tm > halo
    n_small_all = z_small.shape[1]
    n_small = n_small_all - gate_rank
    lora_w = lp["rw_w_up"].shape[1]
    lora_a = lp["rw_a_up"].shape[1]
    lora_g = lp["rw_g_up"].shape[0]
    mu = lp["rw_mu"].reshape(1, -1)
    assert mu.shape[1] == 3 * c_rw + n_small and mu.shape[1] % 4 == 0
    mu_big = mu[:, :3 * c_rw]
    mu_small = mu[:, 3 * c_rw:]
    row = lambda a: a.reshape(1, c_rw)
    hb = tm // halo
    last_halo = n // halo - 1

    def tiles(off, width_blocks):
        w = tn if width_blocks else n_small_all
        col = (lambda j: j + off) if width_blocks else (lambda j: 0)
        return [pl.BlockSpec((tm, w), lambda i, j: (i, col(j))),
                pl.BlockSpec((halo, w), lambda i, j: (jnp.maximum(i * hb - 1, 0), col(j))),
                pl.BlockSpec((halo, w), lambda i, j: (jnp.minimum((i + 1) * hb, last_halo), col(j)))]

    vec = lambda off: pl.BlockSpec((1, tn), lambda i, j: (0, j + off))
    dirvec = pl.BlockSpec((2, 1, tn), lambda i, j: (0, 0, j))
    out_t = pl.BlockSpec((tm, tn), lambda i, j: (i, j))
    out_d = pl.BlockSpec((2, tm, tn), lambda i, j: (0, i, j))
    sds = jax.ShapeDtypeStruct
    kern = functools.partial(_rw_prep_kernel, lora_w=lora_w, lora_a=lora_a, small0=gate_rank, n_ctx=bsz * l_ctx,
                             l_ctx=l_ctx, l_lat=l_lat, c_rw=c_rw, quarter=mu.shape[1] // 4)
    return pl.pallas_call(
        kern,
        grid=(n // tm, nj),
        in_specs=[
            *tiles(0, True), *tiles(nj, True), *tiles(2 * nj, True), *tiles(0, False),
            vec(0), vec(nj), vec(2 * nj),
            pl.BlockSpec((1, n_small), lambda i, j: (0, 0)),
            dirvec,
            pl.BlockSpec((2, lora_w, tn), lambda i, j: (0, 0, j)),
            dirvec,
            pl.BlockSpec((2, lora_a, tn), lambda i, j: (0, 0, j)),
            pl.BlockSpec((lora_g, tn), lambda i, j: (0, j)),
            vec(0), vec(0), vec(0),
            pl.BlockSpec((tn, tn), lambda i, j: (0, 0)),
        ],
        out_specs=[out_t, out_t, out_t, out_d, out_d, out_d, out_t, out_t],
        out_shape=[sds((n, c_rw), F32), sds((n, c_rw), F32), sds((n, c_rw), F32),
                   sds((2, n, c_rw), F32), sds((2, n, c_rw), F32), sds((2, n, c_rw), F32),
                   sds((n, c_rw), F32), sds((n, c_rw), F32)],
        compiler_params=_params(("arbitrary", "arbitrary")),
        name="rwkv_prep",
    )(*([z_rkv] * 9), *([z_small] * 3), mu_big, mu_big, mu_big, mu_small,
      lp["rw_w0"].reshape(2, 1, c_rw), lp["rw_w_up"], lp["rw_a0"].reshape(2, 1, c_rw), lp["rw_a_up"],
      lp["rw_g_up"].astype(BF16), row(lp["rw_k_k"]), row(lp["rw_k_a"]), row(lp["rw_r_k"]),
      _head_ones(tn, head))


def _rw_scan_kernel(r_ref, kk_ref, v_ref, lw_ref, kd_ref, beta_ref, y_ref, st_scr, *, head):
    t_len = r_ref.shape[0]
    n_pack = RW_PACK
    width = n_pack * head
    n_units = r_ref.shape[1] // width
    rev = pl.program_id(0) == 1

    @pl.when(pl.program_id(3) == 0)
    def _():
        st_scr[...] = jnp.zeros_like(st_scr)

    lane_head = lax.broadcasted_iota(jnp.int32, (1, width), 1) // head
    head_masks = [lane_head == h for h in range(n_pack)]

    def block_diag(x):
        xb = x.astype(BF16)
        zero = jnp.zeros_like(xb)
        return jnp.concatenate([jnp.where(m, xb, zero) for m in head_masks], axis=0)

    def mm(a, bd):
        return _dot(a.astype(BF16), bd)

    def mm_nt(a, bd):
        return lax.dot_general(a.astype(BF16), bd, (((1,), (1,)), ((), ())), preferred_element_type=F32)

    def diag_blocks_tn(a, b):
        full = lax.dot_general(a.astype(BF16), b.astype(BF16), (((0,), (0,)), ((), ())),
                               preferred_element_type=F32)
        out = jnp.where(head_masks[0], full[0:head], 0.0)
        for h in range(1, n_pack):
            out = out + jnp.where(head_masks[h], full[h * head:(h + 1) * head], 0.0)
        return out

    sgn = jnp.where(rev, -1, 1)
    tt = lax.broadcasted_iota(jnp.int32, (t_len, width), 0)
    ss = lax.broadcasted_iota(jnp.int32, (t_len, width), 1) % head
    order = (tt - ss) * sgn
    strict = order > 0
    incl = order >= 0
    eye = jnp.where(tt == ss, 1.0, 0.0)
    ti = lax.broadcasted_iota(jnp.int32, (t_len, t_len), 0)
    si = lax.broadcasted_iota(jnp.int32, (t_len, t_len), 1)
    tri = jnp.where((ti - si) * sgn >= 0, 1.0, 0.0).astype(BF16)

    units = [slice(u * width, (u + 1) * width) for u in range(n_units)]
    each = lambda f, *xs: [f(*a) for a in zip(*xs)]
    r = [r_ref[:, c] for c in units]
    kk = [kk_ref[:, c] for c in units]
    v = [v_ref[:, c] for c in units]
    lw = [lw_ref[:, c] for c in units]
    kd = [kd_ref[:, c] for c in units]
    beta = [beta_ref[:, c] for c in units]

    def cumulative(x):
        hi, lo = _split(x)
        return _dot(tri, hi) + _dot(tri, lo)

    log_p = each(cumulative, lw)
    log_pt = each(lambda x: jnp.sum(x, axis=0, keepdims=True), lw)
    p_inv = each(lambda x: jnp.exp(-x), log_p)
    a_bar = each(lambda a, p, w: a * jnp.exp(p - w), kk, log_p, lw)
    r_bar = each(lambda a, p: a * jnp.exp(p), r, log_p)
    b_til = each(jnp.multiply, beta, p_inv)
    k_til = each(jnp.multiply, kd, p_inv)
    p_rest = each(lambda t, p: jnp.exp(t - p), log_pt, log_p)
    b_end = each(jnp.multiply, beta, p_rest)
    k_end = each(jnp.multiply, kd, p_rest)

    ar = each(lambda a, b: jnp.concatenate([a, b], axis=0), a_bar, r_bar)
    gram_b = each(lambda a, b: mm_nt(a, block_diag(b)), ar, b_til)
    gram_k = each(lambda a, b: mm_nt(a, block_diag(b)), ar, k_til)
    l_b = each(lambda g: jnp.where(strict, g[:t_len], 0.0), gram_b)
    m_b = each(lambda g: jnp.where(incl, g[t_len:], 0.0), gram_b)
    l_k = each(lambda g: jnp.where(strict, g[:t_len], 0.0), gram_k)
    m_k = each(lambda g: jnp.where(incl, g[t_len:], 0.0), gram_k)

    t_inv = each(lambda x: eye - x, l_b)
    l_pow = l_b
    span = 2
    while span < t_len:
        l_pow = each(lambda x: mm(x, block_diag(x)), l_pow)
        t_inv = each(lambda t, x: t + mm(t, block_diag(x)), t_inv, l_pow)
        span *= 2

    bd_v = each(block_diag, v)
    w_til = each(lambda t, a: mm(t, block_diag(a)), t_inv, a_bar)
    lkv = each(mm, l_k, bd_v)
    u_til = each(lambda t, a: mm(t, block_diag(a)), t_inv, lkv)
    q_hat = each(lambda a, m, w: a - mm(m, block_diag(w)), r_bar, m_b, w_til)
    y_hat = each(lambda mk, bv, mb, u: mm(mk, bv) - mm(mb, block_diag(u)), m_k, bd_v, m_b, u_til)
    g_mat = each(lambda t, b, w: eye * jnp.exp(t) - diag_blocks_tn(b, w), log_pt, b_end, w_til)
    h_mat = each(lambda k, b, x, u: diag_blocks_tn(jnp.concatenate([k, b], axis=0), jnp.concatenate([x, -u], axis=0)),
                 k_end, b_end, v, u_til)

    bd_s = [block_diag(st_scr[:, c]) for c in units]
    for c, q, y0, g, h, s in zip(units, q_hat, y_hat, g_mat, h_mat, bd_s):
        y_ref[:, c] = mm(q, s) + y0
        st_scr[:, c] = mm(g, s) + h


def _rw_scan(r, kk, v, lw, kd, beta, *, bsz, l_ctx, l_lat, head):
    n, c_rw = r.shape
    t_len = RW_CHUNK
    width = _pick(c_rw, RW_UNITS * RW_PACK * head)
    nc = l_ctx // t_len
    nl = l_lat // t_len
    ctx_blocks = bsz * nc

    def blk(d, b, c):
        cc = jnp.where(d == 1, nc - 1 - c, c)
        cl = jnp.where(d == 1, nl - 1 - (c - nc), c - nc)
        return jnp.where(c < nc, b * nc + cc, ctx_blocks + b * nl + cl)

    shared = pl.BlockSpec((t_len, width), lambda d, b, g, c: (blk(d, b, c), g))
    per_dir = pl.BlockSpec((None, t_len, width), lambda d, b, g, c: (d, blk(d, b, c), g))
    kern = functools.partial(_rw_scan_kernel, head=head)
    return pl.pallas_call(
        kern,
        grid=(2, bsz, c_rw // width, nc + nl),
        in_specs=[shared, shared, shared, per_dir, per_dir, per_dir],
        out_specs=per_dir,
        out_shape=jax.ShapeDtypeStruct((2, n, c_rw), F32),
        scratch_shapes=[pltpu.VMEM((head, width), F32)],
        compiler_params=_params(("arbitrary", "arbitrary", "arbitrary", "arbitrary")),
        name="rwkv_scan",
    )(r, kk, v, lw, kd, beta)


def _rw_readout_kernel(y_ref, g_ref, bonus_ref, lnw_ref, lnb_ref, ones_ref, o_ref, *, head):
    y = y_ref[0] + y_ref[1]
    ones = ones_ref[...]
    inv = 1.0 / head
    mu = _dot_exact_rhs(y, ones) * inv
    yc = y - mu
    var = _dot_exact_rhs(yc * yc, ones) * inv
    yn = yc * lax.rsqrt(var + GN_EPS) * lnw_ref[...] + lnb_ref[...]
    o_ref[...] = ((yn + bonus_ref[...]) * g_ref[...]).astype(o_ref.dtype)


def _rw_readout(y2, g, bonus, ln_w, ln_b, *, head, tm):
    _, n, c_rw = y2.shape
    tn = RW_PACK * head
    tile = pl.BlockSpec((tm, tn), lambda i, j: (i, j))
    vec = pl.BlockSpec((1, tn), lambda i, j: (0, j))
    kern = functools.partial(_rw_readout_kernel, head=head)
    return pl.pallas_call(
        kern,
        grid=(n // tm, c_rw // tn),
        in_specs=[pl.BlockSpec((2, tm, tn), lambda i, j: (0, i, j)), tile, tile, vec, vec,
                  pl.BlockSpec((tn, tn), lambda i, j: (0, 0))],
        out_specs=tile,
        out_shape=jax.ShapeDtypeStruct((n, c_rw), BF16),
        compiler_params=_params(("arbitrary", "arbitrary")),
        name="rwkv_readout",
    )(y2, g, bonus, ln_w.reshape(1, c_rw), ln_b.reshape(1, c_rw), _head_ones(tn, head))


def _final_norm_kernel(h_ref, gain_ref, o_ref):
    x = h_ref[...]
    ms = jnp.mean(x * x, axis=-1, keepdims=True)
    o_ref[...] = x * lax.rsqrt(ms + NORM_EPS) * gain_ref[...]


def _final_norm(h, gain, *, tm, row0):
    n, d = h.shape
    b0 = row0 // tm
    return pl.pallas_call(
        _final_norm_kernel,
        grid=((n - row0) // tm,),
        in_specs=[pl.BlockSpec((tm, d), lambda i: (i + b0, 0)), pl.BlockSpec((1, d), lambda i: (0, 0))],
        out_specs=pl.BlockSpec((tm, d), lambda i: (i, 0)),
        out_shape=jax.ShapeDtypeStruct((n - row0, d), F32),
        compiler_params=_params(("arbitrary",)),
        name="final_norm",
    )(h, gain.reshape(1, d))


def _to_s5_layout(u, bsz, l_ctx, l_lat, groups, gch):
    t = S5_CHUNK
    n_ctx = bsz * l_ctx
    uc = u[:n_ctx].reshape(bsz, l_ctx // t, t, groups, gch)
    ul = u[n_ctx:].reshape(bsz, l_lat // t, t, groups, gch)
    x = jnp.concatenate([uc, ul], axis=1).astype(BF16)
    x = jnp.transpose(x, (3, 1, 0, 2, 4))
    return x.reshape(groups, -1, t * gch)


def _from_s5_layout(y, bsz, l_ctx, l_lat, groups, gch):
    t = S5_CHUNK
    chunks = (l_ctx + l_lat) // t
    x = y.reshape(groups, chunks, bsz, t, gch)
    x = jnp.transpose(x, (2, 1, 3, 0, 4))
    nc = l_ctx // t
    yc = x[:, :nc].reshape(bsz * l_ctx, groups * gch)
    yl = x[:, nc:].reshape(bsz * l_lat, groups * gch)
    return jnp.concatenate([yc, yl], axis=0)


def kernel(x, c, ctx, c_ctx, ada_down, ada_up, ada_b, norm1, w_in, s5_a_re, s5_a_im, s5_log_dt, s5_b_re, s5_b_im, s5_c_re, s5_c_im, s5_d, s5_glu_w, s5_glu_b, rw_mu, rw_w0, rw_w_up, rw_a0, rw_a_up, rw_g_up, rw_k_k, rw_k_a, rw_r_k, rw_ln_w, rw_ln_b, w_proj_a, w_proj_b, gate_up, gate_b, w_out, norm2, mlp_w1, mlp_w2, norm_f):
    bsz, l_lat, d = x.shape
    l_ctx = ctx.shape[1]
    depth = w_in.shape[0]
    n_ctx = bsz * l_ctx
    n_lat = bsz * l_lat
    s5_groups, s5_state = s5_a_re.shape[2], s5_a_re.shape[3]
    s5_gch = s5_b_re.shape[-1]
    c_s5 = s5_groups * s5_gch
    rw_heads, head = rw_r_k.shape[1], rw_r_k.shape[2]
    c_rw = rw_heads * head
    lora_w, lora_a, lora_g = rw_w_up.shape[2], rw_a_up.shape[2], rw_g_up.shape[1]
    gate_rank = gate_up.shape[2]
    n_small_rw = 2 * lora_w + 2 * lora_a + lora_g
    c_big = c_s5 + 3 * c_rw
    assert w_in.shape[2] == c_big + n_small_rw + gate_rank
    assert bsz % 8 == 0 and l_ctx % RW_CHUNK == 0 and l_lat % RW_CHUNK == 0 and l_lat % GRID_W == 0
    assert s5_gch * S5_CHUNK == 256 and RW_PACK * head == 256 and c_rw % 256 == 0 and gate_rank % 128 == 0
    assert c_s5 % (RW_PACK * head) == 0

    tm = _pick(l_lat, _pick(n_ctx, TM))

    def row_of_block(i):
        ctx_blocks = n_ctx // tm
        return jnp.where(i < ctx_blocks, bsz, (i - ctx_blocks) // (l_lat // tm))

    n_cond = ((bsz + 1 + 7) // 8) * 8
    cond = jnp.zeros((n_cond, d), F32).at[:bsz].set(c).at[bsz].set(c_ctx)
    mods_all = _ada_modulation(cond, ada_down, ada_up, ada_b)
    mods_all = jnp.transpose(mods_all.reshape(depth, n_cond, 6, 1, d), (0, 2, 1, 3, 4))

    h = jnp.concatenate([ctx.reshape(n_ctx, d), x.reshape(n_lat, d)], axis=0)

    for l in range(depth):
        last = l == depth - 1
        row0 = n_ctx if last else 0
        mods = mods_all[l]
        lp = {"rw_mu": rw_mu[l], "rw_w0": rw_w0[l], "rw_w_up": rw_w_up[l], "rw_a0": rw_a0[l],
              "rw_a_up": rw_a_up[l], "rw_g_up": rw_g_up[l], "rw_k_k": rw_k_k[l], "rw_k_a": rw_k_a[l],
              "rw_r_k": rw_r_k[l]}
        w_small = jnp.concatenate([w_in[l, :, c_big + n_small_rw:], w_in[l, :, c_big:c_big + n_small_rw]],
                                  axis=1).astype(BF16)
        z_u = _norm_matmul(h, norm1[l], mods, w_in[l, :, :c_s5].astype(BF16), row_of_block, tm=tm,
                           tn=_pick(c_s5, 1024), out_dtype=BF16, name="in_proj_s5")
        z_rkv = _norm_matmul(h, norm1[l], mods, w_in[l, :, c_s5:c_big].astype(BF16), row_of_block, tm=tm,
                             tn=_pick(3 * c_rw, 1024), out_dtype=F32, name="in_proj_rkv")
        z_small = _norm_matmul(h, norm1[l], mods, w_small, row_of_block, tm=tm, tn=w_small.shape[1],
                               out_dtype=F32, name="in_proj_small")

        km, bm, cm, lam = _s5_matrices(s5_a_re[l], s5_a_im[l], s5_log_dt[l], s5_b_re[l], s5_b_im[l],
                                       s5_c_re[l], s5_c_im[l], s5_d[l])
        u_g = _to_s5_layout(z_u, bsz, l_ctx, l_lat, s5_groups, s5_gch)
        y_g = _s5_scan(u_g, km, bm, cm, lam, n_ctx=l_ctx // S5_CHUNK, n_chunks=(l_ctx + l_lat) // S5_CHUNK, bsz=bsz)
        y_s5 = _from_s5_layout(y_g, bsz, l_ctx, l_lat, s5_groups, s5_gch)
        ya = _s5_glu(y_s5, s5_glu_w[l].astype(BF16), s5_glu_b[l], tm=tm)

        r, kk, v, lw, kd, beta, g, bonus = _rw_prep(z_rkv, z_small, lp, bsz=bsz, l_ctx=l_ctx, l_lat=l_lat,
                                                    c_rw=c_rw, head=head, gate_rank=gate_rank)
        y2 = _rw_scan(r, kk, v, lw, kd, beta, bsz=bsz, l_ctx=l_ctx, l_lat=l_lat, head=head)
        yb = _rw_readout(y2, g, bonus, rw_ln_w[l], rw_ln_b[l], head=head, tm=tm)

        h = _merge_out(ya, yb, z_small, w_proj_a[l].astype(BF16), w_proj_b[l].astype(BF16), gate_up[l].astype(BF16),
                       gate_b[l], w_out[l].astype(BF16), h, mods, row_of_block, tm=tm, row0=row0)
        h = _mlp(h, norm2[l], mods, mlp_w1[l].astype(BF16), mlp_w2[l].astype(BF16), row_of_block, tm=tm, row0=row0)

    out = _final_norm(h, norm_f, tm=tm, row0=n_ctx)
    return out.reshape(bsz, l_lat, d)
```

```python
import functools

import jax
import jax.numpy as jnp
from jax import lax
from jax.experimental import pallas as pl
from jax.experimental.pallas import tpu as pltpu

F32 = jnp.float32
BF16 = jnp.bfloat16

GRID_W = 64
NORM_EPS = 1e-6
GN_EPS = 64e-5
A_RE_MAX = -1e-4
S5_CHUNK = 16
RW_CHUNK = 64
RW_PACK = 4
RW_UNITS = 8
TM = 512
VMEM_LIMIT = 56 * 1024 * 1024


def _params(sem):
    return pltpu.CompilerParams(dimension_semantics=sem, vmem_limit_bytes=VMEM_LIMIT)


def _pick(n, pref):
    t = pref
    while n % t:
        t //= 2
    return t


def _dot(a, b):
    return jnp.dot(a, b, preferred_element_type=F32)


def _split(x):
    hi = x.astype(BF16)
    lo = (x - hi.astype(F32)).astype(BF16)
    return hi, lo


def _dot3(a, b):
    ah, al = _split(a)
    bh, bl = _split(b)
    return _dot(ah, bh) + (_dot(ah, bl) + _dot(al, bh))


def _dot_exact_rhs(a, b_bf16):
    ah, al = _split(a)
    return _dot(ah, b_bf16) + _dot(al, b_bf16)


def _sigmoid(x):
    return 1.0 / (1.0 + jnp.exp(-x))


def _gelu_tanh(x):
    c = 0.7978845608028654
    return 0.5 * x * (1.0 + jnp.tanh(c * (x + 0.044715 * (x * x * x))))


def _ada_kernel(cond_ref, wd_ref, wu_ref, b_ref, o_ref, mid_scr):
    @pl.when(pl.program_id(1) == 0)
    def _():
        c = cond_ref[...]
        mid_scr[...] = _dot3(c * _sigmoid(c), wd_ref[...])

    o_ref[...] = _dot3(mid_scr[...], wu_ref[...]) + b_ref[...]


def _ada_modulation(cond, w_down, w_up, bias):
    nl, d, rank = w_down.shape
    r = cond.shape[0]
    n_out = w_up.shape[-1]
    tn = _pick(n_out, 2048)
    return pl.pallas_call(
        _ada_kernel,
        grid=(nl, n_out // tn),
        in_specs=[
            pl.BlockSpec((r, d), lambda l, j: (0, 0)),
            pl.BlockSpec((None, d, rank), lambda l, j: (l, 0, 0)),
            pl.BlockSpec((None, rank, tn), lambda l, j: (l, 0, j)),
            pl.BlockSpec((None, 1, tn), lambda l, j: (l, 0, j)),
        ],
        out_specs=pl.BlockSpec((None, r, tn), lambda l, j: (l, 0, j)),
        out_shape=jax.ShapeDtypeStruct((nl, r, n_out), F32),
        scratch_shapes=[pltpu.VMEM((r, rank), F32)],
        compiler_params=_params(("arbitrary", "arbitrary")),
        name="ada_modulation",
    )(cond, w_down, w_up, bias.reshape(nl, 1, n_out))


def _norm_matmul_kernel(h_ref, gain_ref, shift_ref, scale_ref, w_ref, o_ref, n_scr):
    @pl.when(pl.program_id(1) == 0)
    def _():
        x = h_ref[...]
        ms = jnp.mean(x * x, axis=-1, keepdims=True)
        xn = x * lax.rsqrt(ms + NORM_EPS) * gain_ref[...]
        n_scr[...] = (xn * (1.0 + scale_ref[...]) + shift_ref[...]).astype(BF16)

    o_ref[...] = _dot(n_scr[...], w_ref[...]).astype(o_ref.dtype)


def _norm_matmul(h, gain, mods, w, row_of_block, *, tm, tn, out_dtype, name):
    n, d = h.shape
    n_out = w.shape[1]
    mod = lambda which: pl.BlockSpec((None, None, 1, d), lambda i, j: (which, row_of_block(i), 0, 0))
    return pl.pallas_call(
        _norm_matmul_kernel,
        grid=(n // tm, n_out // tn),
        in_specs=[
            pl.BlockSpec((tm, d), lambda i, j: (i, 0)),
            pl.BlockSpec((1, d), lambda i, j: (0, 0)),
            mod(0), mod(1),
            pl.BlockSpec((d, tn), lambda i, j: (0, j)),
        ],
        out_specs=pl.BlockSpec((tm, tn), lambda i, j: (i, j)),
        out_shape=jax.ShapeDtypeStruct((n, n_out), out_dtype),
        scratch_shapes=[pltpu.VMEM((tm, d), BF16)],
        compiler_params=_params(("arbitrary", "arbitrary")),
        name=name,
    )(h, gain.reshape(1, d), mods, mods, w)


def _mlp_kernel(h_ref, gain_ref, shift_ref, scale_ref, gate_ref, w1_ref, w2_ref, o_ref, n_scr):
    j = pl.program_id(1)

    @pl.when(j == 0)
    def _():
        x = h_ref[...]
        ms = jnp.mean(x * x, axis=-1, keepdims=True)
        xn = x * lax.rsqrt(ms + NORM_EPS) * gain_ref[...]
        n_scr[...] = (xn * (1.0 + scale_ref[...]) + shift_ref[...]).astype(BF16)
        o_ref[...] = jnp.zeros_like(o_ref)

    hid = jnp.square(jnp.maximum(_dot(n_scr[...], w1_ref[...]), 0.0)).astype(BF16)
    o_ref[...] += _dot(hid, w2_ref[...])

    @pl.when(j == pl.num_programs(1) - 1)
    def _():
        o_ref[...] = h_ref[...] + gate_ref[...] * o_ref[...]


def _mlp(h, gain, mods, w1, w2, row_of_block, *, tm, row0):
    n, d = h.shape
    d_ff = w1.shape[1]
    tf = _pick(d_ff, 512)
    b0 = row0 // tm
    mod = lambda which: pl.BlockSpec((None, None, 1, d), lambda i, j: (which, row_of_block(i + b0), 0, 0))
    return pl.pallas_call(
        _mlp_kernel,
        grid=((n - row0) // tm, d_ff // tf),
        in_specs=[
            pl.BlockSpec((tm, d), lambda i, j: (i + b0, 0), pipeline_mode=pl.Buffered(1)),
            pl.BlockSpec((1, d), lambda i, j: (0, 0)),
            mod(3), mod(4), mod(5),
            pl.BlockSpec((d, tf), lambda i, j: (0, j)),
            pl.BlockSpec((tf, d), lambda i, j: (j, 0)),
        ],
        out_specs=pl.BlockSpec((tm, d), lambda i, j: (i + b0, 0)),
        out_shape=jax.ShapeDtypeStruct((n, d), F32),
        scratch_shapes=[pltpu.VMEM((tm, d), BF16)],
        input_output_aliases={0: 0},
        compiler_params=_params(("arbitrary", "arbitrary")),
        name="mlp",
    )(h, gain.reshape(1, d), mods, mods, mods, w1, w2)


def _merge_out_kernel(ya_ref, yb_ref, gz_ref, wa_ref, wb_ref, ga_ref, gb_ref, ba_ref, bb_ref, wo_ref, h_ref,
                      gate_ref, o_ref):
    j = pl.program_id(1)

    @pl.when(j == 0)
    def _():
        o_ref[...] = jnp.zeros_like(o_ref)

    gz = gz_ref[...].astype(BF16)
    ga = _sigmoid(_dot(gz, ga_ref[...]) + ba_ref[...])
    gb = _sigmoid(_dot(gz, gb_ref[...]) + bb_ref[...])
    mixed = (ga * _dot(ya_ref[...], wa_ref[...]) + gb * _dot(yb_ref[...], wb_ref[...])).astype(BF16)
    o_ref[...] += _dot(mixed, wo_ref[...])

    @pl.when(j == pl.num_programs(1) - 1)
    def _():
        o_ref[...] = h_ref[...] + gate_ref[...] * o_ref[...]


def _merge_out(ya, yb, z_small, wa, wb, gate_up, gate_b, w_out, h, mods, row_of_block, *, tm, row0):
    n, d = h.shape
    ca, cb = ya.shape[1], yb.shape[1]
    d_mid = wa.shape[1]
    rank = gate_up.shape[1]
    tn = _pick(d_mid, 512)
    b0 = row0 // tm
    once = pl.Buffered(1)
    return pl.pallas_call(
        _merge_out_kernel,
        grid=((n - row0) // tm, d_mid // tn),
        in_specs=[
            pl.BlockSpec((tm, ca), lambda i, j: (i + b0, 0), pipeline_mode=once),
            pl.BlockSpec((tm, cb), lambda i, j: (i + b0, 0), pipeline_mode=once),
            pl.BlockSpec((tm, rank), lambda i, j: (i + b0, 0)),
            pl.BlockSpec((ca, tn), lambda i, j: (0, j)),
            pl.BlockSpec((cb, tn), lambda i, j: (0, j)),
            pl.BlockSpec((None, rank, tn), lambda i, j: (0, 0, j)),
            pl.BlockSpec((None, rank, tn), lambda i, j: (1, 0, j)),
            pl.BlockSpec((None, 1, tn), lambda i, j: (0, 0, j)),
            pl.BlockSpec((None, 1, tn), lambda i, j: (1, 0, j)),
            pl.BlockSpec((tn, d), lambda i, j: (j, 0)),
            pl.BlockSpec((tm, d), lambda i, j: (i + b0, 0), pipeline_mode=once),
            pl.BlockSpec((None, None, 1, d), lambda i, j: (2, row_of_block(i + b0), 0, 0)),
        ],
        out_specs=pl.BlockSpec((tm, d), lambda i, j: (i + b0, 0)),
        out_shape=jax.ShapeDtypeStruct((n, d), F32),
        input_output_aliases={10: 0},
        compiler_params=_params(("arbitrary", "arbitrary")),
        name="merge_out_proj",
    )(ya, yb, z_small, wa, wb, gate_up, gate_up, gate_b.reshape(2, 1, d_mid), gate_b.reshape(2, 1, d_mid),
      w_out, h, mods)


def _s5_matrices(a_re, a_im, log_dt, b_re, b_im, c_re, c_im, d_skip):
    t_len = S5_CHUNK
    hp = lax.Precision.HIGHEST
    lam_re = jnp.minimum(a_re.astype(F32), A_RE_MAX)
    lam_im = a_im.astype(F32)
    dt = jnp.exp(log_dt.astype(F32))[..., None]
    j = jnp.arange(t_len + 1, dtype=F32)[:, None, None, None]
    mag = jnp.exp(j * (lam_re * dt))
    ang = j * (lam_im * dt)
    pw_re = mag * jnp.cos(ang)
    pw_im = mag * jnp.sin(ang)
    nr, ni = pw_re[1] - 1.0, pw_im[1]
    den = lam_re * lam_re + lam_im * lam_im
    qr = (nr * lam_re + ni * lam_im) / den
    qi = (ni * lam_re - nr * lam_im) / den
    bbr = qr[..., None] * b_re - qi[..., None] * b_im
    bbi = qr[..., None] * b_im + qi[..., None] * b_re
    n_dir, g, p, c = bbr.shape
    clr = c_re[None] * pw_re[:t_len, :, :, None, :] - c_im[None] * pw_im[:t_len, :, :, None, :]
    cli = c_re[None] * pw_im[:t_len, :, :, None, :] + c_im[None] * pw_re[:t_len, :, :, None, :]
    kj = (jnp.einsum("jdgop,dgpi->dgijo", clr, bbr, precision=hp)
          - jnp.einsum("jdgop,dgpi->dgijo", cli, bbi, precision=hp))
    lag0 = (jnp.arange(t_len) == 0).astype(F32)
    skip = d_skip.astype(F32).reshape(g, c)
    kf = kj[0] + skip[:, :, None, None] * lag0[None, None, :, None] * jnp.eye(c, dtype=F32)[None, :, None, :]
    kf = kf.reshape(g, c, t_len * c)
    kb = jnp.flip(kj[1], axis=2).reshape(g, c, t_len * c)
    bbr_t = jnp.transpose(bbr, (0, 1, 3, 2))
    bbi_t = jnp.transpose(bbi, (0, 1, 3, 2))

    def powers(d, e):
        return jnp.transpose(pw_re[e, d], (1, 0, 2)), jnp.transpose(pw_im[e, d], (1, 0, 2))

    def state_in(d, e):
        pr, pi = powers(d, e)
        re = pr[:, :, None, :] * bbr_t[d][:, None] - pi[:, :, None, :] * bbi_t[d][:, None]
        im = pr[:, :, None, :] * bbi_t[d][:, None] + pi[:, :, None, :] * bbr_t[d][:, None]
        return re.reshape(g, t_len * c, p), im.reshape(g, t_len * c, p)

    fr, fi = state_in(0, (t_len - 1) - jnp.arange(t_len))
    br, bi = state_in(1, jnp.arange(t_len))
    bm = jnp.concatenate([fr, br, fi, bi], axis=-1)

    def state_out(d, e):
        pr, pi = powers(d, e)
        re = c_re[d][:, None] * pr[:, :, None, :] - c_im[d][:, None] * pi[:, :, None, :]
        im = c_re[d][:, None] * pi[:, :, None, :] + c_im[d][:, None] * pr[:, :, None, :]
        return re.reshape(g, t_len * c, p), -im.reshape(g, t_len * c, p)

    cfr, cfi = state_out(0, jnp.arange(t_len) + 1)
    cbr, cbi = state_out(1, t_len - jnp.arange(t_len))
    cmt = jnp.concatenate([cfr, cbr, cfi, cbi], axis=-1)
    lam = jnp.stack([jnp.concatenate([pw_re[t_len, 0], pw_re[t_len, 1]], axis=-1),
                     jnp.concatenate([pw_im[t_len, 0], pw_im[t_len, 1]], axis=-1)], axis=1)
    return kf, kb, bm.astype(BF16), cmt.astype(BF16), lam


def _s5_kernel(u_ref, bm_ref, kf_ref, kb_ref, cmt_ref, lam_ref, y_ref, xin_scr, fwd_scr, bwd_scr, km_scr, *,
               n_ctx, n_chunks, bsz, p_state):
    u = u_ref[...]
    xin_scr[...] = _dot(u, bm_ref[...])
    gch, tc = kf_ref.shape
    kf = kf_ref[...]
    kb = kb_ref[...]
    lane_k = lax.broadcasted_iota(jnp.int32, (gch, tc), 1)
    for s in range(tc // gch):
        fwd = kf if s == 0 else jnp.where(lane_k >= gch * s, pltpu.roll(kf, gch * s, axis=1), 0.0)
        shift = (gch * (s + 1)) % tc
        bwd = kb if shift == 0 else jnp.where(lane_k < gch * (s + 1), pltpu.roll(kb, shift, axis=1), 0.0)
        km_scr[s * gch:(s + 1) * gch, :] = (fwd + bwd).astype(BF16)
    two_p = 2 * p_state
    lam_r = jnp.broadcast_to(lam_ref[0:1, :], (bsz, two_p))
    lam_i = jnp.broadcast_to(lam_ref[1:2, :], (bsz, two_p))
    is_fwd = lax.broadcasted_iota(jnp.int32, (bsz, two_p), 1) < p_state

    def body(i, carry):
        re, im = carry
        pb = jnp.where(i < n_ctx, n_ctx - 1 - i, n_chunks + n_ctx - 1 - i)
        rf = pl.multiple_of(i * bsz, bsz)
        rb = pl.multiple_of(pb * bsz, bsz)
        xf = xin_scr[pl.ds(rf, bsz), :]
        xb = xin_scr[pl.ds(rb, bsz), :]
        x_re = jnp.where(is_fwd, xf[:, :two_p], xb[:, :two_p])
        x_im = jnp.where(is_fwd, xf[:, two_p:], xb[:, two_p:])
        state = jnp.concatenate([re, im], axis=1)
        fwd_scr[pl.ds(rf, bsz), :] = state
        bwd_scr[pl.ds(rb, bsz), :] = state
        return (re * lam_r - im * lam_i + x_re, re * lam_i + im * lam_r + x_im)

    zero = jnp.zeros((bsz, two_p), F32)
    lax.fori_loop(0, n_chunks, body, (zero, zero))
    lane = lax.broadcasted_iota(jnp.int32, (1, 2 * two_p), 1)
    fwd_lane = (lane % two_p) < p_state
    xs = jnp.where(fwd_lane, fwd_scr[...], bwd_scr[...]).astype(BF16)
    from_state = lax.dot_general(xs, cmt_ref[...], (((1,), (1,)), ((), ())), preferred_element_type=F32)
    y_ref[...] = (_dot(u, km_scr[...]) + from_state).astype(y_ref.dtype)


def _s5_scan(u_g, kf, kb, bm, cmt, lam, *, n_ctx, n_chunks, bsz):
    g, rows, tc = u_g.shape
    gch = kf.shape[1]
    p4 = bm.shape[-1]
    kern = functools.partial(_s5_kernel, n_ctx=n_ctx, n_chunks=n_chunks, bsz=bsz, p_state=p4 // 4)
    return pl.pallas_call(
        kern,
        grid=(g,),
        in_specs=[
            pl.BlockSpec((None, rows, tc), lambda i: (i, 0, 0)),
            pl.BlockSpec((None, tc, p4), lambda i: (i, 0, 0)),
            pl.BlockSpec((None, gch, tc), lambda i: (i, 0, 0)),
            pl.BlockSpec((None, gch, tc), lambda i: (i, 0, 0)),
            pl.BlockSpec((None, tc, p4), lambda i: (i, 0, 0)),
            pl.BlockSpec((None, 2, p4 // 2), lambda i: (i, 0, 0)),
        ],
        out_specs=pl.BlockSpec((None, rows, tc), lambda i: (i, 0, 0)),
        out_shape=jax.ShapeDtypeStruct((g, rows, tc), BF16),
        scratch_shapes=[pltpu.VMEM((rows, p4), F32), pltpu.VMEM((rows, p4), F32), pltpu.VMEM((rows, p4), F32),
                        pltpu.VMEM((tc, tc), BF16)],
        compiler_params=_params(("arbitrary",)),
        name="s5_scan",
    )(u_g, bm, kf, kb, cmt, lam)


def _glu_kernel(y_ref, w_ref, b_ref, o_ref):
    z = _gelu_tanh(y_ref[...].astype(F32))
    o_ref[...] = (z * _sigmoid(_dot(z.astype(BF16), w_ref[...]) + b_ref[...])).astype(o_ref.dtype)


def _s5_glu(y, w, b, *, tm):
    n, c = y.shape
    return pl.pallas_call(
        _glu_kernel,
        grid=(n // tm,),
        in_specs=[
            pl.BlockSpec((tm, c), lambda i: (i, 0)),
            pl.BlockSpec((c, c), lambda i: (0, 0)),
            pl.BlockSpec((1, c), lambda i: (0, 0)),
        ],
        out_specs=pl.BlockSpec((tm, c), lambda i: (i, 0)),
        out_shape=jax.ShapeDtypeStruct((n, c), BF16),
        compiler_params=_params(("arbitrary",)),
        name="s5_glu",
    )(y, w, b.reshape(1, c))


def _head_ones(width, head):
    i = jnp.arange(width) // head
    return (i[:, None] == i[None, :]).astype(BF16)


def _rw_prep_kernel(zr_ref, zrp_ref, zrn_ref, zk_ref, zkp_ref, zkn_ref, zv_ref, zvp_ref, zvn_ref,
                    zs_ref, zsp_ref, zsn_ref, mur_ref, muk_ref, muv_ref, mus_ref,
                    w0_ref, wup_ref, a0_ref, aup_ref, gup_ref, kk_ref, ka_ref, rk_ref, ones_ref,
                    r_o, kk_o, v_o, lw_o, kd_o, beta_o, g_o, bonus_o, *,
                    lora_w, lora_a, small0, n_ctx, l_ctx, l_lat, c_rw, quarter):
    tm, tn = zr_ref.shape
    halo = zrp_ref.shape[0]
    g0 = pl.program_id(0) * tm
    is_ctx = g0 < n_ctx
    pos = jnp.where(is_ctx, g0 % l_ctx, (g0 - n_ctx) % l_lat) + lax.broadcasted_iota(jnp.int32, (tm, 1), 0)
    row = lax.broadcasted_iota(jnp.int32, (tm, 1), 0)
    ok_m1 = jnp.where(is_ctx, pos, pos % GRID_W) >= 1
    ok_p1 = jnp.where(is_ctx, l_ctx - 1 - pos, GRID_W - 1 - pos % GRID_W) >= 1
    ok_mw = jnp.where(is_ctx, 0, pos) >= GRID_W
    ok_pw = jnp.where(is_ctx, 0, l_lat - GRID_W - pos) >= 1

    def mix(cur, prev, nxt, mu, col0):
        width = cur.shape[1]
        col = col0 + lax.broadcasted_iota(jnp.int32, (1, width), 1)
        cq = col // quarter
        cls = jnp.where(is_ctx, cq // 2, cq)
        m1 = jnp.where(row == 0, prev[halo - 1:halo, :], pltpu.roll(cur, 1, axis=0))
        p1 = jnp.where(row == tm - 1, nxt[0:1, :], pltpu.roll(cur, tm - 1, axis=0))
        mw = jnp.concatenate([prev, cur[:tm - halo]], axis=0)
        pw = jnp.concatenate([cur[halo:], nxt], axis=0)
        zs = jnp.where(cls == 0, jnp.where(ok_m1, m1, 0.0),
                       jnp.where(cls == 1, jnp.where(ok_p1, p1, 0.0),
                                 jnp.where(cls == 2, jnp.where(ok_mw, mw, 0.0), jnp.where(ok_pw, pw, 0.0))))
        return cur + (zs - cur) * mu

    j0 = pl.program_id(1) * tn
    r = mix(zr_ref[...], zrp_ref[...], zrn_ref[...], mur_ref[...], j0)
    k = mix(zk_ref[...], zkp_ref[...], zkn_ref[...], muk_ref[...], c_rw + j0)
    v = mix(zv_ref[...], zvp_ref[...], zvn_ref[...], muv_ref[...], 2 * c_rw + j0)
    sm = mix(zs_ref[:, small0:], zsp_ref[:, small0:], zsn_ref[:, small0:], mus_ref[...], 3 * c_rw)
    o_a = 2 * lora_w
    o_g = o_a + 2 * lora_a
    ones = ones_ref[...]
    g_o[...] = _dot(_sigmoid(sm[:, o_g:]).astype(BF16), gup_ref[...]).astype(g_o.dtype)
    kx = k * kk_ref[...]
    ss = _dot_exact_rhs(kx * kx, ones)
    kk = kx * lax.rsqrt(jnp.maximum(ss, 1e-24))
    k_sum = jnp.zeros_like(k)
    for d in range(2):
        wd = jnp.tanh(sm[:, d * lora_w:(d + 1) * lora_w])
        w = w0_ref[d] + _dot3(wd, wup_ref[d])
        w = -(jnp.maximum(-w, 0.0) + jnp.log(1.0 + jnp.exp(-jnp.abs(w)))) - 0.5
        lw_o[d] = -jnp.exp(w)
        ad = sm[:, o_a + d * lora_a:o_a + (d + 1) * lora_a]
        a = _sigmoid(a0_ref[d] + _dot3(ad, aup_ref[d]))
        kd = k * (1.0 + (a - 1.0) * ka_ref[...])
        kd_o[d] = kd.astype(kd_o.dtype)
        beta_o[d] = (kk * a).astype(beta_o.dtype)
        k_sum = k_sum + kd
    r_o[...] = r.astype(r_o.dtype)
    kk_o[...] = kk.astype(kk_o.dtype)
    v_o[...] = v.astype(v_o.dtype)
    bonus_o[...] = (_dot_exact_rhs(r * (k_sum * 0.5) * rk_ref[...], ones) * v).astype(bonus_o.dtype)


def _rw_prep(z_rkv, z_small, lp, *, bsz, l_ctx, l_lat, c_rw, head, gate_rank):
    n = z_rkv.shape[0]
    tn = RW_PACK * head
    nj = c_rw // tn
    halo = GRID_W
    tm = _pick(l_lat, _pick(l_ctx, 512))
    assert tm % halo == 0 and tm > halo
    n_small_all = z_small.shape[1]
    n_small = n_small_all - gate_rank
    lora_w = lp["rw_w_up"].shape[1]
    lora_a = lp["rw_a_up"].shape[1]
    lora_g = lp["rw_g_up"].shape[0]
    mu = lp["rw_mu"].reshape(1, -1)
    assert mu.shape[1] == 3 * c_rw + n_small and mu.shape[1] % 4 == 0
    mu_big = mu[:, :3 * c_rw]
    mu_small = mu[:, 3 * c_rw:]
    row = lambda a: a.reshape(1, c_rw)
    hb = tm // halo
    last_halo = n // halo - 1

    def tiles(off, width_blocks):
        w = tn if width_blocks else n_small_all
        col = (lambda j: j + off) if width_blocks else (lambda j: 0)
        return [pl.BlockSpec((tm, w), lambda i, j: (i, col(j))),
                pl.BlockSpec((halo, w), lambda i, j: (jnp.maximum(i * hb - 1, 0), col(j))),
                pl.BlockSpec((halo, w), lambda i, j: (jnp.minimum((i + 1) * hb, last_halo), col(j)))]

    vec = lambda off: pl.BlockSpec((1, tn), lambda i, j: (0, j + off))
    dirvec = pl.BlockSpec((2, 1, tn), lambda i, j: (0, 0, j))
    out_t = pl.BlockSpec((tm, tn), lambda i, j: (i, j))
    out_d = pl.BlockSpec((2, tm, tn), lambda i, j: (0, i, j))
    sds = jax.ShapeDtypeStruct
    kern = functools.partial(_rw_prep_kernel, lora_w=lora_w, lora_a=lora_a, small0=gate_rank, n_ctx=bsz * l_ctx,
                             l_ctx=l_ctx, l_lat=l_lat, c_rw=c_rw, quarter=mu.shape[1] // 4)
    return pl.pallas_call(
        kern,
        grid=(n // tm, nj),
        in_specs=[
            *tiles(0, True), *tiles(nj, True), *tiles(2 * nj, True), *tiles(0, False),
            vec(0), vec(nj), vec(2 * nj),
            pl.BlockSpec((1, n_small), lambda i, j: (0, 0)),
            dirvec,
            pl.BlockSpec((2, lora_w, tn), lambda i, j: (0, 0, j)),
            dirvec,
            pl.BlockSpec((2, lora_a, tn), lambda i, j: (0, 0, j)),
            pl.BlockSpec((lora_g, tn), lambda i, j: (0, j)),
            vec(0), vec(0), vec(0),
            pl.BlockSpec((tn, tn), lambda i, j: (0, 0)),
        ],
        out_specs=[out_t, out_t, out_t, out_d, out_d, out_d, out_t, out_t],
        out_shape=[sds((n, c_rw), BF16), sds((n, c_rw), BF16), sds((n, c_rw), BF16),
                   sds((2, n, c_rw), F32), sds((2, n, c_rw), BF16), sds((2, n, c_rw), BF16),
                   sds((n, c_rw), BF16), sds((n, c_rw), BF16)],
        compiler_params=_params(("arbitrary", "arbitrary")),
        name="rwkv_prep",
    )(*([z_rkv] * 9), *([z_small] * 3), mu_big, mu_big, mu_big, mu_small,
      lp["rw_w0"].reshape(2, 1, c_rw), lp["rw_w_up"], lp["rw_a0"].reshape(2, 1, c_rw), lp["rw_a_up"],
      lp["rw_g_up"].astype(BF16), row(lp["rw_k_k"]), row(lp["rw_k_a"]), row(lp["rw_r_k"]),
      _head_ones(tn, head))


def _rw_scan_kernel(r_ref, kk_ref, v_ref, lw_ref, kd_ref, beta_ref, y_ref, st_scr, *, head):
    t_len = r_ref.shape[0]
    n_pack = RW_PACK
    width = n_pack * head
    n_units = r_ref.shape[1] // width
    rev = pl.program_id(0) == 1

    @pl.when(pl.program_id(3) == 0)
    def _():
        st_scr[...] = jnp.zeros_like(st_scr)

    lane_head = lax.broadcasted_iota(jnp.int32, (1, width), 1) // head
    head_masks = [lane_head == h for h in range(n_pack)]

    def block_diag(x):
        xb = x.astype(BF16)
        zero = jnp.zeros_like(xb)
        return jnp.concatenate([jnp.where(m, xb, zero) for m in head_masks], axis=0)

    def mm(a, bd):
        return _dot(a.astype(BF16), bd)

    def mm_nt(a, bd):
        return lax.dot_general(a.astype(BF16), bd, (((1,), (1,)), ((), ())), preferred_element_type=F32)

    sgn = jnp.where(rev, -1, 1)
    tt = lax.broadcasted_iota(jnp.int32, (t_len, width), 0)
    ss = lax.broadcasted_iota(jnp.int32, (t_len, width), 1) % head
    order = (tt - ss) * sgn
    strict = order > 0
    incl = order >= 0
    eye = jnp.where(tt == ss, 1.0, 0.0)
    ti = lax.broadcasted_iota(jnp.int32, (t_len, t_len), 0)
    si = lax.broadcasted_iota(jnp.int32, (t_len, t_len), 1)
    tri = jnp.where((ti - si) * sgn >= 0, 1.0, 0.0).astype(BF16)

    units = [slice(u * width, (u + 1) * width) for u in range(n_units)]
    each = lambda f, *xs: [f(*a) for a in zip(*xs)]
    r = [r_ref[:, c] for c in units]
    kk = [kk_ref[:, c] for c in units]
    v = [v_ref[:, c] for c in units]
    lw = [lw_ref[:, c] for c in units]
    kd = [kd_ref[:, c] for c in units]
    beta = [beta_ref[:, c] for c in units]

    def cumulative(x):
        hi, lo = _split(x)
        return _dot(tri, hi) + _dot(tri, lo)

    log_p = each(cumulative, lw)
    log_pt = each(lambda x: jnp.sum(x, axis=0, keepdims=True), lw)
    p_inv = each(lambda x: jnp.exp(-x), log_p)
    a_bar = each(lambda a, p, w: a * jnp.exp(p - w), kk, log_p, lw)
    r_bar = each(lambda a, p: a * jnp.exp(p), r, log_p)
    b_til = each(jnp.multiply, beta, p_inv)
    k_til = each(jnp.multiply, kd, p_inv)
    p_rest = each(lambda t, p: jnp.exp(t - p), log_pt, log_p)
    b_end = each(jnp.multiply, beta, p_rest)
    k_end = each(jnp.multiply, kd, p_rest)

    ar = each(lambda a, b: jnp.concatenate([a, b], axis=0), a_bar, r_bar)
    gram_b = each(lambda a, b: mm_nt(a, block_diag(b)), ar, b_til)
    gram_k = each(lambda a, b: mm_nt(a, block_diag(b)), ar, k_til)
    l_b = each(lambda g: jnp.where(strict, g[:t_len], 0.0), gram_b)
    m_b = each(lambda g: jnp.where(incl, g[t_len:], 0.0), gram_b)
    l_k = each(lambda g: jnp.where(strict, g[:t_len], 0.0), gram_k)
    m_k = each(lambda g: jnp.where(incl, g[t_len:], 0.0), gram_k)

    stack = lambda a, b: jnp.concatenate([a, b], axis=0)
    t_inv = each(lambda x: eye - x, l_b)
    l_pow = each(lambda x: mm(x, block_diag(x)), l_b)
    span = 4
    while span < t_len:
        both = each(lambda t, x: mm(stack(t, x), block_diag(x)), t_inv, l_pow)
        t_inv = each(lambda t, p: t + p[:t_len], t_inv, both)
        l_pow = each(lambda p: p[t_len:], both)
        span *= 2
    t_inv = each(lambda t, x: t + mm(t, block_diag(x)), t_inv, l_pow)

    def head_t(x):
        return jnp.concatenate([x[:, h * head:(h + 1) * head].T for h in range(n_pack)], axis=1)

    bd_v = each(block_diag, v)
    w_til = each(lambda t, a: mm(t, block_diag(a)), t_inv, a_bar)
    lmv = each(lambda lk, mk, ke, bv: mm(jnp.concatenate([lk, mk, head_t(ke)], axis=0), bv), l_k, m_k, k_end, bd_v)
    u_til = each(lambda t, a: mm(t, block_diag(a[:t_len])), t_inv, lmv)
    mb_bt = each(lambda m, b: stack(m, head_t(b)), m_b, b_end)
    from_w = each(lambda a, w: mm(a, block_diag(w)), mb_bt, w_til)
    from_u = each(lambda a, u: mm(a, block_diag(u)), mb_bt, u_til)
    q_hat = each(lambda a, fw: a - fw[:t_len], r_bar, from_w)
    y_hat = each(lambda a, fu: a[t_len:2 * t_len] - fu[:t_len], lmv, from_u)
    g_mat = each(lambda t, fw: eye * jnp.exp(t) - fw[t_len:], log_pt, from_w)
    h_mat = each(lambda a, fu: a[2 * t_len:] - fu[t_len:], lmv, from_u)

    bd_s = [block_diag(st_scr[:, c]) for c in units]
    from_state = each(lambda q, g, s: mm(stack(q, g), s), q_hat, g_mat, bd_s)
    for c, fs, y0, h in zip(units, from_state, y_hat, h_mat):
        y_ref[:, c] = fs[:t_len] + y0
        st_scr[:, c] = fs[t_len:] + h


def _rw_scan(r, kk, v, lw, kd, beta, *, bsz, l_ctx, l_lat, head):
    n, c_rw = r.shape
    t_len = RW_CHUNK
    width = _pick(c_rw, RW_UNITS * RW_PACK * head)
    nc = l_ctx // t_len
    nl = l_lat // t_len
    ctx_blocks = bsz * nc

    def blk(d, b, c):
        cc = jnp.where(d == 1, nc - 1 - c, c)
        cl = jnp.where(d == 1, nl - 1 - (c - nc), c - nc)
        return jnp.where(c < nc, b * nc + cc, ctx_blocks + b * nl + cl)

    shared = pl.BlockSpec((t_len, width), lambda d, b, g, c: (blk(d, b, c), g))
    per_dir = pl.BlockSpec((None, t_len, width), lambda d, b, g, c: (d, blk(d, b, c), g))
    kern = functools.partial(_rw_scan_kernel, head=head)
    return pl.pallas_call(
        kern,
        grid=(2, bsz, c_rw // width, nc + nl),
        in_specs=[shared, shared, shared, per_dir, per_dir, per_dir],
        out_specs=per_dir,
        out_shape=jax.ShapeDtypeStruct((2, n, c_rw), F32),
        scratch_shapes=[pltpu.VMEM((head, width), F32)],
        compiler_params=_params(("arbitrary", "arbitrary", "arbitrary", "arbitrary")),
        name="rwkv_scan",
    )(r, kk, v, lw, kd, beta)


def _rw_readout_kernel(y_ref, g_ref, bonus_ref, lnw_ref, lnb_ref, ones_ref, o_ref, *, head):
    y = y_ref[0] + y_ref[1]
    ones = ones_ref[...]
    inv = 1.0 / head
    mu = _dot_exact_rhs(y, ones) * inv
    yc = y - mu
    var = _dot_exact_rhs(yc * yc, ones) * inv
    yn = yc * lax.rsqrt(var + GN_EPS) * lnw_ref[...] + lnb_ref[...]
    o_ref[...] = ((yn + bonus_ref[...]) * g_ref[...]).astype(o_ref.dtype)


def _rw_readout(y2, g, bonus, ln_w, ln_b, *, head, tm):
    _, n, c_rw = y2.shape
    tn = RW_PACK * head
    tile = pl.BlockSpec((tm, tn), lambda i, j: (i, j))
    vec = pl.BlockSpec((1, tn), lambda i, j: (0, j))
    kern = functools.partial(_rw_readout_kernel, head=head)
    return pl.pallas_call(
        kern,
        grid=(n // tm, c_rw // tn),
        in_specs=[pl.BlockSpec((2, tm, tn), lambda i, j: (0, i, j)), tile, tile, vec, vec,
                  pl.BlockSpec((tn, tn), lambda i, j: (0, 0))],
        out_specs=tile,
        out_shape=jax.ShapeDtypeStruct((n, c_rw), BF16),
        compiler_params=_params(("arbitrary", "arbitrary")),
        name="rwkv_readout",
    )(y2, g, bonus, ln_w.reshape(1, c_rw), ln_b.reshape(1, c_rw), _head_ones(tn, head))


def _final_norm_kernel(h_ref, gain_ref, o_ref):
    x = h_ref[...]
    ms = jnp.mean(x * x, axis=-1, keepdims=True)
    o_ref[...] = x * lax.rsqrt(ms + NORM_EPS) * gain_ref[...]


def _final_norm(h, gain, *, tm, row0):
    n, d = h.shape
    b0 = row0 // tm
    return pl.pallas_call(
        _final_norm_kernel,
        grid=((n - row0) // tm,),
        in_specs=[pl.BlockSpec((tm, d), lambda i: (i + b0, 0)), pl.BlockSpec((1, d), lambda i: (0, 0))],
        out_specs=pl.BlockSpec((tm, d), lambda i: (i, 0)),
        out_shape=jax.ShapeDtypeStruct((n - row0, d), F32),
        compiler_params=_params(("arbitrary",)),
        name="final_norm",
    )(h, gain.reshape(1, d))


def _to_s5_layout(u, bsz, l_ctx, l_lat, groups, gch):
    t = S5_CHUNK
    n_ctx = bsz * l_ctx
    uc = u[:n_ctx].reshape(bsz, l_ctx // t, t, groups, gch)
    ul = u[n_ctx:].reshape(bsz, l_lat // t, t, groups, gch)
    x = jnp.concatenate([uc, ul], axis=1).astype(BF16)
    x = jnp.transpose(x, (3, 1, 0, 2, 4))
    return x.reshape(groups, -1, t * gch)


def _from_s5_layout(y, bsz, l_ctx, l_lat, groups, gch):
    t = S5_CHUNK
    chunks = (l_ctx + l_lat) // t
    x = y.reshape(groups, chunks, bsz, t, gch)
    x = jnp.transpose(x, (2, 1, 3, 0, 4))
    nc = l_ctx // t
    yc = x[:, :nc].reshape(bsz * l_ctx, groups * gch)
    yl = x[:, nc:].reshape(bsz * l_lat, groups * gch)
    return jnp.concatenate([yc, yl], axis=0)


def kernel(x, c, ctx, c_ctx, ada_down, ada_up, ada_b, norm1, w_in, s5_a_re, s5_a_im, s5_log_dt, s5_b_re, s5_b_im, s5_c_re, s5_c_im, s5_d, s5_glu_w, s5_glu_b, rw_mu, rw_w0, rw_w_up, rw_a0, rw_a_up, rw_g_up, rw_k_k, rw_k_a, rw_r_k, rw_ln_w, rw_ln_b, w_proj_a, w_proj_b, gate_up, gate_b, w_out, norm2, mlp_w1, mlp_w2, norm_f):
    bsz, l_lat, d = x.shape
    l_ctx = ctx.shape[1]
    depth = w_in.shape[0]
    n_ctx = bsz * l_ctx
    n_lat = bsz * l_lat
    s5_groups, s5_state = s5_a_re.shape[2], s5_a_re.shape[3]
    s5_gch = s5_b_re.shape[-1]
    c_s5 = s5_groups * s5_gch
    rw_heads, head = rw_r_k.shape[1], rw_r_k.shape[2]
    c_rw = rw_heads * head
    lora_w, lora_a, lora_g = rw_w_up.shape[2], rw_a_up.shape[2], rw_g_up.shape[1]
    gate_rank = gate_up.shape[2]
    n_small_rw = 2 * lora_w + 2 * lora_a + lora_g
    c_big = c_s5 + 3 * c_rw
    assert w_in.shape[2] == c_big + n_small_rw + gate_rank
    assert bsz % 8 == 0 and l_ctx % RW_CHUNK == 0 and l_lat % RW_CHUNK == 0 and l_lat % GRID_W == 0
    assert s5_gch * S5_CHUNK == 256 and RW_PACK * head == 256 and c_rw % 256 == 0 and gate_rank % 128 == 0
    assert RW_CHUNK == head

    tm = _pick(l_lat, _pick(n_ctx, TM))

    def row_of_block(i):
        ctx_blocks = n_ctx // tm
        return jnp.where(i < ctx_blocks, bsz, (i - ctx_blocks) // (l_lat // tm))

    n_cond = ((bsz + 1 + 7) // 8) * 8
    cond = jnp.zeros((n_cond, d), F32).at[:bsz].set(c).at[bsz].set(c_ctx)
    mods_all = _ada_modulation(cond, ada_down, ada_up, ada_b)
    mods_all = jnp.transpose(mods_all.reshape(depth, n_cond, 6, 1, d), (0, 2, 1, 3, 4))

    h = jnp.concatenate([ctx.reshape(n_ctx, d), x.reshape(n_lat, d)], axis=0)

    for l in range(depth):
        last = l == depth - 1
        row0 = n_ctx if last else 0
        mods = mods_all[l]
        lp = {"rw_mu": rw_mu[l], "rw_w0": rw_w0[l], "rw_w_up": rw_w_up[l], "rw_a0": rw_a0[l],
              "rw_a_up": rw_a_up[l], "rw_g_up": rw_g_up[l], "rw_k_k": rw_k_k[l], "rw_k_a": rw_k_a[l],
              "rw_r_k": rw_r_k[l]}
        w_small = jnp.concatenate([w_in[l, :, c_big + n_small_rw:], w_in[l, :, c_big:c_big + n_small_rw]],
                                  axis=1).astype(BF16)
        z_u = _norm_matmul(h, norm1[l], mods, w_in[l, :, :c_s5].astype(BF16), row_of_block, tm=tm,
                           tn=_pick(c_s5, 1024), out_dtype=BF16, name="in_proj_s5")
        z_rkv = _norm_matmul(h, norm1[l], mods, w_in[l, :, c_s5:c_big].astype(BF16), row_of_block, tm=tm,
                             tn=_pick(3 * c_rw, 1024), out_dtype=F32, name="in_proj_rkv")
        z_small = _norm_matmul(h, norm1[l], mods, w_small, row_of_block, tm=tm, tn=w_small.shape[1],
                               out_dtype=F32, name="in_proj_small")

        kf, kb, bm, cmt, lam = _s5_matrices(s5_a_re[l], s5_a_im[l], s5_log_dt[l], s5_b_re[l], s5_b_im[l],
                                            s5_c_re[l], s5_c_im[l], s5_d[l])
        u_g = _to_s5_layout(z_u, bsz, l_ctx, l_lat, s5_groups, s5_gch)
        y_g = _s5_scan(u_g, kf, kb, bm, cmt, lam, n_ctx=l_ctx // S5_CHUNK, n_chunks=(l_ctx + l_lat) // S5_CHUNK, bsz=bsz)
        y_s5 = _from_s5_layout(y_g, bsz, l_ctx, l_lat, s5_groups, s5_gch)
        ya = _s5_glu(y_s5, s5_glu_w[l].astype(BF16), s5_glu_b[l], tm=tm)

        r, kk, v, lw, kd, beta, g, bonus = _rw_prep(z_rkv, z_small, lp, bsz=bsz, l_ctx=l_ctx, l_lat=l_lat,
                                                    c_rw=c_rw, head=head, gate_rank=gate_rank)
        y2 = _rw_scan(r, kk, v, lw, kd, beta, bsz=bsz, l_ctx=l_ctx, l_lat=l_lat, head=head)
        yb = _rw_readout(y2, g, bonus, rw_ln_w[l], rw_ln_b[l], head=head, tm=tm)

        h = _merge_out(ya, yb, z_small, w_proj_a[l].astype(BF16), w_proj_b[l].astype(BF16), gate_up[l].astype(BF16),
                       gate_b[l], w_out[l].astype(BF16), h, mods, row_of_block, tm=tm, row0=row0)
        h = _mlp(h, norm2[l], mods, mlp_w1[l].astype(BF16), mlp_w2[l].astype(BF16), row_of_block, tm=tm, row0=row0)

    out = _final_norm(h, norm_f, tm=tm, row0=n_ctx)
    return out.reshape(bsz, l_lat, d)
```

```python
import functools

import jax
import jax.numpy as jnp
from jax import lax
from jax.experimental import pallas as pl
from jax.experimental.pallas import tpu as pltpu

F32 = jnp.float32
BF16 = jnp.bfloat16

GRID_W = 64
NORM_EPS = 1e-6
GN_EPS = 64e-5
A_RE_MAX = -1e-4
S5_CHUNK = 16
RW_CHUNK = 64
RW_PACK = 4
RW_UNITS = 8
RW_PREP_WIDTH = 1024
TM = 512
IN_PROJ_TN = 1024
VMEM_LIMIT = 56 * 1024 * 1024


def _params(sem):
    return pltpu.CompilerParams(dimension_semantics=sem, vmem_limit_bytes=VMEM_LIMIT)


def _pick(n, pref):
    t = pref
    while n % t:
        t //= 2
    return t


def _dot(a, b):
    return jnp.dot(a, b, preferred_element_type=F32)


def _split(x):
    hi = x.astype(BF16)
    lo = (x - hi.astype(F32)).astype(BF16)
    return hi, lo


def _dot3(a, b):
    ah, al = _split(a)
    bh, bl = _split(b)
    return _dot(ah, bh) + (_dot(ah, bl) + _dot(al, bh))


def _dot_exact_rhs(a, b_bf16):
    ah, al = _split(a)
    return _dot(ah, b_bf16) + _dot(al, b_bf16)


def _sigmoid(x):
    return 1.0 / (1.0 + jnp.exp(-x))


def _gelu_tanh(x):
    c = 0.7978845608028654
    return 0.5 * x * (1.0 + jnp.tanh(c * (x + 0.044715 * (x * x * x))))


def _ada_kernel(cond_ref, wd_ref, wu_ref, b_ref, o_ref, mid_scr):
    @pl.when(pl.program_id(1) == 0)
    def _():
        c = cond_ref[...]
        mid_scr[...] = _dot3(c * _sigmoid(c), wd_ref[...])

    o_ref[...] = _dot3(mid_scr[...], wu_ref[...]) + b_ref[...]


def _ada_modulation(cond, w_down, w_up, bias):
    nl, d, rank = w_down.shape
    r = cond.shape[0]
    n_out = w_up.shape[-1]
    tn = _pick(n_out, 2048)
    return pl.pallas_call(
        _ada_kernel,
        grid=(nl, n_out // tn),
        in_specs=[
            pl.BlockSpec((r, d), lambda l, j: (0, 0)),
            pl.BlockSpec((None, d, rank), lambda l, j: (l, 0, 0)),
            pl.BlockSpec((None, rank, tn), lambda l, j: (l, 0, j)),
            pl.BlockSpec((None, 1, tn), lambda l, j: (l, 0, j)),
        ],
        out_specs=pl.BlockSpec((None, r, tn), lambda l, j: (l, 0, j)),
        out_shape=jax.ShapeDtypeStruct((nl, r, n_out), F32),
        scratch_shapes=[pltpu.VMEM((r, rank), F32)],
        compiler_params=_params(("arbitrary", "arbitrary")),
        name="ada_modulation",
    )(cond, w_down, w_up, bias.reshape(nl, 1, n_out))


def _in_proj_kernel(h_ref, gain_ref, shift_ref, scale_ref, w_ref, zu_ref, zrkv_ref, zs_ref, n_scr, *, ju, jr):
    j = pl.program_id(1)

    @pl.when(j == 0)
    def _():
        x = h_ref[...]
        ms = jnp.mean(x * x, axis=-1, keepdims=True)
        xn = x * lax.rsqrt(ms + NORM_EPS) * gain_ref[...]
        n_scr[...] = (xn * (1.0 + scale_ref[...]) + shift_ref[...]).astype(BF16)

    acc = _dot(n_scr[...], w_ref[...])

    @pl.when(j < ju)
    def _():
        zu_ref[...] = acc.astype(zu_ref.dtype)

    @pl.when(jnp.logical_and(j >= ju, j < ju + jr))
    def _():
        zrkv_ref[...] = acc

    @pl.when(j == ju + jr)
    def _():
        zs_ref[...] = acc


def _in_proj(h, gain, mods, w_all, row_of_block, *, tm, tn, c_u, c_rkv):
    n, d = h.shape
    ju, jr = c_u // tn, c_rkv // tn
    assert w_all.shape[1] == (ju + jr + 1) * tn
    mod = lambda which: pl.BlockSpec((None, None, 1, d), lambda i, j: (which, row_of_block(i), 0, 0))
    sds = jax.ShapeDtypeStruct
    return pl.pallas_call(
        functools.partial(_in_proj_kernel, ju=ju, jr=jr),
        grid=(n // tm, ju + jr + 1),
        in_specs=[
            pl.BlockSpec((tm, d), lambda i, j: (i, 0), pipeline_mode=pl.Buffered(1)),
            pl.BlockSpec((1, d), lambda i, j: (0, 0)),
            mod(0), mod(1),
            pl.BlockSpec((d, tn), lambda i, j: (0, j)),
        ],
        out_specs=[pl.BlockSpec((tm, tn), lambda i, j: (i, jnp.minimum(j, ju - 1))),
                   pl.BlockSpec((tm, tn), lambda i, j: (i, jnp.clip(j - ju, 0, jr - 1))),
                   pl.BlockSpec((tm, tn), lambda i, j: (i, 0))],
        out_shape=[sds((n, c_u), BF16), sds((n, c_rkv), F32), sds((n, tn), F32)],
        scratch_shapes=[pltpu.VMEM((tm, d), BF16)],
        compiler_params=_params(("arbitrary", "arbitrary")),
        name="in_proj",
    )(h, gain.reshape(1, d), mods, mods, w_all)


def _mlp_kernel(h_ref, gain_ref, shift_ref, scale_ref, gate_ref, w1_ref, w2_ref, o_ref, n_scr):
    j = pl.program_id(1)

    @pl.when(j == 0)
    def _():
        x = h_ref[...]
        ms = jnp.mean(x * x, axis=-1, keepdims=True)
        xn = x * lax.rsqrt(ms + NORM_EPS) * gain_ref[...]
        n_scr[...] = (xn * (1.0 + scale_ref[...]) + shift_ref[...]).astype(BF16)
        o_ref[...] = jnp.zeros_like(o_ref)

    hid = jnp.square(jnp.maximum(_dot(n_scr[...], w1_ref[...]), 0.0)).astype(BF16)
    o_ref[...] += _dot(hid, w2_ref[...])

    @pl.when(j == pl.num_programs(1) - 1)
    def _():
        o_ref[...] = h_ref[...] + gate_ref[...] * o_ref[...]


def _mlp(h, gain, mods, w1, w2, row_of_block, *, tm, row0):
    n, d = h.shape
    d_ff = w1.shape[1]
    tf = _pick(d_ff, 512)
    b0 = row0 // tm
    mod = lambda which: pl.BlockSpec((None, None, 1, d), lambda i, j: (which, row_of_block(i + b0), 0, 0))
    return pl.pallas_call(
        _mlp_kernel,
        grid=((n - row0) // tm, d_ff // tf),
        in_specs=[
            pl.BlockSpec((tm, d), lambda i, j: (i + b0, 0), pipeline_mode=pl.Buffered(1)),
            pl.BlockSpec((1, d), lambda i, j: (0, 0)),
            mod(3), mod(4), mod(5),
            pl.BlockSpec((d, tf), lambda i, j: (0, j)),
            pl.BlockSpec((tf, d), lambda i, j: (j, 0)),
        ],
        out_specs=pl.BlockSpec((tm, d), lambda i, j: (i + b0, 0)),
        out_shape=jax.ShapeDtypeStruct((n, d), F32),
        scratch_shapes=[pltpu.VMEM((tm, d), BF16)],
        input_output_aliases={0: 0},
        compiler_params=_params(("arbitrary", "arbitrary")),
        name="mlp",
    )(h, gain.reshape(1, d), mods, mods, mods, w1, w2)


def _merge_out_kernel(ya_ref, yb_ref, gz_ref, wa_ref, wb_ref, ga_ref, gb_ref, ba_ref, bb_ref, wo_ref, h_ref,
                      gate_ref, o_ref):
    j = pl.program_id(1)

    @pl.when(j == 0)
    def _():
        o_ref[...] = jnp.zeros_like(o_ref)

    gz = gz_ref[...].astype(BF16)
    ga = _sigmoid(_dot(gz, ga_ref[...]) + ba_ref[...])
    gb = _sigmoid(_dot(gz, gb_ref[...]) + bb_ref[...])
    mixed = (ga * _dot(ya_ref[...], wa_ref[...]) + gb * _dot(yb_ref[...], wb_ref[...])).astype(BF16)
    o_ref[...] += _dot(mixed, wo_ref[...])

    @pl.when(j == pl.num_programs(1) - 1)
    def _():
        o_ref[...] = h_ref[...] + gate_ref[...] * o_ref[...]


def _merge_out(ya, yb, z_small, wa, wb, gate_up, gate_b, w_out, h, mods, row_of_block, *, tm, row0):
    n, d = h.shape
    ca, cb = ya.shape[1], yb.shape[1]
    d_mid = wa.shape[1]
    rank = gate_up.shape[1]
    tn = _pick(d_mid, 512)
    b0 = row0 // tm
    once = pl.Buffered(1)
    return pl.pallas_call(
        _merge_out_kernel,
        grid=((n - row0) // tm, d_mid // tn),
        in_specs=[
            pl.BlockSpec((tm, ca), lambda i, j: (i + b0, 0), pipeline_mode=once),
            pl.BlockSpec((tm, cb), lambda i, j: (i + b0, 0), pipeline_mode=once),
            pl.BlockSpec((tm, rank), lambda i, j: (i + b0, 0)),
            pl.BlockSpec((ca, tn), lambda i, j: (0, j)),
            pl.BlockSpec((cb, tn), lambda i, j: (0, j)),
            pl.BlockSpec((None, rank, tn), lambda i, j: (0, 0, j)),
            pl.BlockSpec((None, rank, tn), lambda i, j: (1, 0, j)),
            pl.BlockSpec((None, 1, tn), lambda i, j: (0, 0, j)),
            pl.BlockSpec((None, 1, tn), lambda i, j: (1, 0, j)),
            pl.BlockSpec((tn, d), lambda i, j: (j, 0)),
            pl.BlockSpec((tm, d), lambda i, j: (i + b0, 0), pipeline_mode=once),
            pl.BlockSpec((None, None, 1, d), lambda i, j: (2, row_of_block(i + b0), 0, 0)),
        ],
        out_specs=pl.BlockSpec((tm, d), lambda i, j: (i + b0, 0)),
        out_shape=jax.ShapeDtypeStruct((n, d), F32),
        input_output_aliases={10: 0},
        compiler_params=_params(("arbitrary", "arbitrary")),
        name="merge_out_proj",
    )(ya, yb, z_small, wa, wb, gate_up, gate_up, gate_b.reshape(2, 1, d_mid), gate_b.reshape(2, 1, d_mid),
      w_out, h, mods)


def _s5_matrices(a_re, a_im, log_dt, b_re, b_im, c_re, c_im, d_skip):
    t_len = S5_CHUNK
    hp = lax.Precision.HIGHEST
    lam_re = jnp.minimum(a_re.astype(F32), A_RE_MAX)
    lam_im = a_im.astype(F32)
    dt = jnp.exp(log_dt.astype(F32))[..., None]
    j = jnp.arange(t_len + 1, dtype=F32)[:, None, None, None]
    mag = jnp.exp(j * (lam_re * dt))
    ang = j * (lam_im * dt)
    pw_re = mag * jnp.cos(ang)
    pw_im = mag * jnp.sin(ang)
    nr, ni = pw_re[1] - 1.0, pw_im[1]
    den = lam_re * lam_re + lam_im * lam_im
    qr = (nr * lam_re + ni * lam_im) / den
    qi = (ni * lam_re - nr * lam_im) / den
    bbr = qr[..., None] * b_re - qi[..., None] * b_im
    bbi = qr[..., None] * b_im + qi[..., None] * b_re
    n_dir, g, p, c = bbr.shape
    clr = c_re[None] * pw_re[:t_len, :, :, None, :] - c_im[None] * pw_im[:t_len, :, :, None, :]
    cli = c_re[None] * pw_im[:t_len, :, :, None, :] + c_im[None] * pw_re[:t_len, :, :, None, :]
    kj = (jnp.einsum("jdgop,dgpi->dgijo", clr, bbr, precision=hp)
          - jnp.einsum("jdgop,dgpi->dgijo", cli, bbi, precision=hp))
    lag0 = (jnp.arange(t_len) == 0).astype(F32)
    skip = d_skip.astype(F32).reshape(g, c)
    kf = kj[0] + skip[:, :, None, None] * lag0[None, None, :, None] * jnp.eye(c, dtype=F32)[None, :, None, :]
    kf = kf.reshape(g, c, t_len * c)
    kb = jnp.flip(kj[1], axis=2).reshape(g, c, t_len * c)
    bbr_t = jnp.transpose(bbr, (0, 1, 3, 2))
    bbi_t = jnp.transpose(bbi, (0, 1, 3, 2))

    def powers(d, e):
        return jnp.transpose(pw_re[e, d], (1, 0, 2)), jnp.transpose(pw_im[e, d], (1, 0, 2))

    def state_in(d, e):
        pr, pi = powers(d, e)
        re = pr[:, :, None, :] * bbr_t[d][:, None] - pi[:, :, None, :] * bbi_t[d][:, None]
        im = pr[:, :, None, :] * bbi_t[d][:, None] + pi[:, :, None, :] * bbr_t[d][:, None]
        return re.reshape(g, t_len * c, p), im.reshape(g, t_len * c, p)

    fr, fi = state_in(0, (t_len - 1) - jnp.arange(t_len))
    br, bi = state_in(1, jnp.arange(t_len))
    bm = jnp.concatenate([fr, br, fi, bi], axis=-1)

    def state_out(d, e):
        pr, pi = powers(d, e)
        re = c_re[d][:, None] * pr[:, :, None, :] - c_im[d][:, None] * pi[:, :, None, :]
        im = c_re[d][:, None] * pi[:, :, None, :] + c_im[d][:, None] * pr[:, :, None, :]
        return re.reshape(g, t_len * c, p), -im.reshape(g, t_len * c, p)

    cfr, cfi = state_out(0, jnp.arange(t_len) + 1)
    cbr, cbi = state_out(1, t_len - jnp.arange(t_len))
    cmt = jnp.concatenate([cfr, cbr, cfi, cbi], axis=-1)
    lam = jnp.stack([jnp.concatenate([pw_re[t_len, 0], pw_re[t_len, 1]], axis=-1),
                     jnp.concatenate([pw_im[t_len, 0], pw_im[t_len, 1]], axis=-1)], axis=1)
    return kf, kb, bm.astype(BF16), cmt.astype(BF16), lam


def _s5_kernel(u_ref, bm_ref, kf_ref, kb_ref, cmt_ref, lam_ref, y_ref, xin_scr, fwd_scr, bwd_scr, km_scr, *,
               n_ctx, n_chunks, bsz, p_state):
    u = u_ref[...]
    xin_scr[...] = _dot(u, bm_ref[...])
    gch, tc = kf_ref.shape
    kf = kf_ref[...]
    kb = kb_ref[...]
    lane_k = lax.broadcasted_iota(jnp.int32, (gch, tc), 1)
    for s in range(tc // gch):
        fwd = kf if s == 0 else jnp.where(lane_k >= gch * s, pltpu.roll(kf, gch * s, axis=1), 0.0)
        shift = (gch * (s + 1)) % tc
        bwd = kb if shift == 0 else jnp.where(lane_k < gch * (s + 1), pltpu.roll(kb, shift, axis=1), 0.0)
        km_scr[s * gch:(s + 1) * gch, :] = (fwd + bwd).astype(BF16)
    two_p = 2 * p_state
    lam_r = jnp.broadcast_to(lam_ref[0:1, :], (bsz, two_p))
    lam_i = jnp.broadcast_to(lam_ref[1:2, :], (bsz, two_p))
    is_fwd = lax.broadcasted_iota(jnp.int32, (bsz, two_p), 1) < p_state

    def body(i, carry):
        re, im = carry
        pb = jnp.where(i < n_ctx, n_ctx - 1 - i, n_chunks + n_ctx - 1 - i)
        rf = pl.multiple_of(i * bsz, bsz)
        rb = pl.multiple_of(pb * bsz, bsz)
        xf = xin_scr[pl.ds(rf, bsz), :]
        xb = xin_scr[pl.ds(rb, bsz), :]
        x_re = jnp.where(is_fwd, xf[:, :two_p], xb[:, :two_p])
        x_im = jnp.where(is_fwd, xf[:, two_p:], xb[:, two_p:])
        state = jnp.concatenate([re, im], axis=1)
        fwd_scr[pl.ds(rf, bsz), :] = state
        bwd_scr[pl.ds(rb, bsz), :] = state
        return (re * lam_r - im * lam_i + x_re, re * lam_i + im * lam_r + x_im)

    zero = jnp.zeros((bsz, two_p), F32)
    lax.fori_loop(0, n_chunks, body, (zero, zero))
    lane = lax.broadcasted_iota(jnp.int32, (1, 2 * two_p), 1)
    fwd_lane = (lane % two_p) < p_state
    xs = jnp.where(fwd_lane, fwd_scr[...], bwd_scr[...]).astype(BF16)
    from_state = lax.dot_general(xs, cmt_ref[...], (((1,), (1,)), ((), ())), preferred_element_type=F32)
    y_ref[...] = (_dot(u, km_scr[...]) + from_state).astype(y_ref.dtype)


def _s5_scan(u_g, kf, kb, bm, cmt, lam, *, n_ctx, n_chunks, bsz):
    g, rows, tc = u_g.shape
    gch = kf.shape[1]
    p4 = bm.shape[-1]
    kern = functools.partial(_s5_kernel, n_ctx=n_ctx, n_chunks=n_chunks, bsz=bsz, p_state=p4 // 4)
    return pl.pallas_call(
        kern,
        grid=(g,),
        in_specs=[
            pl.BlockSpec((None, rows, tc), lambda i: (i, 0, 0)),
            pl.BlockSpec((None, tc, p4), lambda i: (i, 0, 0)),
            pl.BlockSpec((None, gch, tc), lambda i: (i, 0, 0)),
            pl.BlockSpec((None, gch, tc), lambda i: (i, 0, 0)),
            pl.BlockSpec((None, tc, p4), lambda i: (i, 0, 0)),
            pl.BlockSpec((None, 2, p4 // 2), lambda i: (i, 0, 0)),
        ],
        out_specs=pl.BlockSpec((None, rows, tc), lambda i: (i, 0, 0)),
        out_shape=jax.ShapeDtypeStruct((g, rows, tc), BF16),
        scratch_shapes=[pltpu.VMEM((rows, p4), F32), pltpu.VMEM((rows, p4), F32), pltpu.VMEM((rows, p4), F32),
                        pltpu.VMEM((tc, tc), BF16)],
        compiler_params=_params(("arbitrary",)),
        name="s5_scan",
    )(u_g, bm, kf, kb, cmt, lam)


def _glu_kernel(yc_ref, yl_ref, w_ref, b_ref, o_ref, *, ctx_blocks):
    def glu(y):
        z = _gelu_tanh(y.astype(F32))
        o_ref[...] = (z * _sigmoid(_dot(z.astype(BF16), w_ref[...]) + b_ref[...])).astype(o_ref.dtype)

    is_ctx = pl.program_id(0) < ctx_blocks

    @pl.when(is_ctx)
    def _():
        glu(yc_ref[...])

    @pl.when(jnp.logical_not(is_ctx))
    def _():
        glu(yl_ref[...])


def _s5_glu(y_ctx, y_lat, w, b, *, tm):
    c = y_ctx.shape[1]
    cb = y_ctx.shape[0] // tm
    n = y_ctx.shape[0] + y_lat.shape[0]
    return pl.pallas_call(
        functools.partial(_glu_kernel, ctx_blocks=cb),
        grid=(n // tm,),
        in_specs=[
            pl.BlockSpec((tm, c), lambda i: (jnp.minimum(i, cb - 1), 0)),
            pl.BlockSpec((tm, c), lambda i: (jnp.maximum(i - cb, 0), 0)),
            pl.BlockSpec((c, c), lambda i: (0, 0)),
            pl.BlockSpec((1, c), lambda i: (0, 0)),
        ],
        out_specs=pl.BlockSpec((tm, c), lambda i: (i, 0)),
        out_shape=jax.ShapeDtypeStruct((n, c), BF16),
        compiler_params=_params(("arbitrary",)),
        name="s5_glu",
    )(y_ctx, y_lat, w, b.reshape(1, c))


def _lane_select_matrices(gch):
    m = 256 // gch
    a = jnp.arange(m)[:, None, None, None]
    b = jnp.arange(m)[None, :, None, None]
    k = jnp.arange(256)[None, None, :, None]
    n = jnp.arange(256)[None, None, None, :]
    sel = (k == a * gch + n % gch) & (n // gch == b)
    return sel.astype(BF16).reshape(m, m * 256, 256)


def _lane_perm_kernel(*refs):
    *x_refs, sel_ref, o_ref = refs
    off = 0
    for x_ref in x_refs:
        rows = x_ref.shape[0]
        o_ref[off:off + rows, :] = _dot(x_ref[...], sel_ref[...]).astype(o_ref.dtype)
        off += rows


def _lane_perm(xs, sel):
    nj = xs[0].shape[0]
    m, kdim, _ = sel.shape
    rows = sum(x.shape[1] for x in xs)
    return pl.pallas_call(
        _lane_perm_kernel,
        grid=(nj, m),
        in_specs=[pl.BlockSpec((None, x.shape[1], kdim), lambda j, a: (j, 0, 0)) for x in xs]
        + [pl.BlockSpec((None, kdim, 256), lambda j, a: (a, 0, 0))],
        out_specs=pl.BlockSpec((None, rows, 256), lambda j, a: (j * m + a, 0, 0)),
        out_shape=jax.ShapeDtypeStruct((nj * m, rows, 256), BF16),
        compiler_params=_params(("arbitrary", "arbitrary")),
        name="s5_lane_perm",
    )(*xs, sel)


def _head_ones(width, head):
    i = jnp.arange(width) // head
    return (i[:, None] == i[None, :]).astype(BF16)


def _rw_prep_kernel(zr_ref, zrp_ref, zrn_ref, zk_ref, zkp_ref, zkn_ref, zv_ref, zvp_ref, zvn_ref,
                    zs_ref, zsp_ref, zsn_ref, mur_ref, muk_ref, muv_ref, mus_ref,
                    w0_ref, wup_ref, a0_ref, aup_ref, gup_ref, kk_ref, ka_ref, rk_ref, ones_ref,
                    r_o, kk_o, v_o, lw_o, kd_o, beta_o, g_o, bonus_o, *,
                    lora_w, lora_a, small0, n_ctx, l_ctx, l_lat, c_rw, quarter):
    tm, width = zr_ref.shape
    halo = zrp_ref.shape[0]
    g0 = pl.program_id(0) * tm
    is_ctx = g0 < n_ctx
    pos = jnp.where(is_ctx, g0 % l_ctx, (g0 - n_ctx) % l_lat) + lax.broadcasted_iota(jnp.int32, (tm, 1), 0)
    row = lax.broadcasted_iota(jnp.int32, (tm, 1), 0)
    ok_m1 = jnp.where(is_ctx, pos, pos % GRID_W) >= 1
    ok_p1 = jnp.where(is_ctx, l_ctx - 1 - pos, GRID_W - 1 - pos % GRID_W) >= 1
    ok_mw = jnp.where(is_ctx, 0, pos) >= GRID_W
    ok_pw = jnp.where(is_ctx, 0, l_lat - GRID_W - pos) >= 1

    def mix(cur, prev, nxt, mu, col0):
        width = cur.shape[1]
        col = col0 + lax.broadcasted_iota(jnp.int32, (1, width), 1)
        cq = col // quarter
        cls = jnp.where(is_ctx, cq // 2, cq)
        m1 = jnp.where(row == 0, prev[halo - 1:halo, :], pltpu.roll(cur, 1, axis=0))
        p1 = jnp.where(row == tm - 1, nxt[0:1, :], pltpu.roll(cur, tm - 1, axis=0))
        mw = jnp.concatenate([prev, cur[:tm - halo]], axis=0)
        pw = jnp.concatenate([cur[halo:], nxt], axis=0)
        zs = jnp.where(cls == 0, jnp.where(ok_m1, m1, 0.0),
                       jnp.where(cls == 1, jnp.where(ok_p1, p1, 0.0),
                                 jnp.where(cls == 2, jnp.where(ok_mw, mw, 0.0), jnp.where(ok_pw, pw, 0.0))))
        return cur + (zs - cur) * mu

    sm = mix(zs_ref[:, small0:], zsp_ref[:, small0:], zsn_ref[:, small0:], mus_ref[...], 3 * c_rw)
    o_a = 2 * lora_w
    o_g = o_a + 2 * lora_a
    ones = ones_ref[...]
    tn = ones.shape[0]
    gate_in = _sigmoid(sm[:, o_g:]).astype(BF16)
    wd = [jnp.tanh(sm[:, d * lora_w:(d + 1) * lora_w]) for d in range(2)]
    ad = [sm[:, o_a + d * lora_a:o_a + (d + 1) * lora_a] for d in range(2)]
    j0 = pl.program_id(1) * width
    for s in range(width // tn):
        cs = slice(s * tn, (s + 1) * tn)
        r = mix(zr_ref[:, cs], zrp_ref[:, cs], zrn_ref[:, cs], mur_ref[:, cs], j0 + s * tn)
        k = mix(zk_ref[:, cs], zkp_ref[:, cs], zkn_ref[:, cs], muk_ref[:, cs], c_rw + j0 + s * tn)
        v = mix(zv_ref[:, cs], zvp_ref[:, cs], zvn_ref[:, cs], muv_ref[:, cs], 2 * c_rw + j0 + s * tn)
        g_o[:, cs] = _dot(gate_in, gup_ref[:, cs]).astype(g_o.dtype)
        kx = k * kk_ref[:, cs]
        ss = _dot_exact_rhs(kx * kx, ones)
        kk = kx * lax.rsqrt(jnp.maximum(ss, 1e-24))
        k_sum = jnp.zeros_like(k)
        for d in range(2):
            w = w0_ref[d][:, cs] + _dot3(wd[d], wup_ref[d][:, cs])
            w = -(jnp.maximum(-w, 0.0) + jnp.log(1.0 + jnp.exp(-jnp.abs(w)))) - 0.5
            lw_o[d, :, cs] = -jnp.exp(w)
            a = _sigmoid(a0_ref[d][:, cs] + _dot3(ad[d], aup_ref[d][:, cs]))
            kd = k * (1.0 + (a - 1.0) * ka_ref[:, cs])
            kd_o[d, :, cs] = kd.astype(kd_o.dtype)
            beta_o[d, :, cs] = (kk * a).astype(beta_o.dtype)
            k_sum = k_sum + kd
        r_o[:, cs] = r.astype(r_o.dtype)
        kk_o[:, cs] = kk.astype(kk_o.dtype)
        v_o[:, cs] = v.astype(v_o.dtype)
        bonus_o[:, cs] = (_dot_exact_rhs(r * (k_sum * 0.5) * rk_ref[:, cs], ones) * v).astype(bonus_o.dtype)


def _rw_prep(z_rkv, z_small, lp, *, bsz, l_ctx, l_lat, c_rw, head, gate_rank):
    n = z_rkv.shape[0]
    pack = RW_PACK * head
    tn = _pick(c_rw, RW_PREP_WIDTH)
    nj = c_rw // tn
    halo = GRID_W
    tm = _pick(l_lat, _pick(l_ctx, 512))
    assert tm % halo == 0 and tm > halo
    lora_w = lp["rw_w_up"].shape[1]
    lora_a = lp["rw_a_up"].shape[1]
    lora_g = lp["rw_g_up"].shape[0]
    mu = lp["rw_mu"].reshape(1, -1)
    n_small = 2 * lora_w + 2 * lora_a + lora_g
    n_small_all = gate_rank + n_small
    assert mu.shape[1] == 3 * c_rw + n_small and mu.shape[1] % 4 == 0 and n_small_all % 128 == 0
    mu_big = mu[:, :3 * c_rw]
    mu_small = mu[:, 3 * c_rw:]
    row = lambda a: a.reshape(1, c_rw)
    hb = tm // halo
    last_halo = n // halo - 1

    def tiles(off, width_blocks):
        w = tn if width_blocks else n_small_all
        col = (lambda j: j + off) if width_blocks else (lambda j: 0)
        return [pl.BlockSpec((tm, w), lambda i, j: (i, col(j))),
                pl.BlockSpec((halo, w), lambda i, j: (jnp.maximum(i * hb - 1, 0), col(j))),
                pl.BlockSpec((halo, w), lambda i, j: (jnp.minimum((i + 1) * hb, last_halo), col(j)))]

    vec = lambda off: pl.BlockSpec((1, tn), lambda i, j: (0, j + off))
    dirvec = pl.BlockSpec((2, 1, tn), lambda i, j: (0, 0, j))
    out_t = pl.BlockSpec((tm, tn), lambda i, j: (i, j))
    out_d = pl.BlockSpec((2, tm, tn), lambda i, j: (0, i, j))
    sds = jax.ShapeDtypeStruct
    kern = functools.partial(_rw_prep_kernel, lora_w=lora_w, lora_a=lora_a, small0=gate_rank, n_ctx=bsz * l_ctx,
                             l_ctx=l_ctx, l_lat=l_lat, c_rw=c_rw, quarter=mu.shape[1] // 4)
    return pl.pallas_call(
        kern,
        grid=(n // tm, nj),
        in_specs=[
            *tiles(0, True), *tiles(nj, True), *tiles(2 * nj, True), *tiles(0, False),
            vec(0), vec(nj), vec(2 * nj),
            pl.BlockSpec((1, n_small), lambda i, j: (0, 0)),
            dirvec,
            pl.BlockSpec((2, lora_w, tn), lambda i, j: (0, 0, j)),
            dirvec,
            pl.BlockSpec((2, lora_a, tn), lambda i, j: (0, 0, j)),
            pl.BlockSpec((lora_g, tn), lambda i, j: (0, j)),
            vec(0), vec(0), vec(0),
            pl.BlockSpec((pack, pack), lambda i, j: (0, 0)),
        ],
        out_specs=[out_t, out_t, out_t, out_d, out_d, out_d, out_t, out_t],
        out_shape=[sds((n, c_rw), BF16), sds((n, c_rw), BF16), sds((n, c_rw), BF16),
                   sds((2, n, c_rw), F32), sds((2, n, c_rw), BF16), sds((2, n, c_rw), BF16),
                   sds((n, c_rw), BF16), sds((n, c_rw), BF16)],
        compiler_params=_params(("arbitrary", "arbitrary")),
        name="rwkv_prep",
    )(*([z_rkv] * 9), *([z_small] * 3), mu_big, mu_big, mu_big, mu_small,
      lp["rw_w0"].reshape(2, 1, c_rw), lp["rw_w_up"], lp["rw_a0"].reshape(2, 1, c_rw), lp["rw_a_up"],
      lp["rw_g_up"].astype(BF16), row(lp["rw_k_k"]), row(lp["rw_k_a"]), row(lp["rw_r_k"]),
      _head_ones(pack, head))


def _rw_scan_kernel(r_ref, kk_ref, v_ref, lw_ref, kd_ref, beta_ref, y_ref, st_scr, *, head):
    t_len = r_ref.shape[0]
    n_pack = RW_PACK
    width = n_pack * head
    n_units = r_ref.shape[1] // width
    rev = pl.program_id(0) == 1

    @pl.when(pl.program_id(3) == 0)
    def _():
        st_scr[...] = jnp.zeros_like(st_scr)

    lane_head = lax.broadcasted_iota(jnp.int32, (1, width), 1) // head
    head_masks = [lane_head == h for h in range(n_pack)]

    def block_diag(x):
        xb = x.astype(BF16)
        zero = jnp.zeros_like(xb)
        return jnp.concatenate([jnp.where(m, xb, zero) for m in head_masks], axis=0)

    def mm(a, bd):
        return _dot(a.astype(BF16), bd)

    def mm_nt(a, bd):
        return lax.dot_general(a.astype(BF16), bd, (((1,), (1,)), ((), ())), preferred_element_type=F32)

    sgn = jnp.where(rev, -1, 1)
    tt = lax.broadcasted_iota(jnp.int32, (t_len, width), 0)
    ss = lax.broadcasted_iota(jnp.int32, (t_len, width), 1) % head
    order = (tt - ss) * sgn
    strict = order > 0
    incl = order >= 0
    eye = jnp.where(tt == ss, 1.0, 0.0)
    ti = lax.broadcasted_iota(jnp.int32, (t_len, t_len), 0)
    si = lax.broadcasted_iota(jnp.int32, (t_len, t_len), 1)
    tri = jnp.where((ti - si) * sgn >= 0, 1.0, 0.0).astype(BF16)

    units = [slice(u * width, (u + 1) * width) for u in range(n_units)]
    each = lambda f, *xs: [f(*a) for a in zip(*xs)]
    r = [r_ref[:, c] for c in units]
    kk = [kk_ref[:, c] for c in units]
    v = [v_ref[:, c] for c in units]
    lw = [lw_ref[:, c] for c in units]
    kd = [kd_ref[:, c] for c in units]
    beta = [beta_ref[:, c] for c in units]

    def cumulative(x):
        hi, lo = _split(x)
        return _dot(tri, hi) + _dot(tri, lo)

    log_p = each(cumulative, lw)
    log_pt = each(lambda x: jnp.sum(x, axis=0, keepdims=True), lw)
    p_inv = each(lambda x: jnp.exp(-x), log_p)
    a_bar = each(lambda a, p, w: a * jnp.exp(p - w), kk, log_p, lw)
    r_bar = each(lambda a, p: a * jnp.exp(p), r, log_p)
    b_til = each(jnp.multiply, beta, p_inv)
    k_til = each(jnp.multiply, kd, p_inv)
    p_rest = each(lambda t, p: jnp.exp(t - p), log_pt, log_p)
    b_end = each(jnp.multiply, beta, p_rest)
    k_end = each(jnp.multiply, kd, p_rest)

    ar = each(lambda a, b: jnp.concatenate([a, b], axis=0), a_bar, r_bar)
    gram_b = each(lambda a, b: mm_nt(a, block_diag(b)), ar, b_til)
    gram_k = each(lambda a, b: mm_nt(a, block_diag(b)), ar, k_til)
    l_b = each(lambda g: jnp.where(strict, g[:t_len], 0.0), gram_b)
    m_b = each(lambda g: jnp.where(incl, g[t_len:], 0.0), gram_b)
    l_k = each(lambda g: jnp.where(strict, g[:t_len], 0.0), gram_k)
    m_k = each(lambda g: jnp.where(incl, g[t_len:], 0.0), gram_k)

    stack = lambda a, b: jnp.concatenate([a, b], axis=0)
    t_inv = each(lambda x: eye - x, l_b)
    l_pow = each(lambda x: mm(x, block_diag(x)), l_b)
    span = 4
    while span < t_len:
        both = each(lambda t, x: mm(stack(t, x), block_diag(x)), t_inv, l_pow)
        t_inv = each(lambda t, p: t + p[:t_len], t_inv, both)
        l_pow = each(lambda p: p[t_len:], both)
        span *= 2
    t_inv = each(lambda t, x: t + mm(t, block_diag(x)), t_inv, l_pow)

    def head_t(x):
        return jnp.concatenate([x[:, h * head:(h + 1) * head].T for h in range(n_pack)], axis=1)

    bd_v = each(block_diag, v)
    w_til = each(lambda t, a: mm(t, block_diag(a)), t_inv, a_bar)
    lmv = each(lambda lk, mk, ke, bv: mm(jnp.concatenate([lk, mk, head_t(ke)], axis=0), bv), l_k, m_k, k_end, bd_v)
    u_til = each(lambda t, a: mm(t, block_diag(a[:t_len])), t_inv, lmv)
    mb_bt = each(lambda m, b: stack(m, head_t(b)), m_b, b_end)
    from_w = each(lambda a, w: mm(a, block_diag(w)), mb_bt, w_til)
    from_u = each(lambda a, u: mm(a, block_diag(u)), mb_bt, u_til)
    q_hat = each(lambda a, fw: a - fw[:t_len], r_bar, from_w)
    y_hat = each(lambda a, fu: a[t_len:2 * t_len] - fu[:t_len], lmv, from_u)
    g_mat = each(lambda t, fw: eye * jnp.exp(t) - fw[t_len:], log_pt, from_w)
    h_mat = each(lambda a, fu: a[2 * t_len:] - fu[t_len:], lmv, from_u)

    bd_s = [block_diag(st_scr[:, c]) for c in units]
    from_state = each(lambda q, g, s: mm(stack(q, g), s), q_hat, g_mat, bd_s)
    for c, fs, y0, h in zip(units, from_state, y_hat, h_mat):
        y_ref[:, c] = fs[:t_len] + y0
        st_scr[:, c] = fs[t_len:] + h


def _rw_scan(r, kk, v, lw, kd, beta, *, bsz, l_ctx, l_lat, head):
    n, c_rw = r.shape
    t_len = RW_CHUNK
    width = _pick(c_rw, RW_UNITS * RW_PACK * head)
    nc = l_ctx // t_len
    nl = l_lat // t_len
    ctx_blocks = bsz * nc

    def blk(d, b, c):
        cc = jnp.where(d == 1, nc - 1 - c, c)
        cl = jnp.where(d == 1, nl - 1 - (c - nc), c - nc)
        return jnp.where(c < nc, b * nc + cc, ctx_blocks + b * nl + cl)

    shared = pl.BlockSpec((t_len, width), lambda d, b, g, c: (blk(d, b, c), g))
    per_dir = pl.BlockSpec((None, t_len, width), lambda d, b, g, c: (d, blk(d, b, c), g))
    kern = functools.partial(_rw_scan_kernel, head=head)
    return pl.pallas_call(
        kern,
        grid=(2, bsz, c_rw // width, nc + nl),
        in_specs=[shared, shared, shared, per_dir, per_dir, per_dir],
        out_specs=per_dir,
        out_shape=jax.ShapeDtypeStruct((2, n, c_rw), F32),
        scratch_shapes=[pltpu.VMEM((head, width), F32)],
        compiler_params=_params(("arbitrary", "arbitrary", "arbitrary", "arbitrary")),
        name="rwkv_scan",
    )(r, kk, v, lw, kd, beta)


def _rw_readout_kernel(y_ref, g_ref, bonus_ref, lnw_ref, lnb_ref, ones_ref, o_ref, *, head):
    ones = ones_ref[...]
    pack = ones.shape[0]
    inv = 1.0 / head
    for s in range(o_ref.shape[1] // pack):
        cs = slice(s * pack, (s + 1) * pack)
        y = y_ref[0, :, cs] + y_ref[1, :, cs]
        mu = _dot_exact_rhs(y, ones) * inv
        yc = y - mu
        var = _dot_exact_rhs(yc * yc, ones) * inv
        yn = yc * lax.rsqrt(var + GN_EPS) * lnw_ref[:, cs] + lnb_ref[:, cs]
        o_ref[:, cs] = ((yn + bonus_ref[:, cs]) * g_ref[:, cs]).astype(o_ref.dtype)


def _rw_readout(y2, g, bonus, ln_w, ln_b, *, head, tm):
    _, n, c_rw = y2.shape
    pack = RW_PACK * head
    tn = _pick(c_rw, RW_PREP_WIDTH)
    tile = pl.BlockSpec((tm, tn), lambda i, j: (i, j))
    vec = pl.BlockSpec((1, tn), lambda i, j: (0, j))
    kern = functools.partial(_rw_readout_kernel, head=head)
    return pl.pallas_call(
        kern,
        grid=(n // tm, c_rw // tn),
        in_specs=[pl.BlockSpec((2, tm, tn), lambda i, j: (0, i, j)), tile, tile, vec, vec,
                  pl.BlockSpec((pack, pack), lambda i, j: (0, 0))],
        out_specs=tile,
        out_shape=jax.ShapeDtypeStruct((n, c_rw), BF16),
        compiler_params=_params(("arbitrary", "arbitrary")),
        name="rwkv_readout",
    )(y2, g, bonus, ln_w.reshape(1, c_rw), ln_b.reshape(1, c_rw), _head_ones(pack, head))


def _final_norm_kernel(h_ref, gain_ref, o_ref):
    x = h_ref[...]
    ms = jnp.mean(x * x, axis=-1, keepdims=True)
    o_ref[...] = x * lax.rsqrt(ms + NORM_EPS) * gain_ref[...]


def _final_norm(h, gain, *, tm, row0):
    n, d = h.shape
    b0 = row0 // tm
    return pl.pallas_call(
        _final_norm_kernel,
        grid=((n - row0) // tm,),
        in_specs=[pl.BlockSpec((tm, d), lambda i: (i + b0, 0)), pl.BlockSpec((1, d), lambda i: (0, 0))],
        out_specs=pl.BlockSpec((tm, d), lambda i: (i, 0)),
        out_shape=jax.ShapeDtypeStruct((n - row0, d), F32),
        compiler_params=_params(("arbitrary",)),
        name="final_norm",
    )(h, gain.reshape(1, d))


def _to_s5_layout(u, bsz, l_ctx, l_lat, sel):
    t = S5_CHUNK
    n_ctx = bsz * l_ctx
    nj = u.shape[1] // 256

    def tiles(x, length):
        x = x.reshape(bsz, length // t, t, nj, 256)
        return jnp.transpose(x, (3, 1, 0, 2, 4)).reshape(nj, (length // t) * bsz, t * 256)

    return _lane_perm([tiles(u[:n_ctx], l_ctx), tiles(u[n_ctx:], l_lat)], sel)


def _from_s5_layout(y, bsz, l_ctx, l_lat, sel):
    t = S5_CHUNK
    g, rows, _ = y.shape
    m = sel.shape[0]
    nj = g // m
    x = jnp.transpose(y.reshape(nj, m, rows, 256), (0, 2, 1, 3)).reshape(nj, rows, m * 256)
    out = _lane_perm([x], sel).reshape(nj, t, rows, 256)
    rows_ctx = (l_ctx // t) * bsz

    def untile(o, length):
        o = o.reshape(nj, t, length // t, bsz, 256)
        return jnp.transpose(o, (3, 2, 1, 0, 4)).reshape(bsz * length, nj * 256)

    return untile(out[:, :, :rows_ctx], l_ctx), untile(out[:, :, rows_ctx:], l_lat)


def kernel(x, c, ctx, c_ctx, ada_down, ada_up, ada_b, norm1, w_in, s5_a_re, s5_a_im, s5_log_dt, s5_b_re, s5_b_im, s5_c_re, s5_c_im, s5_d, s5_glu_w, s5_glu_b, rw_mu, rw_w0, rw_w_up, rw_a0, rw_a_up, rw_g_up, rw_k_k, rw_k_a, rw_r_k, rw_ln_w, rw_ln_b, w_proj_a, w_proj_b, gate_up, gate_b, w_out, norm2, mlp_w1, mlp_w2, norm_f):
    bsz, l_lat, d = x.shape
    l_ctx = ctx.shape[1]
    depth = w_in.shape[0]
    n_ctx = bsz * l_ctx
    n_lat = bsz * l_lat
    s5_groups, s5_state = s5_a_re.shape[2], s5_a_re.shape[3]
    s5_gch = s5_b_re.shape[-1]
    c_s5 = s5_groups * s5_gch
    rw_heads, head = rw_r_k.shape[1], rw_r_k.shape[2]
    c_rw = rw_heads * head
    lora_w, lora_a, lora_g = rw_w_up.shape[2], rw_a_up.shape[2], rw_g_up.shape[1]
    gate_rank = gate_up.shape[2]
    n_small_rw = 2 * lora_w + 2 * lora_a + lora_g
    c_big = c_s5 + 3 * c_rw
    assert w_in.shape[2] == c_big + n_small_rw + gate_rank
    assert bsz % 8 == 0 and l_ctx % RW_CHUNK == 0 and l_lat % RW_CHUNK == 0 and l_lat % GRID_W == 0
    assert s5_gch * S5_CHUNK == 256 and RW_PACK * head == 256 and c_rw % 256 == 0 and gate_rank % 128 == 0
    assert RW_CHUNK == head
    assert c_s5 % IN_PROJ_TN == 0 and (3 * c_rw) % IN_PROJ_TN == 0 and n_small_rw + gate_rank <= IN_PROJ_TN

    tm = _pick(l_lat, _pick(n_ctx, TM))

    def row_of_block(i):
        ctx_blocks = n_ctx // tm
        return jnp.where(i < ctx_blocks, bsz, (i - ctx_blocks) // (l_lat // tm))

    n_cond = ((bsz + 1 + 7) // 8) * 8
    cond = jnp.zeros((n_cond, d), F32).at[:bsz].set(c).at[bsz].set(c_ctx)
    mods_all = _ada_modulation(cond, ada_down, ada_up, ada_b)
    mods_all = jnp.transpose(mods_all.reshape(depth, n_cond, 6, 1, d), (0, 2, 1, 3, 4))

    h = jnp.concatenate([ctx.reshape(n_ctx, d), x.reshape(n_lat, d)], axis=0)

    lane_sel = _lane_select_matrices(s5_gch)

    for l in range(depth):
        last = l == depth - 1
        row0 = n_ctx if last else 0
        mods = mods_all[l]
        lp = {"rw_mu": rw_mu[l], "rw_w0": rw_w0[l], "rw_w_up": rw_w_up[l], "rw_a0": rw_a0[l],
              "rw_a_up": rw_a_up[l], "rw_g_up": rw_g_up[l], "rw_k_k": rw_k_k[l], "rw_k_a": rw_k_a[l],
              "rw_r_k": rw_r_k[l]}
        w_all = jnp.concatenate([w_in[l, :, :c_big], w_in[l, :, c_big + n_small_rw:],
                                 w_in[l, :, c_big:c_big + n_small_rw],
                                 jnp.zeros((d, IN_PROJ_TN - n_small_rw - gate_rank), F32)], axis=1).astype(BF16)
        z_u, z_rkv, z_small = _in_proj(h, norm1[l], mods, w_all, row_of_block, tm=tm, tn=IN_PROJ_TN,
                                       c_u=c_s5, c_rkv=3 * c_rw)

        kf, kb, bm, cmt, lam = _s5_matrices(s5_a_re[l], s5_a_im[l], s5_log_dt[l], s5_b_re[l], s5_b_im[l],
                                            s5_c_re[l], s5_c_im[l], s5_d[l])
        u_g = _to_s5_layout(z_u, bsz, l_ctx, l_lat, lane_sel)
        y_g = _s5_scan(u_g, kf, kb, bm, cmt, lam, n_ctx=l_ctx // S5_CHUNK, n_chunks=(l_ctx + l_lat) // S5_CHUNK, bsz=bsz)
        y_ctx, y_lat = _from_s5_layout(y_g, bsz, l_ctx, l_lat, lane_sel)
        ya = _s5_glu(y_ctx, y_lat, s5_glu_w[l].astype(BF16), s5_glu_b[l], tm=tm)

        r, kk, v, lw, kd, beta, g, bonus = _rw_prep(z_rkv, z_small, lp, bsz=bsz, l_ctx=l_ctx, l_lat=l_lat,
                                                    c_rw=c_rw, head=head, gate_rank=gate_rank)
        y2 = _rw_scan(r, kk, v, lw, kd, beta, bsz=bsz, l_ctx=l_ctx, l_lat=l_lat, head=head)
        yb = _rw_readout(y2, g, bonus, rw_ln_w[l], rw_ln_b[l], head=head, tm=tm)

        h = _merge_out(ya, yb, z_small, w_proj_a[l].astype(BF16), w_proj_b[l].astype(BF16), gate_up[l].astype(BF16),
                       gate_b[l], w_out[l].astype(BF16), h, mods, row_of_block, tm=tm, row0=row0)
        h = _mlp(h, norm2[l], mods, mlp_w1[l].astype(BF16), mlp_w2[l].astype(BF16), row_of_block, tm=tm, row0=row0)

    out = _final_norm(h, norm_f, tm=tm, row0=n_ctx)
    return out.reshape(bsz, l_lat, d)
```

```python
import functools

import jax
import jax.numpy as jnp
from jax import lax
from jax.experimental import pallas as pl
from jax.experimental.pallas import tpu as pltpu

F32 = jnp.float32
BF16 = jnp.bfloat16

GRID_W = 64
NORM_EPS = 1e-6
GN_EPS = 64e-5
A_RE_MAX = -1e-4
S5_CHUNK = 16
RW_CHUNK = 64
RW_PACK = 4
RW_UNITS = 8
RW_PREP_WIDTH = 1024
TM = 512
IN_PROJ_TN = 1024
VMEM_LIMIT = 56 * 1024 * 1024


def _params(sem):
    return pltpu.CompilerParams(dimension_semantics=sem, vmem_limit_bytes=VMEM_LIMIT)


def _pick(n, pref):
    t = pref
    while n % t:
        t //= 2
    return t


def _dot(a, b):
    return jnp.dot(a, b, preferred_element_type=F32)


def _split(x):
    hi = x.astype(BF16)
    lo = (x - hi.astype(F32)).astype(BF16)
    return hi, lo


def _dot3(a, b):
    ah, al = _split(a)
    bh, bl = _split(b)
    return _dot(ah, bh) + (_dot(ah, bl) + _dot(al, bh))


def _dot_exact_rhs(a, b_bf16):
    ah, al = _split(a)
    return _dot(ah, b_bf16) + _dot(al, b_bf16)


def _sigmoid(x):
    return 1.0 / (1.0 + jnp.exp(-x))


def _gelu_tanh(x):
    c = 0.7978845608028654
    return 0.5 * x * (1.0 + jnp.tanh(c * (x + 0.044715 * (x * x * x))))


def _cast_kernel(w_ref, o_ref):
    o_ref[...] = w_ref[...].astype(o_ref.dtype)


def _to_bf16(w):
    nl, r, c = w.shape
    rows_4mib = max(16, (1 << 20) // c)
    tr = _pick(r, 1 << (rows_4mib.bit_length() - 1))
    out = pl.pallas_call(
        _cast_kernel,
        grid=(nl * r // tr,),
        in_specs=[pl.BlockSpec((tr, c), lambda i: (i, 0))],
        out_specs=pl.BlockSpec((tr, c), lambda i: (i, 0)),
        out_shape=jax.ShapeDtypeStruct((nl * r, c), BF16),
        compiler_params=_params(("arbitrary",)),
        name="weights_to_bf16",
    )(w.reshape(nl * r, c))
    return out.reshape(nl, r, c)


def _ada_kernel(cond_ref, wd_ref, wu_ref, b_ref, o_ref, mid_scr):
    @pl.when(pl.program_id(1) == 0)
    def _():
        c = cond_ref[...]
        mid_scr[...] = _dot3(c * _sigmoid(c), wd_ref[...])

    o_ref[...] = _dot3(mid_scr[...], wu_ref[...]) + b_ref[...]


def _ada_modulation(cond, w_down, w_up, bias):
    nl, d, rank = w_down.shape
    r = cond.shape[0]
    n_out = w_up.shape[-1]
    tn = _pick(n_out, 2048)
    return pl.pallas_call(
        _ada_kernel,
        grid=(nl, n_out // tn),
        in_specs=[
            pl.BlockSpec((r, d), lambda l, j: (0, 0)),
            pl.BlockSpec((None, d, rank), lambda l, j: (l, 0, 0)),
            pl.BlockSpec((None, rank, tn), lambda l, j: (l, 0, j)),
            pl.BlockSpec((None, 1, tn), lambda l, j: (l, 0, j)),
        ],
        out_specs=pl.BlockSpec((None, r, tn), lambda l, j: (l, 0, j)),
        out_shape=jax.ShapeDtypeStruct((nl, r, n_out), F32),
        scratch_shapes=[pltpu.VMEM((r, rank), F32)],
        compiler_params=_params(("arbitrary", "arbitrary")),
        name="ada_modulation",
    )(cond, w_down, w_up, bias.reshape(nl, 1, n_out))


def _in_proj_kernel(h_ref, gain_ref, shift_ref, scale_ref, w_ref, zu_ref, zrkv_ref, zs_ref, n_scr, *, ju, jr):
    j = pl.program_id(1)

    @pl.when(j == 0)
    def _():
        x = h_ref[...]
        ms = jnp.mean(x * x, axis=-1, keepdims=True)
        xn = x * lax.rsqrt(ms + NORM_EPS) * gain_ref[...]
        n_scr[...] = (xn * (1.0 + scale_ref[...]) + shift_ref[...]).astype(BF16)

    acc = _dot(n_scr[...], w_ref[...])

    @pl.when(j < ju)
    def _():
        zu_ref[...] = acc.astype(zu_ref.dtype)

    @pl.when(jnp.logical_and(j >= ju, j < ju + jr))
    def _():
        zrkv_ref[...] = acc

    @pl.when(j == ju + jr)
    def _():
        zs_ref[...] = acc


def _in_proj(h, gain, mods, w_all, row_of_block, *, tm, tn, c_u, c_rkv):
    n, d = h.shape
    ju, jr = c_u // tn, c_rkv // tn
    assert w_all.shape[1] == (ju + jr + 1) * tn
    mod = lambda which: pl.BlockSpec((None, None, 1, d), lambda i, j: (which, row_of_block(i), 0, 0))
    sds = jax.ShapeDtypeStruct
    return pl.pallas_call(
        functools.partial(_in_proj_kernel, ju=ju, jr=jr),
        grid=(n // tm, ju + jr + 1),
        in_specs=[
            pl.BlockSpec((tm, d), lambda i, j: (i, 0), pipeline_mode=pl.Buffered(1)),
            pl.BlockSpec((1, d), lambda i, j: (0, 0)),
            mod(0), mod(1),
            pl.BlockSpec((d, tn), lambda i, j: (0, j)),
        ],
        out_specs=[pl.BlockSpec((tm, tn), lambda i, j: (i, jnp.minimum(j, ju - 1))),
                   pl.BlockSpec((tm, tn), lambda i, j: (i, jnp.clip(j - ju, 0, jr - 1))),
                   pl.BlockSpec((tm, tn), lambda i, j: (i, 0))],
        out_shape=[sds((n, c_u), BF16), sds((n, c_rkv), F32), sds((n, tn), F32)],
        scratch_shapes=[pltpu.VMEM((tm, d), BF16)],
        compiler_params=_params(("arbitrary", "arbitrary")),
        name="in_proj",
    )(h, gain.reshape(1, d), mods, mods, w_all)


def _mlp_kernel(h_ref, gain_ref, shift_ref, scale_ref, gate_ref, w1_ref, w2_ref, o_ref, n_scr):
    j = pl.program_id(1)

    @pl.when(j == 0)
    def _():
        x = h_ref[...]
        ms = jnp.mean(x * x, axis=-1, keepdims=True)
        xn = x * lax.rsqrt(ms + NORM_EPS) * gain_ref[...]
        n_scr[...] = (xn * (1.0 + scale_ref[...]) + shift_ref[...]).astype(BF16)
        o_ref[...] = jnp.zeros_like(o_ref)

    hid = jnp.square(jnp.maximum(_dot(n_scr[...], w1_ref[...]), 0.0)).astype(BF16)
    o_ref[...] += _dot(hid, w2_ref[...])

    @pl.when(j == pl.num_programs(1) - 1)
    def _():
        o_ref[...] = h_ref[...] + gate_ref[...] * o_ref[...]


def _mlp(h, gain, mods, w1, w2, layer, row_of_block, *, tm, row0):
    n, d = h.shape
    d_ff = w1.shape[2]
    tf = _pick(d_ff, 512)
    b0 = row0 // tm
    mod = lambda which: pl.BlockSpec((None, None, 1, d), lambda i, j: (which, row_of_block(i + b0), 0, 0))
    return pl.pallas_call(
        _mlp_kernel,
        grid=((n - row0) // tm, d_ff // tf),
        in_specs=[
            pl.BlockSpec((tm, d), lambda i, j: (i + b0, 0), pipeline_mode=pl.Buffered(1)),
            pl.BlockSpec((1, d), lambda i, j: (0, 0)),
            mod(3), mod(4), mod(5),
            pl.BlockSpec((None, d, tf), lambda i, j: (layer, 0, j)),
            pl.BlockSpec((None, tf, d), lambda i, j: (layer, j, 0)),
        ],
        out_specs=pl.BlockSpec((tm, d), lambda i, j: (i + b0, 0)),
        out_shape=jax.ShapeDtypeStruct((n, d), F32),
        scratch_shapes=[pltpu.VMEM((tm, d), BF16)],
        input_output_aliases={0: 0},
        compiler_params=_params(("arbitrary", "arbitrary")),
        name="mlp",
    )(h, gain.reshape(1, d), mods, mods, mods, w1, w2)


def _merge_out_kernel(ya_ref, yb_ref, gz_ref, wa_ref, wb_ref, ga_ref, gb_ref, ba_ref, bb_ref, wo_ref, h_ref,
                      gate_ref, o_ref):
    j = pl.program_id(1)

    @pl.when(j == 0)
    def _():
        o_ref[...] = jnp.zeros_like(o_ref)

    gz = gz_ref[...].astype(BF16)
    ga = _sigmoid(_dot(gz, ga_ref[...]) + ba_ref[...])
    gb = _sigmoid(_dot(gz, gb_ref[...]) + bb_ref[...])
    mixed = (ga * _dot(ya_ref[...], wa_ref[...]) + gb * _dot(yb_ref[...], wb_ref[...])).astype(BF16)
    o_ref[...] += _dot(mixed, wo_ref[...])

    @pl.when(j == pl.num_programs(1) - 1)
    def _():
        o_ref[...] = h_ref[...] + gate_ref[...] * o_ref[...]


def _merge_out(ya, yb, z_small, wa, wb, gate_up, gate_b, w_out, layer, h, mods, row_of_block, *, tm, row0):
    n, d = h.shape
    ca, cb = ya.shape[1], yb.shape[1]
    d_mid = wa.shape[2]
    rank = gate_up.shape[2]
    tn = _pick(d_mid, 512)
    b0 = row0 // tm
    once = pl.Buffered(1)
    return pl.pallas_call(
        _merge_out_kernel,
        grid=((n - row0) // tm, d_mid // tn),
        in_specs=[
            pl.BlockSpec((tm, ca), lambda i, j: (i + b0, 0), pipeline_mode=once),
            pl.BlockSpec((tm, cb), lambda i, j: (i + b0, 0), pipeline_mode=once),
            pl.BlockSpec((tm, rank), lambda i, j: (i + b0, 0)),
            pl.BlockSpec((None, ca, tn), lambda i, j: (layer, 0, j)),
            pl.BlockSpec((None, cb, tn), lambda i, j: (layer, 0, j)),
            pl.BlockSpec((None, None, rank, tn), lambda i, j: (layer, 0, 0, j)),
            pl.BlockSpec((None, None, rank, tn), lambda i, j: (layer, 1, 0, j)),
            pl.BlockSpec((None, 1, tn), lambda i, j: (0, 0, j)),
            pl.BlockSpec((None, 1, tn), lambda i, j: (1, 0, j)),
            pl.BlockSpec((None, tn, d), lambda i, j: (layer, j, 0)),
            pl.BlockSpec((tm, d), lambda i, j: (i + b0, 0), pipeline_mode=once),
            pl.BlockSpec((None, None, 1, d), lambda i, j: (2, row_of_block(i + b0), 0, 0)),
        ],
        out_specs=pl.BlockSpec((tm, d), lambda i, j: (i + b0, 0)),
        out_shape=jax.ShapeDtypeStruct((n, d), F32),
        input_output_aliases={10: 0},
        compiler_params=_params(("arbitrary", "arbitrary")),
        name="merge_out_proj",
    )(ya, yb, z_small, wa, wb, gate_up, gate_up, gate_b.reshape(2, 1, d_mid), gate_b.reshape(2, 1, d_mid),
      w_out, h, mods)


def _s5_matrices(a_re, a_im, log_dt, b_re, b_im, c_re, c_im, d_skip):
    t_len = S5_CHUNK
    hp = lax.Precision.HIGHEST
    lam_re = jnp.minimum(a_re.astype(F32), A_RE_MAX)
    lam_im = a_im.astype(F32)
    dt = jnp.exp(log_dt.astype(F32))[..., None]
    j = jnp.arange(t_len + 1, dtype=F32)[:, None, None, None]
    mag = jnp.exp(j * (lam_re * dt))
    ang = j * (lam_im * dt)
    pw_re = mag * jnp.cos(ang)
    pw_im = mag * jnp.sin(ang)
    nr, ni = pw_re[1] - 1.0, pw_im[1]
    den = lam_re * lam_re + lam_im * lam_im
    qr = (nr * lam_re + ni * lam_im) / den
    qi = (ni * lam_re - nr * lam_im) / den
    bbr = qr[..., None] * b_re - qi[..., None] * b_im
    bbi = qr[..., None] * b_im + qi[..., None] * b_re
    n_dir, g, p, c = bbr.shape
    clr = c_re[None] * pw_re[:t_len, :, :, None, :] - c_im[None] * pw_im[:t_len, :, :, None, :]
    cli = c_re[None] * pw_im[:t_len, :, :, None, :] + c_im[None] * pw_re[:t_len, :, :, None, :]
    kj = (jnp.einsum("jdgop,dgpi->dgijo", clr, bbr, precision=hp)
          - jnp.einsum("jdgop,dgpi->dgijo", cli, bbi, precision=hp))
    lag0 = (jnp.arange(t_len) == 0).astype(F32)
    skip = d_skip.astype(F32).reshape(g, c)
    kf = kj[0] + skip[:, :, None, None] * lag0[None, None, :, None] * jnp.eye(c, dtype=F32)[None, :, None, :]
    kf = kf.reshape(g, c, t_len * c)
    kb = jnp.flip(kj[1], axis=2).reshape(g, c, t_len * c)
    bbr_t = jnp.transpose(bbr, (0, 1, 3, 2))
    bbi_t = jnp.transpose(bbi, (0, 1, 3, 2))

    def powers(d, e):
        return jnp.transpose(pw_re[e, d], (1, 0, 2)), jnp.transpose(pw_im[e, d], (1, 0, 2))

    def state_in(d, e):
        pr, pi = powers(d, e)
        re = pr[:, :, None, :] * bbr_t[d][:, None] - pi[:, :, None, :] * bbi_t[d][:, None]
        im = pr[:, :, None, :] * bbi_t[d][:, None] + pi[:, :, None, :] * bbr_t[d][:, None]
        return re.reshape(g, t_len * c, p), im.reshape(g, t_len * c, p)

    fr, fi = state_in(0, (t_len - 1) - jnp.arange(t_len))
    br, bi = state_in(1, jnp.arange(t_len))
    bm = jnp.concatenate([fr, br, fi, bi], axis=-1)

    def state_out(d, e):
        pr, pi = powers(d, e)
        re = c_re[d][:, None] * pr[:, :, None, :] - c_im[d][:, None] * pi[:, :, None, :]
        im = c_re[d][:, None] * pi[:, :, None, :] + c_im[d][:, None] * pr[:, :, None, :]
        return re.reshape(g, t_len * c, p), -im.reshape(g, t_len * c, p)

    cfr, cfi = state_out(0, jnp.arange(t_len) + 1)
    cbr, cbi = state_out(1, t_len - jnp.arange(t_len))
    cmt = jnp.concatenate([cfr, cbr, cfi, cbi], axis=-1)
    lam = jnp.stack([jnp.concatenate([pw_re[t_len, 0], pw_re[t_len, 1]], axis=-1),
                     jnp.concatenate([pw_im[t_len, 0], pw_im[t_len, 1]], axis=-1)], axis=1)
    return kf, kb, bm.astype(BF16), cmt.astype(BF16), lam


def _s5_kernel(u_ref, bm_ref, kf_ref, kb_ref, cmt_ref, lam_ref, y_ref, xin_scr, fwd_scr, bwd_scr, km_scr, *,
               n_ctx, n_chunks, bsz, p_state):
    u = u_ref[...]
    xin_scr[...] = _dot(u, bm_ref[...])
    gch, tc = kf_ref.shape
    kf = kf_ref[...]
    kb = kb_ref[...]
    lane_k = lax.broadcasted_iota(jnp.int32, (gch, tc), 1)
    for s in range(tc // gch):
        fwd = kf if s == 0 else jnp.where(lane_k >= gch * s, pltpu.roll(kf, gch * s, axis=1), 0.0)
        shift = (gch * (s + 1)) % tc
        bwd = kb if shift == 0 else jnp.where(lane_k < gch * (s + 1), pltpu.roll(kb, shift, axis=1), 0.0)
        km_scr[s * gch:(s + 1) * gch, :] = (fwd + bwd).astype(BF16)
    two_p = 2 * p_state
    lam_r = jnp.broadcast_to(lam_ref[0:1, :], (bsz, two_p))
    lam_i = jnp.broadcast_to(lam_ref[1:2, :], (bsz, two_p))
    is_fwd = lax.broadcasted_iota(jnp.int32, (bsz, two_p), 1) < p_state

    def body(i, carry):
        re, im = carry
        pb = jnp.where(i < n_ctx, n_ctx - 1 - i, n_chunks + n_ctx - 1 - i)
        rf = pl.multiple_of(i * bsz, bsz)
        rb = pl.multiple_of(pb * bsz, bsz)
        xf = xin_scr[pl.ds(rf, bsz), :]
        xb = xin_scr[pl.ds(rb, bsz), :]
        x_re = jnp.where(is_fwd, xf[:, :two_p], xb[:, :two_p])
        x_im = jnp.where(is_fwd, xf[:, two_p:], xb[:, two_p:])
        state = jnp.concatenate([re, im], axis=1)
        fwd_scr[pl.ds(rf, bsz), :] = state
        bwd_scr[pl.ds(rb, bsz), :] = state
        return (re * lam_r - im * lam_i + x_re, re * lam_i + im * lam_r + x_im)

    zero = jnp.zeros((bsz, two_p), F32)
    lax.fori_loop(0, n_chunks, body, (zero, zero))
    lane = lax.broadcasted_iota(jnp.int32, (1, 2 * two_p), 1)
    fwd_lane = (lane % two_p) < p_state
    xs = jnp.where(fwd_lane, fwd_scr[...], bwd_scr[...]).astype(BF16)
    from_state = lax.dot_general(xs, cmt_ref[...], (((1,), (1,)), ((), ())), preferred_element_type=F32)
    y_ref[...] = (_dot(u, km_scr[...]) + from_state).astype(y_ref.dtype)


def _s5_scan(u_g, kf, kb, bm, cmt, lam, *, n_ctx, n_chunks, bsz):
    g, rows, tc = u_g.shape
    gch = kf.shape[1]
    p4 = bm.shape[-1]
    kern = functools.partial(_s5_kernel, n_ctx=n_ctx, n_chunks=n_chunks, bsz=bsz, p_state=p4 // 4)
    return pl.pallas_call(
        kern,
        grid=(g,),
        in_specs=[
            pl.BlockSpec((None, rows, tc), lambda i: (i, 0, 0)),
            pl.BlockSpec((None, tc, p4), lambda i: (i, 0, 0)),
            pl.BlockSpec((None, gch, tc), lambda i: (i, 0, 0)),
            pl.BlockSpec((None, gch, tc), lambda i: (i, 0, 0)),
            pl.BlockSpec((None, tc, p4), lambda i: (i, 0, 0)),
            pl.BlockSpec((None, 2, p4 // 2), lambda i: (i, 0, 0)),
        ],
        out_specs=pl.BlockSpec((None, rows, tc), lambda i: (i, 0, 0)),
        out_shape=jax.ShapeDtypeStruct((g, rows, tc), BF16),
        scratch_shapes=[pltpu.VMEM((rows, p4), F32), pltpu.VMEM((rows, p4), F32), pltpu.VMEM((rows, p4), F32),
                        pltpu.VMEM((tc, tc), BF16)],
        compiler_params=_params(("arbitrary",)),
        name="s5_scan",
    )(u_g, bm, kf, kb, cmt, lam)


def _glu_kernel(yc_ref, yl_ref, w_ref, b_ref, o_ref, *, ctx_blocks):
    def glu(y):
        z = _gelu_tanh(y.astype(F32))
        o_ref[...] = (z * _sigmoid(_dot(z.astype(BF16), w_ref[...]) + b_ref[...])).astype(o_ref.dtype)

    is_ctx = pl.program_id(0) < ctx_blocks

    @pl.when(is_ctx)
    def _():
        glu(yc_ref[...])

    @pl.when(jnp.logical_not(is_ctx))
    def _():
        glu(yl_ref[...])


def _s5_glu(y_ctx, y_lat, w, layer, b, *, tm):
    c = y_ctx.shape[1]
    cb = y_ctx.shape[0] // tm
    n = y_ctx.shape[0] + y_lat.shape[0]
    return pl.pallas_call(
        functools.partial(_glu_kernel, ctx_blocks=cb),
        grid=(n // tm,),
        in_specs=[
            pl.BlockSpec((tm, c), lambda i: (jnp.minimum(i, cb - 1), 0)),
            pl.BlockSpec((tm, c), lambda i: (jnp.maximum(i - cb, 0), 0)),
            pl.BlockSpec((None, c, c), lambda i: (layer, 0, 0)),
            pl.BlockSpec((1, c), lambda i: (0, 0)),
        ],
        out_specs=pl.BlockSpec((tm, c), lambda i: (i, 0)),
        out_shape=jax.ShapeDtypeStruct((n, c), BF16),
        compiler_params=_params(("arbitrary",)),
        name="s5_glu",
    )(y_ctx, y_lat, w, b.reshape(1, c))


def _lane_select_matrices(gch):
    m = 256 // gch
    a = jnp.arange(m)[:, None, None, None]
    b = jnp.arange(m)[None, :, None, None]
    k = jnp.arange(256)[None, None, :, None]
    n = jnp.arange(256)[None, None, None, :]
    sel = (k == a * gch + n % gch) & (n // gch == b)
    return sel.astype(BF16).reshape(m, m * 256, 256)


def _lane_perm_kernel(*refs):
    *x_refs, sel_ref, o_ref = refs
    off = 0
    for x_ref in x_refs:
        rows = x_ref.shape[0]
        o_ref[off:off + rows, :] = _dot(x_ref[...], sel_ref[...]).astype(o_ref.dtype)
        off += rows


def _lane_perm(xs, sel):
    nj = xs[0].shape[0]
    m, kdim, _ = sel.shape
    rows = sum(x.shape[1] for x in xs)
    return pl.pallas_call(
        _lane_perm_kernel,
        grid=(nj, m),
        in_specs=[pl.BlockSpec((None, x.shape[1], kdim), lambda j, a: (j, 0, 0)) for x in xs]
        + [pl.BlockSpec((None, kdim, 256), lambda j, a: (a, 0, 0))],
        out_specs=pl.BlockSpec((None, rows, 256), lambda j, a: (j * m + a, 0, 0)),
        out_shape=jax.ShapeDtypeStruct((nj * m, rows, 256), BF16),
        compiler_params=_params(("arbitrary", "arbitrary")),
        name="s5_lane_perm",
    )(*xs, sel)


def _head_ones(width, head):
    i = jnp.arange(width) // head
    return (i[:, None] == i[None, :]).astype(BF16)


def _rw_prep_kernel(zr_ref, zrp_ref, zrn_ref, zk_ref, zkp_ref, zkn_ref, zv_ref, zvp_ref, zvn_ref,
                    zs_ref, zsp_ref, zsn_ref, mur_ref, muk_ref, muv_ref, mus_ref,
                    w0_ref, wup_ref, a0_ref, aup_ref, gup_ref, kk_ref, ka_ref, rk_ref, ones_ref,
                    r_o, kk_o, v_o, lw_o, kd_o, beta_o, g_o, bonus_o, *,
                    lora_w, lora_a, small0, n_ctx, l_ctx, l_lat, c_rw, quarter):
    tm, width = zr_ref.shape
    halo = zrp_ref.shape[0]
    g0 = pl.program_id(0) * tm
    is_ctx = g0 < n_ctx
    pos = jnp.where(is_ctx, g0 % l_ctx, (g0 - n_ctx) % l_lat) + lax.broadcasted_iota(jnp.int32, (tm, 1), 0)
    row = lax.broadcasted_iota(jnp.int32, (tm, 1), 0)
    ok_m1 = jnp.where(is_ctx, pos, pos % GRID_W) >= 1
    ok_p1 = jnp.where(is_ctx, l_ctx - 1 - pos, GRID_W - 1 - pos % GRID_W) >= 1
    ok_mw = jnp.where(is_ctx, 0, pos) >= GRID_W
    ok_pw = jnp.where(is_ctx, 0, l_lat - GRID_W - pos) >= 1

    def mix(cur, prev, nxt, mu, col0):
        width = cur.shape[1]
        col = col0 + lax.broadcasted_iota(jnp.int32, (1, width), 1)
        cq = col // quarter
        cls = jnp.where(is_ctx, cq // 2, cq)
        m1 = jnp.where(row == 0, prev[halo - 1:halo, :], pltpu.roll(cur, 1, axis=0))
        p1 = jnp.where(row == tm - 1, nxt[0:1, :], pltpu.roll(cur, tm - 1, axis=0))
        mw = jnp.concatenate([prev, cur[:tm - halo]], axis=0)
        pw = jnp.concatenate([cur[halo:], nxt], axis=0)
        zs = jnp.where(cls == 0, jnp.where(ok_m1, m1, 0.0),
                       jnp.where(cls == 1, jnp.where(ok_p1, p1, 0.0),
                                 jnp.where(cls == 2, jnp.where(ok_mw, mw, 0.0), jnp.where(ok_pw, pw, 0.0))))
        return cur + (zs - cur) * mu

    sm = mix(zs_ref[:, small0:], zsp_ref[:, small0:], zsn_ref[:, small0:], mus_ref[...], 3 * c_rw)
    o_a = 2 * lora_w
    o_g = o_a + 2 * lora_a
    ones = ones_ref[...]
    tn = ones.shape[0]
    gate_in = _sigmoid(sm[:, o_g:]).astype(BF16)
    wd = [jnp.tanh(sm[:, d * lora_w:(d + 1) * lora_w]) for d in range(2)]
    ad = [sm[:, o_a + d * lora_a:o_a + (d + 1) * lora_a] for d in range(2)]
    j0 = pl.program_id(1) * width
    for s in range(width // tn):
        cs = slice(s * tn, (s + 1) * tn)
        r = mix(zr_ref[:, cs], zrp_ref[:, cs], zrn_ref[:, cs], mur_ref[:, cs], j0 + s * tn)
        k = mix(zk_ref[:, cs], zkp_ref[:, cs], zkn_ref[:, cs], muk_ref[:, cs], c_rw + j0 + s * tn)
        v = mix(zv_ref[:, cs], zvp_ref[:, cs], zvn_ref[:, cs], muv_ref[:, cs], 2 * c_rw + j0 + s * tn)
        g_o[:, cs] = _dot(gate_in, gup_ref[:, cs]).astype(g_o.dtype)
        kx = k * kk_ref[:, cs]
        ss = _dot_exact_rhs(kx * kx, ones)
        kk = kx * lax.rsqrt(jnp.maximum(ss, 1e-24))
        k_sum = jnp.zeros_like(k)
        for d in range(2):
            w = w0_ref[d][:, cs] + _dot3(wd[d], wup_ref[d][:, cs])
            w = -(jnp.maximum(-w, 0.0) + jnp.log(1.0 + jnp.exp(-jnp.abs(w)))) - 0.5
            lw_o[d, :, cs] = -jnp.exp(w)
            a = _sigmoid(a0_ref[d][:, cs] + _dot3(ad[d], aup_ref[d][:, cs]))
            kd = k * (1.0 + (a - 1.0) * ka_ref[:, cs])
            kd_o[d, :, cs] = kd.astype(kd_o.dtype)
            beta_o[d, :, cs] = (kk * a).astype(beta_o.dtype)
            k_sum = k_sum + kd
        r_o[:, cs] = r.astype(r_o.dtype)
        kk_o[:, cs] = kk.astype(kk_o.dtype)
        v_o[:, cs] = v.astype(v_o.dtype)
        bonus_o[:, cs] = (_dot_exact_rhs(r * (k_sum * 0.5) * rk_ref[:, cs], ones) * v).astype(bonus_o.dtype)


def _rw_prep(z_rkv, z_small, lp, *, bsz, l_ctx, l_lat, c_rw, head, gate_rank):
    n = z_rkv.shape[0]
    pack = RW_PACK * head
    tn = _pick(c_rw, RW_PREP_WIDTH)
    nj = c_rw // tn
    halo = GRID_W
    tm = _pick(l_lat, _pick(l_ctx, 512))
    assert tm % halo == 0 and tm > halo
    lora_w = lp["rw_w_up"].shape[1]
    lora_a = lp["rw_a_up"].shape[1]
    lora_g = lp["rw_g_up"].shape[0]
    mu = lp["rw_mu"].reshape(1, -1)
    n_small = 2 * lora_w + 2 * lora_a + lora_g
    n_small_all = gate_rank + n_small
    assert mu.shape[1] == 3 * c_rw + n_small and mu.shape[1] % 4 == 0 and n_small_all % 128 == 0
    mu_big = mu[:, :3 * c_rw]
    mu_small = mu[:, 3 * c_rw:]
    row = lambda a: a.reshape(1, c_rw)
    hb = tm // halo
    last_halo = n // halo - 1

    def tiles(off, width_blocks):
        w = tn if width_blocks else n_small_all
        col = (lambda j: j + off) if width_blocks else (lambda j: 0)
        return [pl.BlockSpec((tm, w), lambda i, j: (i, col(j))),
                pl.BlockSpec((halo, w), lambda i, j: (jnp.maximum(i * hb - 1, 0), col(j))),
                pl.BlockSpec((halo, w), lambda i, j: (jnp.minimum((i + 1) * hb, last_halo), col(j)))]

    vec = lambda off: pl.BlockSpec((1, tn), lambda i, j: (0, j + off))
    dirvec = pl.BlockSpec((2, 1, tn), lambda i, j: (0, 0, j))
    out_t = pl.BlockSpec((tm, tn), lambda i, j: (i, j))
    out_d = pl.BlockSpec((2, tm, tn), lambda i, j: (0, i, j))
    sds = jax.ShapeDtypeStruct
    kern = functools.partial(_rw_prep_kernel, lora_w=lora_w, lora_a=lora_a, small0=gate_rank, n_ctx=bsz * l_ctx,
                             l_ctx=l_ctx, l_lat=l_lat, c_rw=c_rw, quarter=mu.shape[1] // 4)
    return pl.pallas_call(
        kern,
        grid=(n // tm, nj),
        in_specs=[
            *tiles(0, True), *tiles(nj, True), *tiles(2 * nj, True), *tiles(0, False),
            vec(0), vec(nj), vec(2 * nj),
            pl.BlockSpec((1, n_small), lambda i, j: (0, 0)),
            dirvec,
            pl.BlockSpec((2, lora_w, tn), lambda i, j: (0, 0, j)),
            dirvec,
            pl.BlockSpec((2, lora_a, tn), lambda i, j: (0, 0, j)),
            pl.BlockSpec((lora_g, tn), lambda i, j: (0, j)),
            vec(0), vec(0), vec(0),
            pl.BlockSpec((pack, pack), lambda i, j: (0, 0)),
        ],
        out_specs=[out_t, out_t, out_t, out_d, out_d, out_d, out_t, out_t],
        out_shape=[sds((n, c_rw), BF16), sds((n, c_rw), BF16), sds((n, c_rw), BF16),
                   sds((2, n, c_rw), F32), sds((2, n, c_rw), BF16), sds((2, n, c_rw), BF16),
                   sds((n, c_rw), BF16), sds((n, c_rw), BF16)],
        compiler_params=_params(("arbitrary", "arbitrary")),
        name="rwkv_prep",
    )(*([z_rkv] * 9), *([z_small] * 3), mu_big, mu_big, mu_big, mu_small,
      lp["rw_w0"].reshape(2, 1, c_rw), lp["rw_w_up"], lp["rw_a0"].reshape(2, 1, c_rw), lp["rw_a_up"],
      lp["rw_g_up"].astype(BF16), row(lp["rw_k_k"]), row(lp["rw_k_a"]), row(lp["rw_r_k"]),
      _head_ones(pack, head))


def _rw_scan_kernel(r_ref, kk_ref, v_ref, lw_ref, kd_ref, beta_ref, y_ref, st_scr, *, head):
    t_len = r_ref.shape[0]
    n_pack = RW_PACK
    width = n_pack * head
    n_units = r_ref.shape[1] // width
    rev = pl.program_id(0) == 1

    @pl.when(pl.program_id(3) == 0)
    def _():
        st_scr[...] = jnp.zeros_like(st_scr)

    lane_head = lax.broadcasted_iota(jnp.int32, (1, width), 1) // head
    head_masks = [lane_head == h for h in range(n_pack)]

    def block_diag(x):
        xb = x.astype(BF16)
        zero = jnp.zeros_like(xb)
        return jnp.concatenate([jnp.where(m, xb, zero) for m in head_masks], axis=0)

    def mm(a, bd):
        return _dot(a.astype(BF16), bd)

    def mm_nt(a, bd):
        return lax.dot_general(a.astype(BF16), bd, (((1,), (1,)), ((), ())), preferred_element_type=F32)

    sgn = jnp.where(rev, -1, 1)
    tt = lax.broadcasted_iota(jnp.int32, (t_len, width), 0)
    ss = lax.broadcasted_iota(jnp.int32, (t_len, width), 1) % head
    order = (tt - ss) * sgn
    strict = order > 0
    incl = order >= 0
    eye = jnp.where(tt == ss, 1.0, 0.0)
    ti = lax.broadcasted_iota(jnp.int32, (t_len, t_len), 0)
    si = lax.broadcasted_iota(jnp.int32, (t_len, t_len), 1)
    tri = jnp.where((ti - si) * sgn >= 0, 1.0, 0.0).astype(BF16)

    units = [slice(u * width, (u + 1) * width) for u in range(n_units)]
    each = lambda f, *xs: [f(*a) for a in zip(*xs)]
    r = [r_ref[:, c] for c in units]
    kk = [kk_ref[:, c] for c in units]
    v = [v_ref[:, c] for c in units]
    lw = [lw_ref[:, c] for c in units]
    kd = [kd_ref[:, c] for c in units]
    beta = [beta_ref[:, c] for c in units]

    def cumulative(x):
        hi, lo = _split(x)
        return _dot(tri, hi) + _dot(tri, lo)

    log_p = each(cumulative, lw)
    log_pt = each(lambda x: jnp.sum(x, axis=0, keepdims=True), lw)
    p_inv = each(lambda x: jnp.exp(-x), log_p)
    a_bar = each(lambda a, p, w: a * jnp.exp(p - w), kk, log_p, lw)
    r_bar = each(lambda a, p: a * jnp.exp(p), r, log_p)
    b_til = each(jnp.multiply, beta, p_inv)
    k_til = each(jnp.multiply, kd, p_inv)
    p_rest = each(lambda t, p: jnp.exp(t - p), log_pt, log_p)
    b_end = each(jnp.multiply, beta, p_rest)
    k_end = each(jnp.multiply, kd, p_rest)

    ar = each(lambda a, b: jnp.concatenate([a, b], axis=0), a_bar, r_bar)
    gram_b = each(lambda a, b: mm_nt(a, block_diag(b)), ar, b_til)
    gram_k = each(lambda a, b: mm_nt(a, block_diag(b)), ar, k_til)
    l_b = each(lambda g: jnp.where(strict, g[:t_len], 0.0), gram_b)
    m_b = each(lambda g: jnp.where(incl, g[t_len:], 0.0), gram_b)
    l_k = each(lambda g: jnp.where(strict, g[:t_len], 0.0), gram_k)
    m_k = each(lambda g: jnp.where(incl, g[t_len:], 0.0), gram_k)

    stack = lambda a, b: jnp.concatenate([a, b], axis=0)
    t_inv = each(lambda x: eye - x, l_b)
    l_pow = each(lambda x: mm(x, block_diag(x)), l_b)
    span = 4
    while span < t_len:
        both = each(lambda t, x: mm(stack(t, x), block_diag(x)), t_inv, l_pow)
        t_inv = each(lambda t, p: t + p[:t_len], t_inv, both)
        l_pow = each(lambda p: p[t_len:], both)
        span *= 2
    t_inv = each(lambda t, x: t + mm(t, block_diag(x)), t_inv, l_pow)

    def head_t(x):
        return jnp.concatenate([x[:, h * head:(h + 1) * head].T for h in range(n_pack)], axis=1)

    bd_v = each(block_diag, v)
    w_til = each(lambda t, a: mm(t, block_diag(a)), t_inv, a_bar)
    lmv = each(lambda lk, mk, ke, bv: mm(jnp.concatenate([lk, mk, head_t(ke)], axis=0), bv), l_k, m_k, k_end, bd_v)
    u_til = each(lambda t, a: mm(t, block_diag(a[:t_len])), t_inv, lmv)
    mb_bt = each(lambda m, b: stack(m, head_t(b)), m_b, b_end)
    from_w = each(lambda a, w: mm(a, block_diag(w)), mb_bt, w_til)
    from_u = each(lambda a, u: mm(a, block_diag(u)), mb_bt, u_til)
    q_hat = each(lambda a, fw: a - fw[:t_len], r_bar, from_w)
    y_hat = each(lambda a, fu: a[t_len:2 * t_len] - fu[:t_len], lmv, from_u)
    g_mat = each(lambda t, fw: eye * jnp.exp(t) - fw[t_len:], log_pt, from_w)
    h_mat = each(lambda a, fu: a[2 * t_len:] - fu[t_len:], lmv, from_u)

    bd_s = [block_diag(st_scr[:, c]) for c in units]
    from_state = each(lambda q, g, s: mm(stack(q, g), s), q_hat, g_mat, bd_s)
    for c, fs, y0, h in zip(units, from_state, y_hat, h_mat):
        y_ref[:, c] = fs[:t_len] + y0
        st_scr[:, c] = fs[t_len:] + h


def _rw_scan(r, kk, v, lw, kd, beta, *, bsz, l_ctx, l_lat, head):
    n, c_rw = r.shape
    t_len = RW_CHUNK
    width = _pick(c_rw, RW_UNITS * RW_PACK * head)
    nc = l_ctx // t_len
    nl = l_lat // t_len
    ctx_blocks = bsz * nc

    def blk(d, b, c):
        cc = jnp.where(d == 1, nc - 1 - c, c)
        cl = jnp.where(d == 1, nl - 1 - (c - nc), c - nc)
        return jnp.where(c < nc, b * nc + cc, ctx_blocks + b * nl + cl)

    shared = pl.BlockSpec((t_len, width), lambda d, b, g, c: (blk(d, b, c), g))
    per_dir = pl.BlockSpec((None, t_len, width), lambda d, b, g, c: (d, blk(d, b, c), g))
    kern = functools.partial(_rw_scan_kernel, head=head)
    return pl.pallas_call(
        kern,
        grid=(2, bsz, c_rw // width, nc + nl),
        in_specs=[shared, shared, shared, per_dir, per_dir, per_dir],
        out_specs=per_dir,
        out_shape=jax.ShapeDtypeStruct((2, n, c_rw), F32),
        scratch_shapes=[pltpu.VMEM((head, width), F32)],
        compiler_params=_params(("arbitrary", "arbitrary", "arbitrary", "arbitrary")),
        name="rwkv_scan",
    )(r, kk, v, lw, kd, beta)


def _rw_readout_kernel(y_ref, g_ref, bonus_ref, lnw_ref, lnb_ref, ones_ref, o_ref, *, head):
    ones = ones_ref[...]
    pack = ones.shape[0]
    inv = 1.0 / head
    for s in range(o_ref.shape[1] // pack):
        cs = slice(s * pack, (s + 1) * pack)
        y = y_ref[0, :, cs] + y_ref[1, :, cs]
        mu = _dot_exact_rhs(y, ones) * inv
        yc = y - mu
        var = _dot_exact_rhs(yc * yc, ones) * inv
        yn = yc * lax.rsqrt(var + GN_EPS) * lnw_ref[:, cs] + lnb_ref[:, cs]
        o_ref[:, cs] = ((yn + bonus_ref[:, cs]) * g_ref[:, cs]).astype(o_ref.dtype)


def _rw_readout(y2, g, bonus, ln_w, ln_b, *, head, tm):
    _, n, c_rw = y2.shape
    pack = RW_PACK * head
    tn = _pick(c_rw, RW_PREP_WIDTH)
    tile = pl.BlockSpec((tm, tn), lambda i, j: (i, j))
    vec = pl.BlockSpec((1, tn), lambda i, j: (0, j))
    kern = functools.partial(_rw_readout_kernel, head=head)
    return pl.pallas_call(
        kern,
        grid=(n // tm, c_rw // tn),
        in_specs=[pl.BlockSpec((2, tm, tn), lambda i, j: (0, i, j)), tile, tile, vec, vec,
                  pl.BlockSpec((pack, pack), lambda i, j: (0, 0))],
        out_specs=tile,
        out_shape=jax.ShapeDtypeStruct((n, c_rw), BF16),
        compiler_params=_params(("arbitrary", "arbitrary")),
        name="rwkv_readout",
    )(y2, g, bonus, ln_w.reshape(1, c_rw), ln_b.reshape(1, c_rw), _head_ones(pack, head))


def _final_norm_kernel(h_ref, gain_ref, o_ref):
    x = h_ref[...]
    ms = jnp.mean(x * x, axis=-1, keepdims=True)
    o_ref[...] = x * lax.rsqrt(ms + NORM_EPS) * gain_ref[...]


def _final_norm(h, gain, *, tm, row0):
    n, d = h.shape
    b0 = row0 // tm
    return pl.pallas_call(
        _final_norm_kernel,
        grid=((n - row0) // tm,),
        in_specs=[pl.BlockSpec((tm, d), lambda i: (i + b0, 0)), pl.BlockSpec((1, d), lambda i: (0, 0))],
        out_specs=pl.BlockSpec((tm, d), lambda i: (i, 0)),
        out_shape=jax.ShapeDtypeStruct((n - row0, d), F32),
        compiler_params=_params(("arbitrary",)),
        name="final_norm",
    )(h, gain.reshape(1, d))


def _to_s5_layout(u, bsz, l_ctx, l_lat, sel):
    t = S5_CHUNK
    n_ctx = bsz * l_ctx
    nj = u.shape[1] // 256

    def tiles(x, length):
        x = x.reshape(bsz, length // t, t, nj, 256)
        return jnp.transpose(x, (3, 1, 0, 2, 4)).reshape(nj, (length // t) * bsz, t * 256)

    return _lane_perm([tiles(u[:n_ctx], l_ctx), tiles(u[n_ctx:], l_lat)], sel)


def _from_s5_layout(y, bsz, l_ctx, l_lat, sel):
    t = S5_CHUNK
    g, rows, _ = y.shape
    m = sel.shape[0]
    nj = g // m
    x = jnp.transpose(y.reshape(nj, m, rows, 256), (0, 2, 1, 3)).reshape(nj, rows, m * 256)
    out = _lane_perm([x], sel).reshape(nj, t, rows, 256)
    rows_ctx = (l_ctx // t) * bsz

    def untile(o, length):
        o = o.reshape(nj, t, length // t, bsz, 256)
        return jnp.transpose(o, (3, 2, 1, 0, 4)).reshape(bsz * length, nj * 256)

    return untile(out[:, :, :rows_ctx], l_ctx), untile(out[:, :, rows_ctx:], l_lat)


def kernel(x, c, ctx, c_ctx, ada_down, ada_up, ada_b, norm1, w_in, s5_a_re, s5_a_im, s5_log_dt, s5_b_re, s5_b_im, s5_c_re, s5_c_im, s5_d, s5_glu_w, s5_glu_b, rw_mu, rw_w0, rw_w_up, rw_a0, rw_a_up, rw_g_up, rw_k_k, rw_k_a, rw_r_k, rw_ln_w, rw_ln_b, w_proj_a, w_proj_b, gate_up, gate_b, w_out, norm2, mlp_w1, mlp_w2, norm_f):
    bsz, l_lat, d = x.shape
    l_ctx = ctx.shape[1]
    depth = w_in.shape[0]
    n_ctx = bsz * l_ctx
    n_lat = bsz * l_lat
    s5_groups, s5_state = s5_a_re.shape[2], s5_a_re.shape[3]
    s5_gch = s5_b_re.shape[-1]
    c_s5 = s5_groups * s5_gch
    rw_heads, head = rw_r_k.shape[1], rw_r_k.shape[2]
    c_rw = rw_heads * head
    lora_w, lora_a, lora_g = rw_w_up.shape[2], rw_a_up.shape[2], rw_g_up.shape[1]
    gate_rank = gate_up.shape[2]
    n_small_rw = 2 * lora_w + 2 * lora_a + lora_g
    c_big = c_s5 + 3 * c_rw
    assert w_in.shape[2] == c_big + n_small_rw + gate_rank
    assert bsz % 8 == 0 and l_ctx % RW_CHUNK == 0 and l_lat % RW_CHUNK == 0 and l_lat % GRID_W == 0
    assert s5_gch * S5_CHUNK == 256 and RW_PACK * head == 256 and c_rw % 256 == 0 and gate_rank % 128 == 0
    assert RW_CHUNK == head
    assert c_s5 % IN_PROJ_TN == 0 and (3 * c_rw) % IN_PROJ_TN == 0 and n_small_rw + gate_rank <= IN_PROJ_TN

    tm = _pick(l_lat, _pick(n_ctx, TM))

    def row_of_block(i):
        ctx_blocks = n_ctx // tm
        return jnp.where(i < ctx_blocks, bsz, (i - ctx_blocks) // (l_lat // tm))

    n_cond = ((bsz + 1 + 7) // 8) * 8
    cond = jnp.zeros((n_cond, d), F32).at[:bsz].set(c).at[bsz].set(c_ctx)
    mods_all = _ada_modulation(cond, ada_down, ada_up, ada_b)
    mods_all = jnp.transpose(mods_all.reshape(depth, n_cond, 6, 1, d), (0, 2, 1, 3, 4))

    h = jnp.concatenate([ctx.reshape(n_ctx, d), x.reshape(n_lat, d)], axis=0)

    lane_sel = _lane_select_matrices(s5_gch)
    w_in_bf = _to_bf16(w_in)
    glu_bf = _to_bf16(s5_glu_w)
    proj_a_bf, proj_b_bf, w_out_bf = _to_bf16(w_proj_a), _to_bf16(w_proj_b), _to_bf16(w_out)
    gate_up_bf = _to_bf16(gate_up.reshape(depth, 2 * gate_rank, d)).reshape(gate_up.shape)
    w1_bf, w2_bf = _to_bf16(mlp_w1), _to_bf16(mlp_w2)

    for l in range(depth):
        last = l == depth - 1
        row0 = n_ctx if last else 0
        mods = mods_all[l]
        lp = {"rw_mu": rw_mu[l], "rw_w0": rw_w0[l], "rw_w_up": rw_w_up[l], "rw_a0": rw_a0[l],
              "rw_a_up": rw_a_up[l], "rw_g_up": rw_g_up[l], "rw_k_k": rw_k_k[l], "rw_k_a": rw_k_a[l],
              "rw_r_k": rw_r_k[l]}
        w_all = jnp.concatenate([w_in_bf[l, :, :c_big], w_in_bf[l, :, c_big + n_small_rw:],
                                 w_in_bf[l, :, c_big:c_big + n_small_rw],
                                 jnp.zeros((d, IN_PROJ_TN - n_small_rw - gate_rank), BF16)], axis=1)
        z_u, z_rkv, z_small = _in_proj(h, norm1[l], mods, w_all, row_of_block, tm=tm, tn=IN_PROJ_TN,
                                       c_u=c_s5, c_rkv=3 * c_rw)

        kf, kb, bm, cmt, lam = _s5_matrices(s5_a_re[l], s5_a_im[l], s5_log_dt[l], s5_b_re[l], s5_b_im[l],
                                            s5_c_re[l], s5_c_im[l], s5_d[l])
        u_g = _to_s5_layout(z_u, bsz, l_ctx, l_lat, lane_sel)
        y_g = _s5_scan(u_g, kf, kb, bm, cmt, lam, n_ctx=l_ctx // S5_CHUNK, n_chunks=(l_ctx + l_lat) // S5_CHUNK, bsz=bsz)
        y_ctx, y_lat = _from_s5_layout(y_g, bsz, l_ctx, l_lat, lane_sel)
        ya = _s5_glu(y_ctx, y_lat, glu_bf, l, s5_glu_b[l], tm=tm)

        r, kk, v, lw, kd, beta, g, bonus = _rw_prep(z_rkv, z_small, lp, bsz=bsz, l_ctx=l_ctx, l_lat=l_lat,
                                                    c_rw=c_rw, head=head, gate_rank=gate_rank)
        y2 = _rw_scan(r, kk, v, lw, kd, beta, bsz=bsz, l_ctx=l_ctx, l_lat=l_lat, head=head)
        yb = _rw_readout(y2, g, bonus, rw_ln_w[l], rw_ln_b[l], head=head, tm=tm)

        h = _merge_out(ya, yb, z_small, proj_a_bf, proj_b_bf, gate_up_bf, gate_b[l], w_out_bf, l, h, mods,
                       row_of_block, tm=tm, row0=row0)
        h = _mlp(h, norm2[l], mods, w1_bf, w2_bf, l, row_of_block, tm=tm, row0=row0)

    out = _final_norm(h, norm_f, tm=tm, row0=n_ctx)
    return out.reshape(bsz, l_lat, d)
```

```python
import functools

import jax
import jax.numpy as jnp
from jax import lax
from jax.experimental import pallas as pl
from jax.experimental.pallas import tpu as pltpu

F32 = jnp.float32
BF16 = jnp.bfloat16

GRID_W = 64
NORM_EPS = 1e-6
GN_EPS = 64e-5
A_RE_MAX = -1e-4
S5_CHUNK = 16
RW_CHUNK = 64
RW_PACK = 4
RW_PREP_WIDTH = 1024
TM = 512
IN_PROJ_TN = 1024
VMEM_LIMIT = 56 * 1024 * 1024


def _params(sem):
    return pltpu.CompilerParams(dimension_semantics=sem, vmem_limit_bytes=VMEM_LIMIT)


def _pick(n, pref):
    t = pref
    while n % t:
        t //= 2
    return t


def _dot(a, b):
    return jnp.dot(a, b, preferred_element_type=F32)


def _split(x):
    hi = x.astype(BF16)
    lo = (x - hi.astype(F32)).astype(BF16)
    return hi, lo


def _dot3(a, b):
    ah, al = _split(a)
    bh, bl = _split(b)
    return _dot(ah, bh) + (_dot(ah, bl) + _dot(al, bh))


def _dot_exact_rhs(a, b_bf16):
    ah, al = _split(a)
    return _dot(ah, b_bf16) + _dot(al, b_bf16)


def _sigmoid(x):
    return 1.0 / (1.0 + jnp.exp(-x))


def _gelu_tanh(x):
    c = 0.7978845608028654
    return 0.5 * x * (1.0 + jnp.tanh(c * (x + 0.044715 * (x * x * x))))


def _cast_kernel(w_ref, o_ref):
    o_ref[...] = w_ref[...].astype(o_ref.dtype)


def _to_bf16(w):
    nl, r, c = w.shape
    rows_4mib = max(16, (1 << 20) // c)
    tr = _pick(r, 1 << (rows_4mib.bit_length() - 1))
    out = pl.pallas_call(
        _cast_kernel,
        grid=(nl * r // tr,),
        in_specs=[pl.BlockSpec((tr, c), lambda i: (i, 0))],
        out_specs=pl.BlockSpec((tr, c), lambda i: (i, 0)),
        out_shape=jax.ShapeDtypeStruct((nl * r, c), BF16),
        compiler_params=_params(("arbitrary",)),
        name="weights_to_bf16",
    )(w.reshape(nl * r, c))
    return out.reshape(nl, r, c)


def _ada_kernel(cond_ref, wd_ref, wu_ref, b_ref, o_ref, mid_scr):
    @pl.when(pl.program_id(1) == 0)
    def _():
        c = cond_ref[...]
        mid_scr[...] = _dot3(c * _sigmoid(c), wd_ref[...])

    o_ref[...] = _dot3(mid_scr[...], wu_ref[...]) + b_ref[...]


def _ada_modulation(cond, w_down, w_up, bias):
    nl, d, rank = w_down.shape
    r = cond.shape[0]
    n_out = w_up.shape[-1]
    tn = _pick(n_out, 2048)
    return pl.pallas_call(
        _ada_kernel,
        grid=(nl, n_out // tn),
        in_specs=[
            pl.BlockSpec((r, d), lambda l, j: (0, 0)),
            pl.BlockSpec((None, d, rank), lambda l, j: (l, 0, 0)),
            pl.BlockSpec((None, rank, tn), lambda l, j: (l, 0, j)),
            pl.BlockSpec((None, 1, tn), lambda l, j: (l, 0, j)),
        ],
        out_specs=pl.BlockSpec((None, r, tn), lambda l, j: (l, 0, j)),
        out_shape=jax.ShapeDtypeStruct((nl, r, n_out), F32),
        scratch_shapes=[pltpu.VMEM((r, rank), F32)],
        compiler_params=_params(("arbitrary", "arbitrary")),
        name="ada_modulation",
    )(cond, w_down, w_up, bias.reshape(nl, 1, n_out))


def _in_proj_kernel(h_ref, gain_ref, shift_ref, scale_ref, w_ref, zu_ref, zrkv_ref, zs_ref, n_scr, *, ju, jr):
    j = pl.program_id(1)

    @pl.when(j == 0)
    def _():
        x = h_ref[...]
        ms = jnp.mean(x * x, axis=-1, keepdims=True)
        xn = x * lax.rsqrt(ms + NORM_EPS) * gain_ref[...]
        n_scr[...] = (xn * (1.0 + scale_ref[...]) + shift_ref[...]).astype(BF16)

    acc = _dot(n_scr[...], w_ref[...])

    @pl.when(j < ju)
    def _():
        zu_ref[...] = acc.astype(zu_ref.dtype)

    @pl.when(jnp.logical_and(j >= ju, j < ju + jr))
    def _():
        zrkv_ref[...] = acc.astype(zrkv_ref.dtype)

    @pl.when(j == ju + jr)
    def _():
        zs_ref[...] = acc


def _in_proj(h, gain, mods, w_all, row_of_block, *, tm, tn, c_u, c_rkv):
    n, d = h.shape
    ju, jr = c_u // tn, c_rkv // tn
    assert w_all.shape[1] == (ju + jr + 1) * tn
    mod = lambda which: pl.BlockSpec((None, None, 1, d), lambda i, j: (which, row_of_block(i), 0, 0))
    sds = jax.ShapeDtypeStruct
    return pl.pallas_call(
        functools.partial(_in_proj_kernel, ju=ju, jr=jr),
        grid=(n // tm, ju + jr + 1),
        in_specs=[
            pl.BlockSpec((tm, d), lambda i, j: (i, 0), pipeline_mode=pl.Buffered(1)),
            pl.BlockSpec((1, d), lambda i, j: (0, 0)),
            mod(0), mod(1),
            pl.BlockSpec((d, tn), lambda i, j: (0, j)),
        ],
        out_specs=[pl.BlockSpec((tm, tn), lambda i, j: (i, jnp.minimum(j, ju - 1))),
                   pl.BlockSpec((tm, tn), lambda i, j: (i, jnp.clip(j - ju, 0, jr - 1))),
                   pl.BlockSpec((tm, tn), lambda i, j: (i, 0))],
        out_shape=[sds((n, c_u), BF16), sds((n, c_rkv), BF16), sds((n, tn), F32)],
        scratch_shapes=[pltpu.VMEM((tm, d), BF16)],
        compiler_params=_params(("arbitrary", "arbitrary")),
        name="in_proj",
    )(h, gain.reshape(1, d), mods, mods, w_all)


def _mlp_kernel(h_ref, gain_ref, shift_ref, scale_ref, gate_ref, w1_ref, w2_ref, o_ref, n_scr):
    j = pl.program_id(1)

    @pl.when(j == 0)
    def _():
        x = h_ref[...]
        ms = jnp.mean(x * x, axis=-1, keepdims=True)
        xn = x * lax.rsqrt(ms + NORM_EPS) * gain_ref[...]
        n_scr[...] = (xn * (1.0 + scale_ref[...]) + shift_ref[...]).astype(BF16)
        o_ref[...] = jnp.zeros_like(o_ref)

    hid = jnp.square(jnp.maximum(_dot(n_scr[...], w1_ref[...]), 0.0)).astype(BF16)
    o_ref[...] += _dot(hid, w2_ref[...])

    @pl.when(j == pl.num_programs(1) - 1)
    def _():
        o_ref[...] = h_ref[...] + gate_ref[...] * o_ref[...]


def _mlp(h, gain, mods, w1, w2, layer, row_of_block, *, tm, row0):
    n, d = h.shape
    d_ff = w1.shape[2]
    tf = _pick(d_ff, 512)
    b0 = row0 // tm
    mod = lambda which: pl.BlockSpec((None, None, 1, d), lambda i, j: (which, row_of_block(i + b0), 0, 0))
    return pl.pallas_call(
        _mlp_kernel,
        grid=((n - row0) // tm, d_ff // tf),
        in_specs=[
            pl.BlockSpec((tm, d), lambda i, j: (i + b0, 0), pipeline_mode=pl.Buffered(1)),
            pl.BlockSpec((1, d), lambda i, j: (0, 0)),
            mod(3), mod(4), mod(5),
            pl.BlockSpec((None, d, tf), lambda i, j: (layer, 0, j)),
            pl.BlockSpec((None, tf, d), lambda i, j: (layer, j, 0)),
        ],
        out_specs=pl.BlockSpec((tm, d), lambda i, j: (i + b0, 0)),
        out_shape=jax.ShapeDtypeStruct((n, d), F32),
        scratch_shapes=[pltpu.VMEM((tm, d), BF16)],
        input_output_aliases={0: 0},
        compiler_params=_params(("arbitrary", "arbitrary")),
        name="mlp",
    )(h, gain.reshape(1, d), mods, mods, mods, w1, w2)


def _merge_out_kernel(ya_ref, yb_ref, gz_ref, wa_ref, wb_ref, ga_ref, gb_ref, ba_ref, bb_ref, wo_ref, h_ref,
                      gate_ref, o_ref):
    j = pl.program_id(1)

    @pl.when(j == 0)
    def _():
        o_ref[...] = jnp.zeros_like(o_ref)

    gz = gz_ref[...].astype(BF16)
    ga = _sigmoid(_dot(gz, ga_ref[...]) + ba_ref[...])
    gb = _sigmoid(_dot(gz, gb_ref[...]) + bb_ref[...])
    mixed = (ga * _dot(ya_ref[...], wa_ref[...]) + gb * _dot(yb_ref[...], wb_ref[...])).astype(BF16)
    o_ref[...] += _dot(mixed, wo_ref[...])

    @pl.when(j == pl.num_programs(1) - 1)
    def _():
        o_ref[...] = h_ref[...] + gate_ref[...] * o_ref[...]


def _merge_out(ya, yb, z_small, wa, wb, gate_up, gate_b, w_out, layer, h, mods, row_of_block, *, tm, row0):
    n, d = h.shape
    ca, cb = ya.shape[1], yb.shape[1]
    d_mid = wa.shape[2]
    rank = gate_up.shape[2]
    tn = _pick(d_mid, 512)
    b0 = row0 // tm
    once = pl.Buffered(1)
    return pl.pallas_call(
        _merge_out_kernel,
        grid=((n - row0) // tm, d_mid // tn),
        in_specs=[
            pl.BlockSpec((tm, ca), lambda i, j: (i + b0, 0), pipeline_mode=once),
            pl.BlockSpec((tm, cb), lambda i, j: (i + b0, 0), pipeline_mode=once),
            pl.BlockSpec((tm, rank), lambda i, j: (i + b0, 0)),
            pl.BlockSpec((None, ca, tn), lambda i, j: (layer, 0, j)),
            pl.BlockSpec((None, cb, tn), lambda i, j: (layer, 0, j)),
            pl.BlockSpec((None, None, rank, tn), lambda i, j: (layer, 0, 0, j)),
            pl.BlockSpec((None, None, rank, tn), lambda i, j: (layer, 1, 0, j)),
            pl.BlockSpec((None, 1, tn), lambda i, j: (0, 0, j)),
            pl.BlockSpec((None, 1, tn), lambda i, j: (1, 0, j)),
            pl.BlockSpec((None, tn, d), lambda i, j: (layer, j, 0)),
            pl.BlockSpec((tm, d), lambda i, j: (i + b0, 0), pipeline_mode=once),
            pl.BlockSpec((None, None, 1, d), lambda i, j: (2, row_of_block(i + b0), 0, 0)),
        ],
        out_specs=pl.BlockSpec((tm, d), lambda i, j: (i + b0, 0)),
        out_shape=jax.ShapeDtypeStruct((n, d), F32),
        input_output_aliases={10: 0},
        compiler_params=_params(("arbitrary", "arbitrary")),
        name="merge_out_proj",
    )(ya, yb, z_small, wa, wb, gate_up, gate_up, gate_b.reshape(2, 1, d_mid), gate_b.reshape(2, 1, d_mid),
      w_out, h, mods)


def _s5_matrices(a_re, a_im, log_dt, b_re, b_im, c_re, c_im, d_skip):
    t_len = S5_CHUNK
    hp = lax.Precision.HIGHEST
    lam_re = jnp.minimum(a_re.astype(F32), A_RE_MAX)
    lam_im = a_im.astype(F32)
    dt = jnp.exp(log_dt.astype(F32))[..., None]
    j = jnp.arange(t_len + 1, dtype=F32)[:, None, None, None]
    mag = jnp.exp(j * (lam_re * dt))
    ang = j * (lam_im * dt)
    pw_re = mag * jnp.cos(ang)
    pw_im = mag * jnp.sin(ang)
    nr, ni = pw_re[1] - 1.0, pw_im[1]
    den = lam_re * lam_re + lam_im * lam_im
    qr = (nr * lam_re + ni * lam_im) / den
    qi = (ni * lam_re - nr * lam_im) / den
    bbr = qr[..., None] * b_re - qi[..., None] * b_im
    bbi = qr[..., None] * b_im + qi[..., None] * b_re
    n_dir, g, p, c = bbr.shape
    clr = c_re[None] * pw_re[:t_len, :, :, None, :] - c_im[None] * pw_im[:t_len, :, :, None, :]
    cli = c_re[None] * pw_im[:t_len, :, :, None, :] + c_im[None] * pw_re[:t_len, :, :, None, :]
    kj = (jnp.einsum("jdgop,dgpi->dgijo", clr, bbr, precision=hp)
          - jnp.einsum("jdgop,dgpi->dgijo", cli, bbi, precision=hp))
    lag0 = (jnp.arange(t_len) == 0).astype(F32)
    skip = d_skip.astype(F32).reshape(g, c)
    kf = kj[0] + skip[:, :, None, None] * lag0[None, None, :, None] * jnp.eye(c, dtype=F32)[None, :, None, :]
    kf = kf.reshape(g, c, t_len * c)
    kb = jnp.flip(kj[1], axis=2).reshape(g, c, t_len * c)
    bbr_t = jnp.transpose(bbr, (0, 1, 3, 2))
    bbi_t = jnp.transpose(bbi, (0, 1, 3, 2))

    def powers(d, e):
        return jnp.transpose(pw_re[e, d], (1, 0, 2)), jnp.transpose(pw_im[e, d], (1, 0, 2))

    def state_in(d, e):
        pr, pi = powers(d, e)
        re = pr[:, :, None, :] * bbr_t[d][:, None] - pi[:, :, None, :] * bbi_t[d][:, None]
        im = pr[:, :, None, :] * bbi_t[d][:, None] + pi[:, :, None, :] * bbr_t[d][:, None]
        return re.reshape(g, t_len * c, p), im.reshape(g, t_len * c, p)

    fr, fi = state_in(0, (t_len - 1) - jnp.arange(t_len))
    br, bi = state_in(1, jnp.arange(t_len))
    bm = jnp.concatenate([fr, br, fi, bi], axis=-1)

    def state_out(d, e):
        pr, pi = powers(d, e)
        re = c_re[d][:, None] * pr[:, :, None, :] - c_im[d][:, None] * pi[:, :, None, :]
        im = c_re[d][:, None] * pi[:, :, None, :] + c_im[d][:, None] * pr[:, :, None, :]
        return re.reshape(g, t_len * c, p), -im.reshape(g, t_len * c, p)

    cfr, cfi = state_out(0, jnp.arange(t_len) + 1)
    cbr, cbi = state_out(1, t_len - jnp.arange(t_len))
    cmt = jnp.concatenate([cfr, cbr, cfi, cbi], axis=-1)
    lam = jnp.stack([jnp.concatenate([pw_re[t_len, 0], pw_re[t_len, 1]], axis=-1),
                     jnp.concatenate([pw_im[t_len, 0], pw_im[t_len, 1]], axis=-1)], axis=1)
    return kf, kb, bm.astype(BF16), cmt.astype(BF16), lam


def _s5_kernel(u_ref, bm_ref, kf_ref, kb_ref, cmt_ref, lam_ref, y_ref, xin_scr, fwd_scr, bwd_scr, km_scr, *,
               n_ctx, n_chunks, bsz, p_state):
    u = u_ref[...]
    xin_scr[...] = _dot(u, bm_ref[...])
    gch, tc = kf_ref.shape
    kf = kf_ref[...]
    kb = kb_ref[...]
    lane_k = lax.broadcasted_iota(jnp.int32, (gch, tc), 1)
    for s in range(tc // gch):
        fwd = kf if s == 0 else jnp.where(lane_k >= gch * s, pltpu.roll(kf, gch * s, axis=1), 0.0)
        shift = (gch * (s + 1)) % tc
        bwd = kb if shift == 0 else jnp.where(lane_k < gch * (s + 1), pltpu.roll(kb, shift, axis=1), 0.0)
        km_scr[s * gch:(s + 1) * gch, :] = (fwd + bwd).astype(BF16)
    two_p = 2 * p_state
    lam_r = jnp.broadcast_to(lam_ref[0:1, :], (bsz, two_p))
    lam_i = jnp.broadcast_to(lam_ref[1:2, :], (bsz, two_p))
    is_fwd = lax.broadcasted_iota(jnp.int32, (bsz, two_p), 1) < p_state

    def body(i, carry):
        re, im = carry
        pb = jnp.where(i < n_ctx, n_ctx - 1 - i, n_chunks + n_ctx - 1 - i)
        rf = pl.multiple_of(i * bsz, bsz)
        rb = pl.multiple_of(pb * bsz, bsz)
        xf = xin_scr[pl.ds(rf, bsz), :]
        xb = xin_scr[pl.ds(rb, bsz), :]
        x_re = jnp.where(is_fwd, xf[:, :two_p], xb[:, :two_p])
        x_im = jnp.where(is_fwd, xf[:, two_p:], xb[:, two_p:])
        state = jnp.concatenate([re, im], axis=1)
        fwd_scr[pl.ds(rf, bsz), :] = state
        bwd_scr[pl.ds(rb, bsz), :] = state
        return (re * lam_r - im * lam_i + x_re, re * lam_i + im * lam_r + x_im)

    zero = jnp.zeros((bsz, two_p), F32)
    lax.fori_loop(0, n_chunks, body, (zero, zero))
    lane = lax.broadcasted_iota(jnp.int32, (1, 2 * two_p), 1)
    fwd_lane = (lane % two_p) < p_state
    xs = jnp.where(fwd_lane, fwd_scr[...], bwd_scr[...]).astype(BF16)
    from_state = lax.dot_general(xs, cmt_ref[...], (((1,), (1,)), ((), ())), preferred_element_type=F32)
    y_ref[...] = (_dot(u, km_scr[...]) + from_state).astype(y_ref.dtype)


def _s5_scan(u_g, kf, kb, bm, cmt, lam, *, n_ctx, n_chunks, bsz):
    g, rows, tc = u_g.shape
    gch = kf.shape[1]
    p4 = bm.shape[-1]
    kern = functools.partial(_s5_kernel, n_ctx=n_ctx, n_chunks=n_chunks, bsz=bsz, p_state=p4 // 4)
    return pl.pallas_call(
        kern,
        grid=(g,),
        in_specs=[
            pl.BlockSpec((None, rows, tc), lambda i: (i, 0, 0)),
            pl.BlockSpec((None, tc, p4), lambda i: (i, 0, 0)),
            pl.BlockSpec((None, gch, tc), lambda i: (i, 0, 0)),
            pl.BlockSpec((None, gch, tc), lambda i: (i, 0, 0)),
            pl.BlockSpec((None, tc, p4), lambda i: (i, 0, 0)),
            pl.BlockSpec((None, 2, p4 // 2), lambda i: (i, 0, 0)),
        ],
        out_specs=pl.BlockSpec((None, rows, tc), lambda i: (i, 0, 0)),
        out_shape=jax.ShapeDtypeStruct((g, rows, tc), BF16),
        scratch_shapes=[pltpu.VMEM((rows, p4), F32), pltpu.VMEM((rows, p4), F32), pltpu.VMEM((rows, p4), F32),
                        pltpu.VMEM((tc, tc), BF16)],
        compiler_params=_params(("arbitrary",)),
        name="s5_scan",
    )(u_g, bm, kf, kb, cmt, lam)


def _glu_kernel(yc_ref, yl_ref, w_ref, b_ref, o_ref, *, ctx_blocks):
    def glu(y):
        z = _gelu_tanh(y.astype(F32))
        o_ref[...] = (z * _sigmoid(_dot(z.astype(BF16), w_ref[...]) + b_ref[...])).astype(o_ref.dtype)

    is_ctx = pl.program_id(0) < ctx_blocks

    @pl.when(is_ctx)
    def _():
        glu(yc_ref[...])

    @pl.when(jnp.logical_not(is_ctx))
    def _():
        glu(yl_ref[...])


def _s5_glu(y_ctx, y_lat, w, layer, b, *, tm):
    c = y_ctx.shape[1]
    cb = y_ctx.shape[0] // tm
    n = y_ctx.shape[0] + y_lat.shape[0]
    return pl.pallas_call(
        functools.partial(_glu_kernel, ctx_blocks=cb),
        grid=(n // tm,),
        in_specs=[
            pl.BlockSpec((tm, c), lambda i: (jnp.minimum(i, cb - 1), 0)),
            pl.BlockSpec((tm, c), lambda i: (jnp.maximum(i - cb, 0), 0)),
            pl.BlockSpec((None, c, c), lambda i: (layer, 0, 0)),
            pl.BlockSpec((1, c), lambda i: (0, 0)),
        ],
        out_specs=pl.BlockSpec((tm, c), lambda i: (i, 0)),
        out_shape=jax.ShapeDtypeStruct((n, c), BF16),
        compiler_params=_params(("arbitrary",)),
        name="s5_glu",
    )(y_ctx, y_lat, w, b.reshape(1, c))


def _lane_select_matrices(gch):
    m = 256 // gch
    a = jnp.arange(m)[:, None, None, None]
    b = jnp.arange(m)[None, :, None, None]
    k = jnp.arange(256)[None, None, :, None]
    n = jnp.arange(256)[None, None, None, :]
    sel = (k == a * gch + n % gch) & (n // gch == b)
    return sel.astype(BF16).reshape(m, m * 256, 256)


def _lane_perm_kernel(*refs):
    *x_refs, sel_ref, o_ref = refs
    off = 0
    for x_ref in x_refs:
        rows = x_ref.shape[0]
        o_ref[off:off + rows, :] = _dot(x_ref[...], sel_ref[...]).astype(o_ref.dtype)
        off += rows


def _lane_perm(xs, sel):
    nj = xs[0].shape[0]
    m, kdim, _ = sel.shape
    rows = sum(x.shape[1] for x in xs)
    return pl.pallas_call(
        _lane_perm_kernel,
        grid=(nj, m),
        in_specs=[pl.BlockSpec((None, x.shape[1], kdim), lambda j, a: (j, 0, 0)) for x in xs]
        + [pl.BlockSpec((None, kdim, 256), lambda j, a: (a, 0, 0))],
        out_specs=pl.BlockSpec((None, rows, 256), lambda j, a: (j * m + a, 0, 0)),
        out_shape=jax.ShapeDtypeStruct((nj * m, rows, 256), BF16),
        compiler_params=_params(("arbitrary", "arbitrary")),
        name="s5_lane_perm",
    )(*xs, sel)


def _head_ones(width, head):
    i = jnp.arange(width) // head
    return (i[:, None] == i[None, :]).astype(BF16)


def _rw_prep_kernel(zr_ref, zrp_ref, zrn_ref, zk_ref, zkp_ref, zkn_ref, zv_ref, zvp_ref, zvn_ref,
                    zs_ref, zsp_ref, zsn_ref, mur_ref, muk_ref, muv_ref, mus_ref,
                    w0_ref, wup_ref, a0_ref, aup_ref, gup_ref, kk_ref, ka_ref, rk_ref, ones_ref,
                    r_o, kk_o, v_o, lw_o, kd_o, beta_o, g_o, bonus_o, *,
                    lora_w, lora_a, small0, n_ctx, l_ctx, l_lat, c_rw, quarter):
    tm, width = zr_ref.shape
    halo = zrp_ref.shape[0]
    g0 = pl.program_id(0) * tm
    is_ctx = g0 < n_ctx
    pos = jnp.where(is_ctx, g0 % l_ctx, (g0 - n_ctx) % l_lat) + lax.broadcasted_iota(jnp.int32, (tm, 1), 0)
    row = lax.broadcasted_iota(jnp.int32, (tm, 1), 0)
    ok_m1 = jnp.where(is_ctx, pos, pos % GRID_W) >= 1
    ok_p1 = jnp.where(is_ctx, l_ctx - 1 - pos, GRID_W - 1 - pos % GRID_W) >= 1
    ok_mw = jnp.where(is_ctx, 0, pos) >= GRID_W
    ok_pw = jnp.where(is_ctx, 0, l_lat - GRID_W - pos) >= 1

    def mix(cur, prev, nxt, mu, col0):
        cur, prev, nxt = cur.astype(F32), prev.astype(F32), nxt.astype(F32)
        width = cur.shape[1]
        col = col0 + lax.broadcasted_iota(jnp.int32, (1, width), 1)
        cq = col // quarter
        cls = jnp.where(is_ctx, cq // 2, cq)
        m1 = jnp.where(row == 0, prev[halo - 1:halo, :], pltpu.roll(cur, 1, axis=0))
        p1 = jnp.where(row == tm - 1, nxt[0:1, :], pltpu.roll(cur, tm - 1, axis=0))
        mw = jnp.concatenate([prev, cur[:tm - halo]], axis=0)
        pw = jnp.concatenate([cur[halo:], nxt], axis=0)
        zs = jnp.where(cls == 0, jnp.where(ok_m1, m1, 0.0),
                       jnp.where(cls == 1, jnp.where(ok_p1, p1, 0.0),
                                 jnp.where(cls == 2, jnp.where(ok_mw, mw, 0.0), jnp.where(ok_pw, pw, 0.0))))
        return cur + (zs - cur) * mu

    sm = mix(zs_ref[:, small0:], zsp_ref[:, small0:], zsn_ref[:, small0:], mus_ref[...], 3 * c_rw)
    o_a = 2 * lora_w
    o_g = o_a + 2 * lora_a
    ones = ones_ref[...]
    tn = ones.shape[0]
    gate_in = _sigmoid(sm[:, o_g:]).astype(BF16)
    wd = [jnp.tanh(sm[:, d * lora_w:(d + 1) * lora_w]) for d in range(2)]
    ad = [sm[:, o_a + d * lora_a:o_a + (d + 1) * lora_a] for d in range(2)]
    j0 = pl.program_id(1) * width
    for s in range(width // tn):
        cs = slice(s * tn, (s + 1) * tn)
        r = mix(zr_ref[:, cs], zrp_ref[:, cs], zrn_ref[:, cs], mur_ref[:, cs], j0 + s * tn)
        k = mix(zk_ref[:, cs], zkp_ref[:, cs], zkn_ref[:, cs], muk_ref[:, cs], c_rw + j0 + s * tn)
        v = mix(zv_ref[:, cs], zvp_ref[:, cs], zvn_ref[:, cs], muv_ref[:, cs], 2 * c_rw + j0 + s * tn)
        g_o[:, cs] = _dot(gate_in, gup_ref[:, cs]).astype(g_o.dtype)
        kx = k * kk_ref[:, cs]
        ss = _dot_exact_rhs(kx * kx, ones)
        kk = kx * lax.rsqrt(jnp.maximum(ss, 1e-24))
        k_sum = jnp.zeros_like(k)
        for d in range(2):
            w = w0_ref[d][:, cs] + _dot3(wd[d], wup_ref[d][:, cs])
            w = -(jnp.maximum(-w, 0.0) + jnp.log(1.0 + jnp.exp(-jnp.abs(w)))) - 0.5
            lw_o[d, :, cs] = -jnp.exp(w)
            a = _sigmoid(a0_ref[d][:, cs] + _dot3(ad[d], aup_ref[d][:, cs]))
            kd = k * (1.0 + (a - 1.0) * ka_ref[:, cs])
            kd_o[d, :, cs] = kd.astype(kd_o.dtype)
            beta_o[d, :, cs] = (kk * a).astype(beta_o.dtype)
            k_sum = k_sum + kd
        r_o[:, cs] = r.astype(r_o.dtype)
        kk_o[:, cs] = kk.astype(kk_o.dtype)
        v_o[:, cs] = v.astype(v_o.dtype)
        bonus_o[:, cs] = (_dot_exact_rhs(r * (k_sum * 0.5) * rk_ref[:, cs], ones) * v).astype(bonus_o.dtype)


def _rw_prep(z_rkv, z_small, lp, *, bsz, l_ctx, l_lat, c_rw, head, gate_rank):
    n = z_rkv.shape[0]
    pack = RW_PACK * head
    tn = _pick(c_rw, RW_PREP_WIDTH)
    nj = c_rw // tn
    halo = GRID_W
    tm = _pick(l_lat, _pick(l_ctx, 512))
    assert tm % halo == 0 and tm > halo
    lora_w = lp["rw_w_up"].shape[1]
    lora_a = lp["rw_a_up"].shape[1]
    lora_g = lp["rw_g_up"].shape[0]
    mu = lp["rw_mu"].reshape(1, -1)
    n_small = 2 * lora_w + 2 * lora_a + lora_g
    n_small_all = gate_rank + n_small
    assert mu.shape[1] == 3 * c_rw + n_small and mu.shape[1] % 4 == 0 and n_small_all % 128 == 0
    mu_big = mu[:, :3 * c_rw]
    mu_small = mu[:, 3 * c_rw:]
    row = lambda a: a.reshape(1, c_rw)
    hb = tm // halo
    last_halo = n // halo - 1

    def tiles(off, width_blocks):
        w = tn if width_blocks else n_small_all
        col = (lambda j: j + off) if width_blocks else (lambda j: 0)
        return [pl.BlockSpec((tm, w), lambda i, j: (i, col(j))),
                pl.BlockSpec((halo, w), lambda i, j: (jnp.maximum(i * hb - 1, 0), col(j))),
                pl.BlockSpec((halo, w), lambda i, j: (jnp.minimum((i + 1) * hb, last_halo), col(j)))]

    vec = lambda off: pl.BlockSpec((1, tn), lambda i, j: (0, j + off))
    dirvec = pl.BlockSpec((2, 1, tn), lambda i, j: (0, 0, j))
    out_t = pl.BlockSpec((tm, tn), lambda i, j: (i, j))
    out_d = pl.BlockSpec((2, tm, tn), lambda i, j: (0, i, j))
    sds = jax.ShapeDtypeStruct
    kern = functools.partial(_rw_prep_kernel, lora_w=lora_w, lora_a=lora_a, small0=gate_rank, n_ctx=bsz * l_ctx,
                             l_ctx=l_ctx, l_lat=l_lat, c_rw=c_rw, quarter=mu.shape[1] // 4)
    return pl.pallas_call(
        kern,
        grid=(n // tm, nj),
        in_specs=[
            *tiles(0, True), *tiles(nj, True), *tiles(2 * nj, True), *tiles(0, False),
            vec(0), vec(nj), vec(2 * nj),
            pl.BlockSpec((1, n_small), lambda i, j: (0, 0)),
            dirvec,
            pl.BlockSpec((2, lora_w, tn), lambda i, j: (0, 0, j)),
            dirvec,
            pl.BlockSpec((2, lora_a, tn), lambda i, j: (0, 0, j)),
            pl.BlockSpec((lora_g, tn), lambda i, j: (0, j)),
            vec(0), vec(0), vec(0),
            pl.BlockSpec((pack, pack), lambda i, j: (0, 0)),
        ],
        out_specs=[out_t, out_t, out_t, out_d, out_d, out_d, out_t, out_t],
        out_shape=[sds((n, c_rw), BF16), sds((n, c_rw), BF16), sds((n, c_rw), BF16),
                   sds((2, n, c_rw), F32), sds((2, n, c_rw), BF16), sds((2, n, c_rw), BF16),
                   sds((n, c_rw), BF16), sds((n, c_rw), BF16)],
        compiler_params=_params(("arbitrary", "arbitrary")),
        name="rwkv_prep",
    )(*([z_rkv] * 9), *([z_small] * 3), mu_big, mu_big, mu_big, mu_small,
      lp["rw_w0"].reshape(2, 1, c_rw), lp["rw_w_up"], lp["rw_a0"].reshape(2, 1, c_rw), lp["rw_a_up"],
      lp["rw_g_up"].astype(BF16), row(lp["rw_k_k"]), row(lp["rw_k_a"]), row(lp["rw_r_k"]),
      _head_ones(pack, head))


def _rw_scan_kernel(rf_ref, kkf_ref, vf_ref, lwf_ref, kdf_ref, betaf_ref, rb_ref, kkb_ref, vb_ref, lwb_ref, kdb_ref,
                    betab_ref, yf_ref, yb_ref, st_scr, *, head):
    t_len = rf_ref.shape[0]
    n_pack = RW_PACK
    width = n_pack * head
    n_groups = rf_ref.shape[1] // width

    @pl.when(pl.program_id(1) == 0)
    def _():
        st_scr[...] = jnp.zeros_like(st_scr)

    lane_head = lax.broadcasted_iota(jnp.int32, (1, width), 1) // head
    head_masks = [lane_head == h for h in range(n_pack)]

    def block_diag(x):
        xb = x.astype(BF16)
        zero = jnp.zeros_like(xb)
        return jnp.concatenate([jnp.where(m, xb, zero) for m in head_masks], axis=0)

    def mm(a, bd):
        return _dot(a.astype(BF16), bd)

    def mm_nt(a, bd):
        return lax.dot_general(a.astype(BF16), bd, (((1,), (1,)), ((), ())), preferred_element_type=F32)

    tt = lax.broadcasted_iota(jnp.int32, (t_len, width), 0)
    ss = lax.broadcasted_iota(jnp.int32, (t_len, width), 1) % head
    eye = jnp.where(tt == ss, 1.0, 0.0)
    ti = lax.broadcasted_iota(jnp.int32, (t_len, t_len), 0)
    si = lax.broadcasted_iota(jnp.int32, (t_len, t_len), 1)
    strict_d = [tt > ss, tt < ss]
    incl_d = [tt >= ss, tt <= ss]
    tri_d = [jnp.where(ti >= si, 1.0, 0.0).astype(BF16), jnp.where(ti <= si, 1.0, 0.0).astype(BF16)]

    units = [(d, slice(g * width, (g + 1) * width)) for d in range(2) for g in range(n_groups)]
    each = lambda f, *xs: [f(*a) for a in zip(*xs)]
    pick = lambda fwd_ref, bwd_ref: [(bwd_ref if d else fwd_ref)[:, c] for d, c in units]
    r = pick(rf_ref, rb_ref)
    kk = pick(kkf_ref, kkb_ref)
    v = pick(vf_ref, vb_ref)
    lw = pick(lwf_ref, lwb_ref)
    kd = pick(kdf_ref, kdb_ref)
    beta = pick(betaf_ref, betab_ref)
    strict = [strict_d[d] for d, _ in units]
    incl = [incl_d[d] for d, _ in units]
    tri = [tri_d[d] for d, _ in units]

    def cumulative(x, tri_u):
        hi, lo = _split(x)
        return _dot(tri_u, hi) + _dot(tri_u, lo)

    log_p = each(cumulative, lw, tri)
    log_pt = each(lambda x: jnp.sum(x, axis=0, keepdims=True), lw)
    p_inv = each(lambda x: jnp.exp(-x), log_p)
    a_bar = each(lambda a, p, w: a * jnp.exp(p - w), kk, log_p, lw)
    r_bar = each(lambda a, p: a * jnp.exp(p), r, log_p)
    b_til = each(jnp.multiply, beta, p_inv)
    k_til = each(jnp.multiply, kd, p_inv)
    p_rest = each(lambda t, p: jnp.exp(t - p), log_pt, log_p)
    b_end = each(jnp.multiply, beta, p_rest)
    k_end = each(jnp.multiply, kd, p_rest)

    ar = each(lambda a, b: jnp.concatenate([a, b], axis=0), a_bar, r_bar)
    gram_b = each(lambda a, b: mm_nt(a, block_diag(b)), ar, b_til)
    gram_k = each(lambda a, b: mm_nt(a, block_diag(b)), ar, k_til)
    l_b = each(lambda g, m: jnp.where(m, g[:t_len], 0.0), gram_b, strict)
    m_b = each(lambda g, m: jnp.where(m, g[t_len:], 0.0), gram_b, incl)
    l_k = each(lambda g, m: jnp.where(m, g[:t_len], 0.0), gram_k, strict)
    m_k = each(lambda g, m: jnp.where(m, g[t_len:], 0.0), gram_k, incl)

    stack = lambda a, b: jnp.concatenate([a, b], axis=0)
    t_inv = each(lambda x: eye - x, l_b)
    l_pow = each(lambda x: mm(x, block_diag(x)), l_b)
    span = 4
    while span < t_len:
        both = each(lambda t, x: mm(stack(t, x), block_diag(x)), t_inv, l_pow)
        t_inv = each(lambda t, p: t + p[:t_len], t_inv, both)
        l_pow = each(lambda p: p[t_len:], both)
        span *= 2
    t_inv = each(lambda t, x: t + mm(t, block_diag(x)), t_inv, l_pow)

    def head_t(x):
        return jnp.concatenate([x[:, h * head:(h + 1) * head].T for h in range(n_pack)], axis=1)

    bd_v = each(block_diag, v)
    w_til = each(lambda t, a: mm(t, block_diag(a)), t_inv, a_bar)
    lmv = each(lambda lk, mk, ke, bv: mm(jnp.concatenate([lk, mk, head_t(ke)], axis=0), bv), l_k, m_k, k_end, bd_v)
    u_til = each(lambda t, a: mm(t, block_diag(a[:t_len])), t_inv, lmv)
    mb_bt = each(lambda m, b: stack(m, head_t(b)), m_b, b_end)
    from_w = each(lambda a, w: mm(a, block_diag(w)), mb_bt, w_til)
    from_u = each(lambda a, u: mm(a, block_diag(u)), mb_bt, u_til)
    q_hat = each(lambda a, fw: a - fw[:t_len], r_bar, from_w)
    y_hat = each(lambda a, fu: a[t_len:2 * t_len] - fu[:t_len], lmv, from_u)
    g_mat = each(lambda t, fw: eye * jnp.exp(t) - fw[t_len:], log_pt, from_w)
    h_mat = each(lambda a, fu: a[2 * t_len:] - fu[t_len:], lmv, from_u)

    bd_s = [block_diag(st_scr[d, :, c]) for d, c in units]
    from_state = each(lambda q, g, s: mm(stack(q, g), s), q_hat, g_mat, bd_s)
    for (d, c), fs, y0, h in zip(units, from_state, y_hat, h_mat):
        (yb_ref if d else yf_ref)[:, c] = (fs[:t_len] + y0).astype(yf_ref.dtype)
        st_scr[d, :, c] = fs[t_len:] + h


def _rw_scan(r, kk, v, lw, kd, beta, *, bsz, l_ctx, l_lat, head):
    n, c_rw = r.shape
    t_len = RW_CHUNK
    nc = l_ctx // t_len
    nl = l_lat // t_len
    ctx_blocks = bsz * nc

    def blk(d, b, c):
        cc = nc - 1 - c if d else c
        cl = nl - 1 - (c - nc) if d else c - nc
        return jnp.where(c < nc, b * nc + cc, ctx_blocks + b * nl + cl)

    shared = lambda d: pl.BlockSpec((t_len, c_rw), lambda b, c: (blk(d, b, c), 0))
    per_dir = lambda d: pl.BlockSpec((None, t_len, c_rw), lambda b, c: (d, blk(d, b, c), 0))
    side = lambda d: [shared(d), shared(d), shared(d), per_dir(d), per_dir(d), per_dir(d)]
    kern = functools.partial(_rw_scan_kernel, head=head)
    sds = jax.ShapeDtypeStruct
    return pl.pallas_call(
        kern,
        grid=(bsz, nc + nl),
        in_specs=side(0) + side(1),
        out_specs=[shared(0), shared(1)],
        out_shape=[sds((n, c_rw), BF16), sds((n, c_rw), BF16)],
        scratch_shapes=[pltpu.VMEM((2, head, c_rw), F32)],
        compiler_params=_params(("arbitrary", "arbitrary")),
        name="rwkv_scan",
    )(r, kk, v, lw, kd, beta, r, kk, v, lw, kd, beta)


def _rw_readout_kernel(yf_ref, yb_ref, g_ref, bonus_ref, lnw_ref, lnb_ref, ones_ref, o_ref, *, head):
    ones = ones_ref[...]
    pack = ones.shape[0]
    inv = 1.0 / head
    for s in range(o_ref.shape[1] // pack):
        cs = slice(s * pack, (s + 1) * pack)
        y = yf_ref[:, cs].astype(F32) + yb_ref[:, cs].astype(F32)
        mu = _dot_exact_rhs(y, ones) * inv
        yc = y - mu
        var = _dot_exact_rhs(yc * yc, ones) * inv
        yn = yc * lax.rsqrt(var + GN_EPS) * lnw_ref[:, cs] + lnb_ref[:, cs]
        o_ref[:, cs] = ((yn + bonus_ref[:, cs]) * g_ref[:, cs]).astype(o_ref.dtype)


def _rw_readout(y_fwd, y_bwd, g, bonus, ln_w, ln_b, *, head, tm):
    n, c_rw = y_fwd.shape
    pack = RW_PACK * head
    tn = _pick(c_rw, RW_PREP_WIDTH)
    tile = pl.BlockSpec((tm, tn), lambda i, j: (i, j))
    vec = pl.BlockSpec((1, tn), lambda i, j: (0, j))
    kern = functools.partial(_rw_readout_kernel, head=head)
    return pl.pallas_call(
        kern,
        grid=(n // tm, c_rw // tn),
        in_specs=[tile, tile, tile, tile, vec, vec, pl.BlockSpec((pack, pack), lambda i, j: (0, 0))],
        out_specs=tile,
        out_shape=jax.ShapeDtypeStruct((n, c_rw), BF16),
        compiler_params=_params(("arbitrary", "arbitrary")),
        name="rwkv_readout",
    )(y_fwd, y_bwd, g, bonus, ln_w.reshape(1, c_rw), ln_b.reshape(1, c_rw), _head_ones(pack, head))


def _final_norm_kernel(h_ref, gain_ref, o_ref):
    x = h_ref[...]
    ms = jnp.mean(x * x, axis=-1, keepdims=True)
    o_ref[...] = x * lax.rsqrt(ms + NORM_EPS) * gain_ref[...]


def _final_norm(h, gain, *, tm, row0):
    n, d = h.shape
    b0 = row0 // tm
    return pl.pallas_call(
        _final_norm_kernel,
        grid=((n - row0) // tm,),
        in_specs=[pl.BlockSpec((tm, d), lambda i: (i + b0, 0)), pl.BlockSpec((1, d), lambda i: (0, 0))],
        out_specs=pl.BlockSpec((tm, d), lambda i: (i, 0)),
        out_shape=jax.ShapeDtypeStruct((n - row0, d), F32),
        compiler_params=_params(("arbitrary",)),
        name="final_norm",
    )(h, gain.reshape(1, d))


def _to_s5_layout(u, bsz, l_ctx, l_lat, sel):
    t = S5_CHUNK
    n_ctx = bsz * l_ctx
    nj = u.shape[1] // 256

    def tiles(x, length):
        x = x.reshape(bsz, length // t, t, nj, 256)
        return jnp.transpose(x, (3, 1, 0, 2, 4)).reshape(nj, (length // t) * bsz, t * 256)

    return _lane_perm([tiles(u[:n_ctx], l_ctx), tiles(u[n_ctx:], l_lat)], sel)


def _from_s5_layout(y, bsz, l_ctx, l_lat, sel):
    t = S5_CHUNK
    g, rows, _ = y.shape
    m = sel.shape[0]
    nj = g // m
    x = jnp.transpose(y.reshape(nj, m, rows, 256), (0, 2, 1, 3)).reshape(nj, rows, m * 256)
    out = _lane_perm([x], sel).reshape(nj, t, rows, 256)
    rows_ctx = (l_ctx // t) * bsz

    def untile(o, length):
        o = o.reshape(nj, t, length // t, bsz, 256)
        return jnp.transpose(o, (3, 2, 1, 0, 4)).reshape(bsz * length, nj * 256)

    return untile(out[:, :, :rows_ctx], l_ctx), untile(out[:, :, rows_ctx:], l_lat)


def kernel(x, c, ctx, c_ctx, ada_down, ada_up, ada_b, norm1, w_in, s5_a_re, s5_a_im, s5_log_dt, s5_b_re, s5_b_im, s5_c_re, s5_c_im, s5_d, s5_glu_w, s5_glu_b, rw_mu, rw_w0, rw_w_up, rw_a0, rw_a_up, rw_g_up, rw_k_k, rw_k_a, rw_r_k, rw_ln_w, rw_ln_b, w_proj_a, w_proj_b, gate_up, gate_b, w_out, norm2, mlp_w1, mlp_w2, norm_f):
    bsz, l_lat, d = x.shape
    l_ctx = ctx.shape[1]
    depth = w_in.shape[0]
    n_ctx = bsz * l_ctx
    n_lat = bsz * l_lat
    s5_groups, s5_state = s5_a_re.shape[2], s5_a_re.shape[3]
    s5_gch = s5_b_re.shape[-1]
    c_s5 = s5_groups * s5_gch
    rw_heads, head = rw_r_k.shape[1], rw_r_k.shape[2]
    c_rw = rw_heads * head
    lora_w, lora_a, lora_g = rw_w_up.shape[2], rw_a_up.shape[2], rw_g_up.shape[1]
    gate_rank = gate_up.shape[2]
    n_small_rw = 2 * lora_w + 2 * lora_a + lora_g
    c_big = c_s5 + 3 * c_rw
    assert w_in.shape[2] == c_big + n_small_rw + gate_rank
    assert bsz % 8 == 0 and l_ctx % RW_CHUNK == 0 and l_lat % RW_CHUNK == 0 and l_lat % GRID_W == 0
    assert s5_gch * S5_CHUNK == 256 and RW_PACK * head == 256 and c_rw % 256 == 0 and gate_rank % 128 == 0
    assert RW_CHUNK == head
    assert c_s5 % IN_PROJ_TN == 0 and (3 * c_rw) % IN_PROJ_TN == 0 and n_small_rw + gate_rank <= IN_PROJ_TN

    tm = _pick(l_lat, _pick(n_ctx, TM))

    def row_of_block(i):
        ctx_blocks = n_ctx // tm
        return jnp.where(i < ctx_blocks, bsz, (i - ctx_blocks) // (l_lat // tm))

    n_cond = ((bsz + 1 + 7) // 8) * 8
    cond = jnp.zeros((n_cond, d), F32).at[:bsz].set(c).at[bsz].set(c_ctx)
    mods_all = _ada_modulation(cond, ada_down, ada_up, ada_b)
    mods_all = jnp.transpose(mods_all.reshape(depth, n_cond, 6, 1, d), (0, 2, 1, 3, 4))

    h = jnp.concatenate([ctx.reshape(n_ctx, d), x.reshape(n_lat, d)], axis=0)

    lane_sel = _lane_select_matrices(s5_gch)
    w_in_bf = _to_bf16(w_in)
    glu_bf = _to_bf16(s5_glu_w)
    proj_a_bf, proj_b_bf, w_out_bf = _to_bf16(w_proj_a), _to_bf16(w_proj_b), _to_bf16(w_out)
    gate_up_bf = _to_bf16(gate_up.reshape(depth, 2 * gate_rank, d)).reshape(gate_up.shape)
    w1_bf, w2_bf = _to_bf16(mlp_w1), _to_bf16(mlp_w2)

    for l in range(depth):
        last = l == depth - 1
        row0 = n_ctx if last else 0
        mods = mods_all[l]
        lp = {"rw_mu": rw_mu[l], "rw_w0": rw_w0[l], "rw_w_up": rw_w_up[l], "rw_a0": rw_a0[l],
              "rw_a_up": rw_a_up[l], "rw_g_up": rw_g_up[l], "rw_k_k": rw_k_k[l], "rw_k_a": rw_k_a[l],
              "rw_r_k": rw_r_k[l]}
        w_all = jnp.concatenate([w_in_bf[l, :, :c_big], w_in_bf[l, :, c_big + n_small_rw:],
                                 w_in_bf[l, :, c_big:c_big + n_small_rw],
                                 jnp.zeros((d, IN_PROJ_TN - n_small_rw - gate_rank), BF16)], axis=1)
        z_u, z_rkv, z_small = _in_proj(h, norm1[l], mods, w_all, row_of_block, tm=tm, tn=IN_PROJ_TN,
                                       c_u=c_s5, c_rkv=3 * c_rw)

        kf, kb, bm, cmt, lam = _s5_matrices(s5_a_re[l], s5_a_im[l], s5_log_dt[l], s5_b_re[l], s5_b_im[l],
                                            s5_c_re[l], s5_c_im[l], s5_d[l])
        u_g = _to_s5_layout(z_u, bsz, l_ctx, l_lat, lane_sel)
        y_g = _s5_scan(u_g, kf, kb, bm, cmt, lam, n_ctx=l_ctx // S5_CHUNK, n_chunks=(l_ctx + l_lat) // S5_CHUNK, bsz=bsz)
        y_ctx, y_lat = _from_s5_layout(y_g, bsz, l_ctx, l_lat, lane_sel)
        ya = _s5_glu(y_ctx, y_lat, glu_bf, l, s5_glu_b[l], tm=tm)

        r, kk, v, lw, kd, beta, g, bonus = _rw_prep(z_rkv, z_small, lp, bsz=bsz, l_ctx=l_ctx, l_lat=l_lat,
                                                    c_rw=c_rw, head=head, gate_rank=gate_rank)
        y_fwd, y_bwd = _rw_scan(r, kk, v, lw, kd, beta, bsz=bsz, l_ctx=l_ctx, l_lat=l_lat, head=head)
        yb = _rw_readout(y_fwd, y_bwd, g, bonus, rw_ln_w[l], rw_ln_b[l], head=head, tm=tm)

        h = _merge_out(ya, yb, z_small, proj_a_bf, proj_b_bf, gate_up_bf, gate_b[l], w_out_bf, l, h, mods,
                       row_of_block, tm=tm, row0=row0)
        h = _mlp(h, norm2[l], mods, w1_bf, w2_bf, l, row_of_block, tm=tm, row0=row0)

    out = _final_norm(h, norm_f, tm=tm, row0=n_ctx)
    return out.reshape(bsz, l_lat, d)
```

```python
import functools

import jax
import jax.numpy as jnp
from jax import lax
from jax.experimental import pallas as pl
from jax.experimental.pallas import tpu as pltpu

F32 = jnp.float32
BF16 = jnp.bfloat16

GRID_W = 64
NORM_EPS = 1e-6
GN_EPS = 64e-5
A_RE_MAX = -1e-4
S5_CHUNK = 16
RW_CHUNK = 64
RW_PACK = 4
RW_PREP_WIDTH = 1024
RW_PREP_ROWS = 512
TM = 512
IN_PROJ_TN = 1024
VMEM_LIMIT = 56 * 1024 * 1024


def _params(sem):
    return pltpu.CompilerParams(dimension_semantics=sem, vmem_limit_bytes=VMEM_LIMIT)


def _pick(n, pref):
    t = pref
    while n % t:
        t //= 2
    return t


def _dot(a, b):
    return jnp.dot(a, b, preferred_element_type=F32)


def _split(x):
    hi = x.astype(BF16)
    lo = (x - hi.astype(F32)).astype(BF16)
    return hi, lo


def _dot3(a, b):
    ah, al = _split(a)
    bh, bl = _split(b)
    return _dot(ah, bh) + (_dot(ah, bl) + _dot(al, bh))


def _dot_exact_rhs(a, b_bf16):
    ah, al = _split(a)
    return _dot(ah, b_bf16) + _dot(al, b_bf16)


def _sigmoid(x):
    return 1.0 / (1.0 + jnp.exp(-x))


def _gelu_tanh(x):
    c = 0.7978845608028654
    return 0.5 * x * (1.0 + jnp.tanh(c * (x + 0.044715 * (x * x * x))))


def _cast_kernel(w_ref, o_ref):
    o_ref[...] = w_ref[...].astype(o_ref.dtype)


def _to_bf16(w):
    nl, r, c = w.shape
    rows_4mib = max(16, (1 << 20) // c)
    tr = _pick(r, 1 << (rows_4mib.bit_length() - 1))
    out = pl.pallas_call(
        _cast_kernel,
        grid=(nl * r // tr,),
        in_specs=[pl.BlockSpec((tr, c), lambda i: (i, 0))],
        out_specs=pl.BlockSpec((tr, c), lambda i: (i, 0)),
        out_shape=jax.ShapeDtypeStruct((nl * r, c), BF16),
        compiler_params=_params(("arbitrary",)),
        name="weights_to_bf16",
    )(w.reshape(nl * r, c))
    return out.reshape(nl, r, c)


def _ada_kernel(cond_ref, wd_ref, wu_ref, b_ref, o_ref, mid_scr):
    @pl.when(pl.program_id(1) == 0)
    def _():
        c = cond_ref[...]
        mid_scr[...] = _dot3(c * _sigmoid(c), wd_ref[...])

    o_ref[...] = _dot3(mid_scr[...], wu_ref[...]) + b_ref[...]


def _ada_modulation(cond, w_down, w_up, bias):
    nl, d, rank = w_down.shape
    r = cond.shape[0]
    n_out = w_up.shape[-1]
    tn = _pick(n_out, 2048)
    return pl.pallas_call(
        _ada_kernel,
        grid=(nl, n_out // tn),
        in_specs=[
            pl.BlockSpec((r, d), lambda l, j: (0, 0)),
            pl.BlockSpec((None, d, rank), lambda l, j: (l, 0, 0)),
            pl.BlockSpec((None, rank, tn), lambda l, j: (l, 0, j)),
            pl.BlockSpec((None, 1, tn), lambda l, j: (l, 0, j)),
        ],
        out_specs=pl.BlockSpec((None, r, tn), lambda l, j: (l, 0, j)),
        out_shape=jax.ShapeDtypeStruct((nl, r, n_out), F32),
        scratch_shapes=[pltpu.VMEM((r, rank), F32)],
        compiler_params=_params(("arbitrary", "arbitrary")),
        name="ada_modulation",
    )(cond, w_down, w_up, bias.reshape(nl, 1, n_out))


def _in_proj_kernel(h_ref, gain_ref, shift_ref, scale_ref, w_ref, zu_ref, zrkv_ref, zs_ref, n_scr, *, ju, jr):
    j = pl.program_id(1)

    @pl.when(j == 0)
    def _():
        x = h_ref[...]
        ms = jnp.mean(x * x, axis=-1, keepdims=True)
        xn = x * lax.rsqrt(ms + NORM_EPS) * gain_ref[...]
        n_scr[...] = (xn * (1.0 + scale_ref[...]) + shift_ref[...]).astype(BF16)

    acc = _dot(n_scr[...], w_ref[...])

    @pl.when(j < ju)
    def _():
        for t in range(zu_ref.shape[0]):
            zu_ref[t] = acc[:, t * 256:(t + 1) * 256].astype(zu_ref.dtype)

    @pl.when(jnp.logical_and(j >= ju, j < ju + jr))
    def _():
        zrkv_ref[...] = acc.astype(zrkv_ref.dtype)

    @pl.when(j == ju + jr)
    def _():
        zs_ref[...] = acc


def _in_proj(h, gain, mods, w_all, row_of_block, *, tm, tn, c_u, c_rkv):
    n, d = h.shape
    ju, jr = c_u // tn, c_rkv // tn
    assert w_all.shape[1] == (ju + jr + 1) * tn
    mod = lambda which: pl.BlockSpec((None, None, 1, d), lambda i, j: (which, row_of_block(i), 0, 0))
    sds = jax.ShapeDtypeStruct
    return pl.pallas_call(
        functools.partial(_in_proj_kernel, ju=ju, jr=jr),
        grid=(n // tm, ju + jr + 1),
        in_specs=[
            pl.BlockSpec((tm, d), lambda i, j: (i, 0), pipeline_mode=pl.Buffered(1)),
            pl.BlockSpec((1, d), lambda i, j: (0, 0)),
            mod(0), mod(1),
            pl.BlockSpec((d, tn), lambda i, j: (0, j)),
        ],
        out_specs=[pl.BlockSpec((tn // 256, tm, 256), lambda i, j: (jnp.minimum(j, ju - 1), i, 0)),
                   pl.BlockSpec((tm, tn), lambda i, j: (i, jnp.clip(j - ju, 0, jr - 1))),
                   pl.BlockSpec((tm, tn), lambda i, j: (i, 0))],
        out_shape=[sds((c_u // 256, n, 256), BF16), sds((n, c_rkv), BF16), sds((n, tn), F32)],
        scratch_shapes=[pltpu.VMEM((tm, d), BF16)],
        compiler_params=_params(("arbitrary", "arbitrary")),
        name="in_proj",
    )(h, gain.reshape(1, d), mods, mods, w_all)


def _mlp_kernel(h_ref, gain_ref, shift_ref, scale_ref, gate_ref, w1_ref, w2_ref, o_ref, n_scr):
    j = pl.program_id(1)

    @pl.when(j == 0)
    def _():
        x = h_ref[...]
        ms = jnp.mean(x * x, axis=-1, keepdims=True)
        xn = x * lax.rsqrt(ms + NORM_EPS) * gain_ref[...]
        n_scr[...] = (xn * (1.0 + scale_ref[...]) + shift_ref[...]).astype(BF16)
        o_ref[...] = jnp.zeros_like(o_ref)

    hid = jnp.square(jnp.maximum(_dot(n_scr[...], w1_ref[...]), 0.0)).astype(BF16)
    o_ref[...] += _dot(hid, w2_ref[...])

    @pl.when(j == pl.num_programs(1) - 1)
    def _():
        o_ref[...] = h_ref[...] + gate_ref[...] * o_ref[...]


def _mlp(h, gain, mods, w1, w2, layer, row_of_block, *, tm, row0):
    n, d = h.shape
    d_ff = w1.shape[2]
    tf = _pick(d_ff, 512)
    b0 = row0 // tm
    mod = lambda which: pl.BlockSpec((None, None, 1, d), lambda i, j: (which, row_of_block(i + b0), 0, 0))
    return pl.pallas_call(
        _mlp_kernel,
        grid=((n - row0) // tm, d_ff // tf),
        in_specs=[
            pl.BlockSpec((tm, d), lambda i, j: (i + b0, 0), pipeline_mode=pl.Buffered(1)),
            pl.BlockSpec((1, d), lambda i, j: (0, 0)),
            mod(3), mod(4), mod(5),
            pl.BlockSpec((None, d, tf), lambda i, j: (layer, 0, j)),
            pl.BlockSpec((None, tf, d), lambda i, j: (layer, j, 0)),
        ],
        out_specs=pl.BlockSpec((tm, d), lambda i, j: (i + b0, 0)),
        out_shape=jax.ShapeDtypeStruct((n, d), F32),
        scratch_shapes=[pltpu.VMEM((tm, d), BF16)],
        input_output_aliases={0: 0},
        compiler_params=_params(("arbitrary", "arbitrary")),
        name="mlp",
    )(h, gain.reshape(1, d), mods, mods, mods, w1, w2)


def _merge_out_kernel(ya_ref, yb_ref, gz_ref, wa_ref, wb_ref, ga_ref, gb_ref, ba_ref, bb_ref, wo_ref, h_ref,
                      gate_ref, o_ref):
    j = pl.program_id(1)

    @pl.when(j == 0)
    def _():
        o_ref[...] = jnp.zeros_like(o_ref)

    gz = gz_ref[...].astype(BF16)
    ga = _sigmoid(_dot(gz, ga_ref[...]) + ba_ref[...])
    gb = _sigmoid(_dot(gz, gb_ref[...]) + bb_ref[...])
    mixed = (ga * _dot(ya_ref[...], wa_ref[...]) + gb * _dot(yb_ref[...], wb_ref[...])).astype(BF16)
    o_ref[...] += _dot(mixed, wo_ref[...])

    @pl.when(j == pl.num_programs(1) - 1)
    def _():
        o_ref[...] = h_ref[...] + gate_ref[...] * o_ref[...]


def _merge_out(ya, yb, z_small, wa, wb, gate_up, gate_b, w_out, layer, h, mods, row_of_block, *, tm, row0):
    n, d = h.shape
    ca, cb = ya.shape[1], yb.shape[1]
    d_mid = wa.shape[2]
    rank = gate_up.shape[2]
    tn = _pick(d_mid, 512)
    b0 = row0 // tm
    once = pl.Buffered(1)
    return pl.pallas_call(
        _merge_out_kernel,
        grid=((n - row0) // tm, d_mid // tn),
        in_specs=[
            pl.BlockSpec((tm, ca), lambda i, j: (i + b0, 0), pipeline_mode=once),
            pl.BlockSpec((tm, cb), lambda i, j: (i + b0, 0), pipeline_mode=once),
            pl.BlockSpec((tm, rank), lambda i, j: (i + b0, 0)),
            pl.BlockSpec((None, ca, tn), lambda i, j: (layer, 0, j)),
            pl.BlockSpec((None, cb, tn), lambda i, j: (layer, 0, j)),
            pl.BlockSpec((None, None, rank, tn), lambda i, j: (layer, 0, 0, j)),
            pl.BlockSpec((None, None, rank, tn), lambda i, j: (layer, 1, 0, j)),
            pl.BlockSpec((None, 1, tn), lambda i, j: (0, 0, j)),
            pl.BlockSpec((None, 1, tn), lambda i, j: (1, 0, j)),
            pl.BlockSpec((None, tn, d), lambda i, j: (layer, j, 0)),
            pl.BlockSpec((tm, d), lambda i, j: (i + b0, 0), pipeline_mode=once),
            pl.BlockSpec((None, None, 1, d), lambda i, j: (2, row_of_block(i + b0), 0, 0)),
        ],
        out_specs=pl.BlockSpec((tm, d), lambda i, j: (i + b0, 0)),
        out_shape=jax.ShapeDtypeStruct((n, d), F32),
        input_output_aliases={10: 0},
        compiler_params=_params(("arbitrary", "arbitrary")),
        name="merge_out_proj",
    )(ya, yb, z_small, wa, wb, gate_up, gate_up, gate_b.reshape(2, 1, d_mid), gate_b.reshape(2, 1, d_mid),
      w_out, h, mods)


def _s5_matrices(a_re, a_im, log_dt, b_re, b_im, c_re, c_im, d_skip):
    t_len = S5_CHUNK
    hp = lax.Precision.HIGHEST
    lam_re = jnp.minimum(a_re.astype(F32), A_RE_MAX)
    lam_im = a_im.astype(F32)
    dt = jnp.exp(log_dt.astype(F32))[..., None]
    j = jnp.arange(t_len + 1, dtype=F32)[:, None, None, None]
    mag = jnp.exp(j * (lam_re * dt))
    ang = j * (lam_im * dt)
    pw_re = mag * jnp.cos(ang)
    pw_im = mag * jnp.sin(ang)
    nr, ni = pw_re[1] - 1.0, pw_im[1]
    den = lam_re * lam_re + lam_im * lam_im
    qr = (nr * lam_re + ni * lam_im) / den
    qi = (ni * lam_re - nr * lam_im) / den
    bbr = qr[..., None] * b_re - qi[..., None] * b_im
    bbi = qr[..., None] * b_im + qi[..., None] * b_re
    n_dir, g, p, c = bbr.shape
    clr = c_re[None] * pw_re[:t_len, :, :, None, :] - c_im[None] * pw_im[:t_len, :, :, None, :]
    cli = c_re[None] * pw_im[:t_len, :, :, None, :] + c_im[None] * pw_re[:t_len, :, :, None, :]
    kj = (jnp.einsum("jdgop,dgpi->dgijo", clr, bbr, precision=hp)
          - jnp.einsum("jdgop,dgpi->dgijo", cli, bbi, precision=hp))
    lag0 = (jnp.arange(t_len) == 0).astype(F32)
    skip = d_skip.astype(F32).reshape(g, c)
    kf = kj[0] + skip[:, :, None, None] * lag0[None, None, :, None] * jnp.eye(c, dtype=F32)[None, :, None, :]
    kf = kf.reshape(g, c, t_len * c)
    kb = jnp.flip(kj[1], axis=2).reshape(g, c, t_len * c)
    bbr_t = jnp.transpose(bbr, (0, 1, 3, 2))
    bbi_t = jnp.transpose(bbi, (0, 1, 3, 2))

    def powers(d, e):
        return jnp.transpose(pw_re[e, d], (1, 0, 2)), jnp.transpose(pw_im[e, d], (1, 0, 2))

    def state_in(d, e):
        pr, pi = powers(d, e)
        re = pr[:, :, None, :] * bbr_t[d][:, None] - pi[:, :, None, :] * bbi_t[d][:, None]
        im = pr[:, :, None, :] * bbi_t[d][:, None] + pi[:, :, None, :] * bbr_t[d][:, None]
        return re.reshape(g, t_len * c, p), im.reshape(g, t_len * c, p)

    fr, fi = state_in(0, (t_len - 1) - jnp.arange(t_len))
    br, bi = state_in(1, jnp.arange(t_len))
    bm = jnp.concatenate([fr, br, fi, bi], axis=-1)

    def state_out(d, e):
        pr, pi = powers(d, e)
        re = c_re[d][:, None] * pr[:, :, None, :] - c_im[d][:, None] * pi[:, :, None, :]
        im = c_re[d][:, None] * pi[:, :, None, :] + c_im[d][:, None] * pr[:, :, None, :]
        return re.reshape(g, t_len * c, p), -im.reshape(g, t_len * c, p)

    cfr, cfi = state_out(0, jnp.arange(t_len) + 1)
    cbr, cbi = state_out(1, t_len - jnp.arange(t_len))
    cmt = jnp.concatenate([cfr, cbr, cfi, cbi], axis=-1)
    lam = jnp.stack([jnp.concatenate([pw_re[t_len, 0], pw_re[t_len, 1]], axis=-1),
                     jnp.concatenate([pw_im[t_len, 0], pw_im[t_len, 1]], axis=-1)], axis=1)
    return kf, kb, bm.astype(BF16), cmt.astype(BF16), lam


def _s5_kernel(u_ref, bm_ref, kf_ref, kb_ref, cmt_ref, lam_ref, y_ref, xin_scr, fwd_scr, bwd_scr, km_scr, *,
               n_ctx, n_chunks, bsz, p_state):
    u = u_ref[...]
    xin_scr[...] = _dot(u, bm_ref[...])
    gch, tc = kf_ref.shape
    kf = kf_ref[...]
    kb = kb_ref[...]
    lane_k = lax.broadcasted_iota(jnp.int32, (gch, tc), 1)
    for s in range(tc // gch):
        fwd = kf if s == 0 else jnp.where(lane_k >= gch * s, pltpu.roll(kf, gch * s, axis=1), 0.0)
        shift = (gch * (s + 1)) % tc
        bwd = kb if shift == 0 else jnp.where(lane_k < gch * (s + 1), pltpu.roll(kb, shift, axis=1), 0.0)
        km_scr[s * gch:(s + 1) * gch, :] = (fwd + bwd).astype(BF16)
    two_p = 2 * p_state
    lam_r = jnp.broadcast_to(lam_ref[0:1, :], (bsz, two_p))
    lam_i = jnp.broadcast_to(lam_ref[1:2, :], (bsz, two_p))
    is_fwd = lax.broadcasted_iota(jnp.int32, (bsz, two_p), 1) < p_state

    def body(i, carry):
        re, im = carry
        pb = jnp.where(i < n_ctx, n_ctx - 1 - i, n_chunks + n_ctx - 1 - i)
        rf = pl.multiple_of(i * bsz, bsz)
        rb = pl.multiple_of(pb * bsz, bsz)
        xf = xin_scr[pl.ds(rf, bsz), :]
        xb = xin_scr[pl.ds(rb, bsz), :]
        x_re = jnp.where(is_fwd, xf[:, :two_p], xb[:, :two_p])
        x_im = jnp.where(is_fwd, xf[:, two_p:], xb[:, two_p:])
        state = jnp.concatenate([re, im], axis=1)
        fwd_scr[pl.ds(rf, bsz), :] = state
        bwd_scr[pl.ds(rb, bsz), :] = state
        return (re * lam_r - im * lam_i + x_re, re * lam_i + im * lam_r + x_im)

    zero = jnp.zeros((bsz, two_p), F32)
    lax.fori_loop(0, n_chunks, body, (zero, zero))
    lane = lax.broadcasted_iota(jnp.int32, (1, 2 * two_p), 1)
    fwd_lane = (lane % two_p) < p_state
    xs = jnp.where(fwd_lane, fwd_scr[...], bwd_scr[...]).astype(BF16)
    from_state = lax.dot_general(xs, cmt_ref[...], (((1,), (1,)), ((), ())), preferred_element_type=F32)
    y_ref[...] = (_dot(u, km_scr[...]) + from_state).astype(y_ref.dtype)


def _s5_scan(u_g, kf, kb, bm, cmt, lam, *, n_ctx, n_chunks, bsz):
    g, rows, tc = u_g.shape
    gch = kf.shape[1]
    m = 256 // gch
    p4 = bm.shape[-1]
    kern = functools.partial(_s5_kernel, n_ctx=n_ctx, n_chunks=n_chunks, bsz=bsz, p_state=p4 // 4)
    return pl.pallas_call(
        kern,
        grid=(g,),
        in_specs=[
            pl.BlockSpec((None, rows, tc), lambda i: (i, 0, 0)),
            pl.BlockSpec((None, tc, p4), lambda i: (i, 0, 0)),
            pl.BlockSpec((None, gch, tc), lambda i: (i, 0, 0)),
            pl.BlockSpec((None, gch, tc), lambda i: (i, 0, 0)),
            pl.BlockSpec((None, tc, p4), lambda i: (i, 0, 0)),
            pl.BlockSpec((None, 2, p4 // 2), lambda i: (i, 0, 0)),
        ],
        out_specs=pl.BlockSpec((None, rows, tc), lambda i: (i // m, 0, i % m)),
        out_shape=jax.ShapeDtypeStruct((g // m, rows, m * tc), BF16),
        scratch_shapes=[pltpu.VMEM((rows, p4), F32), pltpu.VMEM((rows, p4), F32), pltpu.VMEM((rows, p4), F32),
                        pltpu.VMEM((tc, tc), BF16)],
        compiler_params=_params(("arbitrary",)),
        name="s5_scan",
    )(u_g, bm, kf, kb, cmt, lam)


def _glu_kernel(yc_ref, yl_ref, w_ref, b_ref, o_ref, *, ctx_blocks):
    def glu(y):
        z = _gelu_tanh(y.astype(F32))
        o_ref[...] = (z * _sigmoid(_dot(z.astype(BF16), w_ref[...]) + b_ref[...])).astype(o_ref.dtype)

    is_ctx = pl.program_id(0) < ctx_blocks

    @pl.when(is_ctx)
    def _():
        glu(yc_ref[...])

    @pl.when(jnp.logical_not(is_ctx))
    def _():
        glu(yl_ref[...])


def _s5_glu(y_ctx, y_lat, w, layer, b, *, tm):
    c = y_ctx.shape[1]
    cb = y_ctx.shape[0] // tm
    n = y_ctx.shape[0] + y_lat.shape[0]
    return pl.pallas_call(
        functools.partial(_glu_kernel, ctx_blocks=cb),
        grid=(n // tm,),
        in_specs=[
            pl.BlockSpec((tm, c), lambda i: (jnp.minimum(i, cb - 1), 0)),
            pl.BlockSpec((tm, c), lambda i: (jnp.maximum(i - cb, 0), 0)),
            pl.BlockSpec((None, c, c), lambda i: (layer, 0, 0)),
            pl.BlockSpec((1, c), lambda i: (0, 0)),
        ],
        out_specs=pl.BlockSpec((tm, c), lambda i: (i, 0)),
        out_shape=jax.ShapeDtypeStruct((n, c), BF16),
        compiler_params=_params(("arbitrary",)),
        name="s5_glu",
    )(y_ctx, y_lat, w, b.reshape(1, c))


def _lane_select_matrices(gch):
    m = 256 // gch
    a = jnp.arange(m)[:, None, None, None]
    b = jnp.arange(m)[None, :, None, None]
    k = jnp.arange(256)[None, None, :, None]
    n = jnp.arange(256)[None, None, None, :]
    sel = (k == a * gch + n % gch) & (n // gch == b)
    return sel.astype(BF16).reshape(m, m * 256, 256)


def _lane_perm_kernel(*refs):
    *x_refs, sel_ref, o_ref = refs
    off = 0
    for x_ref in x_refs:
        rows = x_ref.shape[0]
        o_ref[off:off + rows, :] = _dot(x_ref[...], sel_ref[...]).astype(o_ref.dtype)
        off += rows


def _lane_perm(xs, sel):
    nj = xs[0].shape[0]
    m, kdim, _ = sel.shape
    rows = sum(x.shape[1] for x in xs)
    return pl.pallas_call(
        _lane_perm_kernel,
        grid=(nj, m),
        in_specs=[pl.BlockSpec((None, x.shape[1], kdim), lambda j, a: (j, 0, 0)) for x in xs]
        + [pl.BlockSpec((None, kdim, 256), lambda j, a: (a, 0, 0))],
        out_specs=pl.BlockSpec((None, rows, 256), lambda j, a: (j * m + a, 0, 0)),
        out_shape=jax.ShapeDtypeStruct((nj * m, rows, 256), BF16),
        compiler_params=_params(("arbitrary", "arbitrary")),
        name="s5_lane_perm",
    )(*xs, sel)


def _head_ones(width, head):
    i = jnp.arange(width) // head
    return (i[:, None] == i[None, :]).astype(BF16)


def _rw_prep_kernel(zr_ref, zrp_ref, zrn_ref, zk_ref, zkp_ref, zkn_ref, zv_ref, zvp_ref, zvn_ref,
                    zs_ref, zsp_ref, zsn_ref, mur_ref, muk_ref, muv_ref, mus_ref,
                    w0_ref, wup_ref, a0_ref, aup_ref, gup_ref, kk_ref, ka_ref, rk_ref, ones_ref,
                    r_o, kk_o, v_o, lw_o, kd_o, beta_o, g_o, bonus_o, *,
                    lora_w, lora_a, small0, n_ctx, l_ctx, l_lat, c_rw, quarter):
    tm, width = zr_ref.shape
    halo = zrp_ref.shape[0]
    g0 = pl.program_id(0) * tm
    is_ctx = g0 < n_ctx
    pos = jnp.where(is_ctx, g0 % l_ctx, (g0 - n_ctx) % l_lat) + lax.broadcasted_iota(jnp.int32, (tm, 1), 0)
    row = lax.broadcasted_iota(jnp.int32, (tm, 1), 0)
    ok_m1 = jnp.where(is_ctx, pos, pos % GRID_W) >= 1
    ok_p1 = jnp.where(is_ctx, l_ctx - 1 - pos, GRID_W - 1 - pos % GRID_W) >= 1
    ok_mw = jnp.where(is_ctx, 0, pos) >= GRID_W
    ok_pw = jnp.where(is_ctx, 0, l_lat - GRID_W - pos) >= 1

    def mix(cur, prev, nxt, mu, col0):
        cur, prev, nxt = cur.astype(F32), prev.astype(F32), nxt.astype(F32)
        width = cur.shape[1]
        col = col0 + lax.broadcasted_iota(jnp.int32, (1, width), 1)
        cq = col // quarter
        cls = jnp.where(is_ctx, cq // 2, cq)
        m1 = jnp.where(row == 0, prev[halo - 1:halo, :], pltpu.roll(cur, 1, axis=0))
        p1 = jnp.where(row == tm - 1, nxt[0:1, :], pltpu.roll(cur, tm - 1, axis=0))
        mw = jnp.concatenate([prev, cur[:tm - halo]], axis=0)
        pw = jnp.concatenate([cur[halo:], nxt], axis=0)
        zs = jnp.where(cls == 0, jnp.where(ok_m1, m1, 0.0),
                       jnp.where(cls == 1, jnp.where(ok_p1, p1, 0.0),
                                 jnp.where(cls == 2, jnp.where(ok_mw, mw, 0.0), jnp.where(ok_pw, pw, 0.0))))
        return cur + (zs - cur) * mu

    sm = mix(zs_ref[:, small0:], zsp_ref[:, small0:], zsn_ref[:, small0:], mus_ref[...], 3 * c_rw)
    o_a = 2 * lora_w
    o_g = o_a + 2 * lora_a
    ones = ones_ref[...]
    tn = ones.shape[0]
    gate_in = _sigmoid(sm[:, o_g:]).astype(BF16)
    wd = [jnp.tanh(sm[:, d * lora_w:(d + 1) * lora_w]) for d in range(2)]
    ad = [sm[:, o_a + d * lora_a:o_a + (d + 1) * lora_a] for d in range(2)]
    j0 = pl.program_id(1) * width
    for s in range(width // tn):
        cs = slice(s * tn, (s + 1) * tn)
        r = mix(zr_ref[:, cs], zrp_ref[:, cs], zrn_ref[:, cs], mur_ref[:, cs], j0 + s * tn)
        k = mix(zk_ref[:, cs], zkp_ref[:, cs], zkn_ref[:, cs], muk_ref[:, cs], c_rw + j0 + s * tn)
        v = mix(zv_ref[:, cs], zvp_ref[:, cs], zvn_ref[:, cs], muv_ref[:, cs], 2 * c_rw + j0 + s * tn)
        g_o[:, cs] = _dot(gate_in, gup_ref[:, cs]).astype(g_o.dtype)
        kx = k * kk_ref[:, cs]
        ss = _dot_exact_rhs(kx * kx, ones)
        kk = kx * lax.rsqrt(jnp.maximum(ss, 1e-24))
        k_sum = jnp.zeros_like(k)
        for d in range(2):
            w = w0_ref[d][:, cs] + _dot3(wd[d], wup_ref[d][:, cs])
            w = -(jnp.maximum(-w, 0.0) + jnp.log(1.0 + jnp.exp(-jnp.abs(w)))) - 0.5
            lw_o[d, :, cs] = -jnp.exp(w)
            a = _sigmoid(a0_ref[d][:, cs] + _dot3(ad[d], aup_ref[d][:, cs]))
            kd = k * (1.0 + (a - 1.0) * ka_ref[:, cs])
            kd_o[d, :, cs] = kd.astype(kd_o.dtype)
            beta_o[d, :, cs] = (kk * a).astype(beta_o.dtype)
            k_sum = k_sum + kd
        r_o[:, cs] = r.astype(r_o.dtype)
        kk_o[:, cs] = kk.astype(kk_o.dtype)
        v_o[:, cs] = v.astype(v_o.dtype)
        bonus_o[:, cs] = (_dot_exact_rhs(r * (k_sum * 0.5) * rk_ref[:, cs], ones) * v).astype(bonus_o.dtype)


def _rw_prep(z_rkv, z_small, lp, *, bsz, l_ctx, l_lat, c_rw, head, gate_rank):
    n = z_rkv.shape[0]
    pack = RW_PACK * head
    tn = _pick(c_rw, RW_PREP_WIDTH)
    nj = c_rw // tn
    halo = GRID_W
    tm = _pick(l_lat, _pick(l_ctx, RW_PREP_ROWS))
    assert tm % halo == 0 and tm > halo
    lora_w = lp["rw_w_up"].shape[1]
    lora_a = lp["rw_a_up"].shape[1]
    lora_g = lp["rw_g_up"].shape[0]
    mu = lp["rw_mu"].reshape(1, -1)
    n_small = 2 * lora_w + 2 * lora_a + lora_g
    n_small_all = gate_rank + n_small
    assert mu.shape[1] == 3 * c_rw + n_small and mu.shape[1] % 4 == 0 and n_small_all % 128 == 0
    mu_big = mu[:, :3 * c_rw]
    mu_small = mu[:, 3 * c_rw:]
    row = lambda a: a.reshape(1, c_rw)
    hb = tm // halo
    last_halo = n // halo - 1

    def tiles(off, width_blocks):
        w = tn if width_blocks else n_small_all
        col = (lambda j: j + off) if width_blocks else (lambda j: 0)
        return [pl.BlockSpec((tm, w), lambda i, j: (i, col(j))),
                pl.BlockSpec((halo, w), lambda i, j: (jnp.maximum(i * hb - 1, 0), col(j))),
                pl.BlockSpec((halo, w), lambda i, j: (jnp.minimum((i + 1) * hb, last_halo), col(j)))]

    vec = lambda off: pl.BlockSpec((1, tn), lambda i, j: (0, j + off))
    dirvec = pl.BlockSpec((2, 1, tn), lambda i, j: (0, 0, j))
    out_t = pl.BlockSpec((tm, tn), lambda i, j: (i, j))
    out_d = pl.BlockSpec((2, tm, tn), lambda i, j: (0, i, j))
    sds = jax.ShapeDtypeStruct
    kern = functools.partial(_rw_prep_kernel, lora_w=lora_w, lora_a=lora_a, small0=gate_rank, n_ctx=bsz * l_ctx,
                             l_ctx=l_ctx, l_lat=l_lat, c_rw=c_rw, quarter=mu.shape[1] // 4)
    return pl.pallas_call(
        kern,
        grid=(n // tm, nj),
        in_specs=[
            *tiles(0, True), *tiles(nj, True), *tiles(2 * nj, True), *tiles(0, False),
            vec(0), vec(nj), vec(2 * nj),
            pl.BlockSpec((1, n_small), lambda i, j: (0, 0)),
            dirvec,
            pl.BlockSpec((2, lora_w, tn), lambda i, j: (0, 0, j)),
            dirvec,
            pl.BlockSpec((2, lora_a, tn), lambda i, j: (0, 0, j)),
            pl.BlockSpec((lora_g, tn), lambda i, j: (0, j)),
            vec(0), vec(0), vec(0),
            pl.BlockSpec((pack, pack), lambda i, j: (0, 0)),
        ],
        out_specs=[out_t, out_t, out_t, out_d, out_d, out_d, out_t, out_t],
        out_shape=[sds((n, c_rw), BF16), sds((n, c_rw), BF16), sds((n, c_rw), BF16),
                   sds((2, n, c_rw), F32), sds((2, n, c_rw), BF16), sds((2, n, c_rw), BF16),
                   sds((n, c_rw), BF16), sds((n, c_rw), BF16)],
        compiler_params=_params(("arbitrary", "arbitrary")),
        name="rwkv_prep",
    )(*([z_rkv] * 9), *([z_small] * 3), mu_big, mu_big, mu_big, mu_small,
      lp["rw_w0"].reshape(2, 1, c_rw), lp["rw_w_up"], lp["rw_a0"].reshape(2, 1, c_rw), lp["rw_a_up"],
      lp["rw_g_up"].astype(BF16), row(lp["rw_k_k"]), row(lp["rw_k_a"]), row(lp["rw_r_k"]),
      _head_ones(pack, head))


def _rw_scan_kernel(rf_ref, kkf_ref, vf_ref, lwf_ref, kdf_ref, betaf_ref, rb_ref, kkb_ref, vb_ref, lwb_ref, kdb_ref,
                    betab_ref, yf_ref, yb_ref, st_scr, *, head):
    t_len = rf_ref.shape[0]
    n_pack = RW_PACK
    width = n_pack * head
    n_groups = rf_ref.shape[1] // width

    @pl.when(pl.program_id(1) == 0)
    def _():
        st_scr[...] = jnp.zeros_like(st_scr)

    lane_head = lax.broadcasted_iota(jnp.int32, (1, width), 1) // head
    head_masks = [lane_head == h for h in range(n_pack)]

    def block_diag(x):
        xb = x.astype(BF16)
        zero = jnp.zeros_like(xb)
        return jnp.concatenate([jnp.where(m, xb, zero) for m in head_masks], axis=0)

    def mm(a, bd):
        return _dot(a.astype(BF16), bd)

    def mm_nt(a, bd):
        return lax.dot_general(a.astype(BF16), bd, (((1,), (1,)), ((), ())), preferred_element_type=F32)

    tt = lax.broadcasted_iota(jnp.int32, (t_len, width), 0)
    ss = lax.broadcasted_iota(jnp.int32, (t_len, width), 1) % head
    eye = jnp.where(tt == ss, 1.0, 0.0)
    ti = lax.broadcasted_iota(jnp.int32, (t_len, t_len), 0)
    si = lax.broadcasted_iota(jnp.int32, (t_len, t_len), 1)
    strict_d = [tt > ss, tt < ss]
    incl_d = [tt >= ss, tt <= ss]
    tri_d = [jnp.where(ti >= si, 1.0, 0.0).astype(BF16), jnp.where(ti <= si, 1.0, 0.0).astype(BF16)]

    units = [(d, slice(g * width, (g + 1) * width)) for d in range(2) for g in range(n_groups)]
    each = lambda f, *xs: [f(*a) for a in zip(*xs)]
    pick = lambda fwd_ref, bwd_ref: [(bwd_ref if d else fwd_ref)[:, c] for d, c in units]
    r = pick(rf_ref, rb_ref)
    kk = pick(kkf_ref, kkb_ref)
    v = pick(vf_ref, vb_ref)
    lw = pick(lwf_ref, lwb_ref)
    kd = pick(kdf_ref, kdb_ref)
    beta = pick(betaf_ref, betab_ref)
    strict = [strict_d[d] for d, _ in units]
    incl = [incl_d[d] for d, _ in units]
    tri = [tri_d[d] for d, _ in units]

    def cumulative(x, tri_u):
        hi, lo = _split(x)
        return _dot(tri_u, hi) + _dot(tri_u, lo)

    log_p = each(cumulative, lw, tri)
    log_pt = each(lambda x: jnp.sum(x, axis=0, keepdims=True), lw)
    p_inv = each(lambda x: jnp.exp(-x), log_p)
    a_bar = each(lambda a, p, w: a * jnp.exp(p - w), kk, log_p, lw)
    r_bar = each(lambda a, p: a * jnp.exp(p), r, log_p)
    b_til = each(jnp.multiply, beta, p_inv)
    k_til = each(jnp.multiply, kd, p_inv)
    p_rest = each(lambda t, p: jnp.exp(t - p), log_pt, log_p)
    b_end = each(jnp.multiply, beta, p_rest)
    k_end = each(jnp.multiply, kd, p_rest)

    ar = each(lambda a, b: jnp.concatenate([a, b], axis=0), a_bar, r_bar)
    gram_b = each(lambda a, b: mm_nt(a, block_diag(b)), ar, b_til)
    gram_k = each(lambda a, b: mm_nt(a, block_diag(b)), ar, k_til)
    l_b = each(lambda g, m: jnp.where(m, g[:t_len], 0.0), gram_b, strict)
    m_b = each(lambda g, m: jnp.where(m, g[t_len:], 0.0), gram_b, incl)
    l_k = each(lambda g, m: jnp.where(m, g[:t_len], 0.0), gram_k, strict)
    m_k = each(lambda g, m: jnp.where(m, g[t_len:], 0.0), gram_k, incl)

    stack = lambda a, b: jnp.concatenate([a, b], axis=0)
    t_inv = each(lambda x: eye - x, l_b)
    l_pow = each(lambda x: mm(x, block_diag(x)), l_b)
    span = 4
    while span < t_len:
        both = each(lambda t, x: mm(stack(t, x), block_diag(x)), t_inv, l_pow)
        t_inv = each(lambda t, p: t + p[:t_len], t_inv, both)
        l_pow = each(lambda p: p[t_len:], both)
        span *= 2
    t_inv = each(lambda t, x: t + mm(t, block_diag(x)), t_inv, l_pow)

    def head_t(x):
        return jnp.concatenate([x[:, h * head:(h + 1) * head].T for h in range(n_pack)], axis=1)

    bd_v = each(block_diag, v)
    w_til = each(lambda t, a: mm(t, block_diag(a)), t_inv, a_bar)
    lmv = each(lambda lk, mk, ke, bv: mm(jnp.concatenate([lk, mk, head_t(ke)], axis=0), bv), l_k, m_k, k_end, bd_v)
    u_til = each(lambda t, a: mm(t, block_diag(a[:t_len])), t_inv, lmv)
    mb_bt = each(lambda m, b: stack(m, head_t(b)), m_b, b_end)
    from_w = each(lambda a, w: mm(a, block_diag(w)), mb_bt, w_til)
    from_u = each(lambda a, u: mm(a, block_diag(u)), mb_bt, u_til)
    q_hat = each(lambda a, fw: a - fw[:t_len], r_bar, from_w)
    y_hat = each(lambda a, fu: a[t_len:2 * t_len] - fu[:t_len], lmv, from_u)
    g_mat = each(lambda t, fw: eye * jnp.exp(t) - fw[t_len:], log_pt, from_w)
    h_mat = each(lambda a, fu: a[2 * t_len:] - fu[t_len:], lmv, from_u)

    bd_s = [block_diag(st_scr[d, :, c]) for d, c in units]
    from_state = each(lambda q, g, s: mm(stack(q, g), s), q_hat, g_mat, bd_s)
    for (d, c), fs, y0, h in zip(units, from_state, y_hat, h_mat):
        (yb_ref if d else yf_ref)[:, c] = (fs[:t_len] + y0).astype(yf_ref.dtype)
        st_scr[d, :, c] = fs[t_len:] + h


def _rw_scan(r, kk, v, lw, kd, beta, *, bsz, l_ctx, l_lat, head):
    n, c_rw = r.shape
    t_len = RW_CHUNK
    nc = l_ctx // t_len
    nl = l_lat // t_len
    ctx_blocks = bsz * nc

    def blk(d, b, c):
        cc = nc - 1 - c if d else c
        cl = nl - 1 - (c - nc) if d else c - nc
        return jnp.where(c < nc, b * nc + cc, ctx_blocks + b * nl + cl)

    shared = lambda d: pl.BlockSpec((t_len, c_rw), lambda b, c: (blk(d, b, c), 0))
    per_dir = lambda d: pl.BlockSpec((None, t_len, c_rw), lambda b, c: (d, blk(d, b, c), 0))
    side = lambda d: [shared(d), shared(d), shared(d), per_dir(d), per_dir(d), per_dir(d)]
    kern = functools.partial(_rw_scan_kernel, head=head)
    sds = jax.ShapeDtypeStruct
    return pl.pallas_call(
        kern,
        grid=(bsz, nc + nl),
        in_specs=side(0) + side(1),
        out_specs=[shared(0), shared(1)],
        out_shape=[sds((n, c_rw), BF16), sds((n, c_rw), BF16)],
        scratch_shapes=[pltpu.VMEM((2, head, c_rw), F32)],
        compiler_params=_params(("arbitrary", "arbitrary")),
        name="rwkv_scan",
    )(r, kk, v, lw, kd, beta, r, kk, v, lw, kd, beta)


def _rw_readout_kernel(yf_ref, yb_ref, g_ref, bonus_ref, lnw_ref, lnb_ref, ones_ref, o_ref, *, head):
    ones = ones_ref[...]
    pack = ones.shape[0]
    inv = 1.0 / head
    for s in range(o_ref.shape[1] // pack):
        cs = slice(s * pack, (s + 1) * pack)
        y = yf_ref[:, cs].astype(F32) + yb_ref[:, cs].astype(F32)
        mu = _dot_exact_rhs(y, ones) * inv
        yc = y - mu
        var = _dot_exact_rhs(yc * yc, ones) * inv
        yn = yc * lax.rsqrt(var + GN_EPS) * lnw_ref[:, cs] + lnb_ref[:, cs]
        o_ref[:, cs] = ((yn + bonus_ref[:, cs]) * g_ref[:, cs]).astype(o_ref.dtype)


def _rw_readout(y_fwd, y_bwd, g, bonus, ln_w, ln_b, *, head, tm):
    n, c_rw = y_fwd.shape
    pack = RW_PACK * head
    tn = _pick(c_rw, RW_PREP_WIDTH)
    tile = pl.BlockSpec((tm, tn), lambda i, j: (i, j))
    vec = pl.BlockSpec((1, tn), lambda i, j: (0, j))
    kern = functools.partial(_rw_readout_kernel, head=head)
    return pl.pallas_call(
        kern,
        grid=(n // tm, c_rw // tn),
        in_specs=[tile, tile, tile, tile, vec, vec, pl.BlockSpec((pack, pack), lambda i, j: (0, 0))],
        out_specs=tile,
        out_shape=jax.ShapeDtypeStruct((n, c_rw), BF16),
        compiler_params=_params(("arbitrary", "arbitrary")),
        name="rwkv_readout",
    )(y_fwd, y_bwd, g, bonus, ln_w.reshape(1, c_rw), ln_b.reshape(1, c_rw), _head_ones(pack, head))


def _final_norm_kernel(h_ref, gain_ref, o_ref):
    x = h_ref[...]
    ms = jnp.mean(x * x, axis=-1, keepdims=True)
    o_ref[...] = x * lax.rsqrt(ms + NORM_EPS) * gain_ref[...]


def _final_norm(h, gain, *, tm, row0):
    n, d = h.shape
    b0 = row0 // tm
    return pl.pallas_call(
        _final_norm_kernel,
        grid=((n - row0) // tm,),
        in_specs=[pl.BlockSpec((tm, d), lambda i: (i + b0, 0)), pl.BlockSpec((1, d), lambda i: (0, 0))],
        out_specs=pl.BlockSpec((tm, d), lambda i: (i, 0)),
        out_shape=jax.ShapeDtypeStruct((n - row0, d), F32),
        compiler_params=_params(("arbitrary",)),
        name="final_norm",
    )(h, gain.reshape(1, d))


def _to_s5_layout(u, bsz, l_ctx, l_lat, sel):
    t = S5_CHUNK
    n_ctx = bsz * l_ctx
    nj = u.shape[0]

    def tiles(x, length):
        x = x.reshape(nj, bsz, length // t, t * 256)
        return jnp.transpose(x, (0, 2, 1, 3)).reshape(nj, (length // t) * bsz, t * 256)

    return _lane_perm([tiles(u[:, :n_ctx], l_ctx), tiles(u[:, n_ctx:], l_lat)], sel)


def _from_s5_layout(y, bsz, l_ctx, l_lat, sel):
    t = S5_CHUNK
    nj, rows, _ = y.shape
    out = _lane_perm([y], sel).reshape(nj, t, rows, 256)
    rows_ctx = (l_ctx // t) * bsz

    def untile(o, length):
        o = o.reshape(nj, t, length // t, bsz, 256)
        return jnp.transpose(o, (3, 2, 1, 0, 4)).reshape(bsz * length, nj * 256)

    return untile(out[:, :, :rows_ctx], l_ctx), untile(out[:, :, rows_ctx:], l_lat)


def kernel(x, c, ctx, c_ctx, ada_down, ada_up, ada_b, norm1, w_in, s5_a_re, s5_a_im, s5_log_dt, s5_b_re, s5_b_im, s5_c_re, s5_c_im, s5_d, s5_glu_w, s5_glu_b, rw_mu, rw_w0, rw_w_up, rw_a0, rw_a_up, rw_g_up, rw_k_k, rw_k_a, rw_r_k, rw_ln_w, rw_ln_b, w_proj_a, w_proj_b, gate_up, gate_b, w_out, norm2, mlp_w1, mlp_w2, norm_f):
    bsz, l_lat, d = x.shape
    l_ctx = ctx.shape[1]
    depth = w_in.shape[0]
    n_ctx = bsz * l_ctx
    n_lat = bsz * l_lat
    s5_groups, s5_state = s5_a_re.shape[2], s5_a_re.shape[3]
    s5_gch = s5_b_re.shape[-1]
    c_s5 = s5_groups * s5_gch
    rw_heads, head = rw_r_k.shape[1], rw_r_k.shape[2]
    c_rw = rw_heads * head
    lora_w, lora_a, lora_g = rw_w_up.shape[2], rw_a_up.shape[2], rw_g_up.shape[1]
    gate_rank = gate_up.shape[2]
    n_small_rw = 2 * lora_w + 2 * lora_a + lora_g
    c_big = c_s5 + 3 * c_rw
    assert w_in.shape[2] == c_big + n_small_rw + gate_rank
    assert bsz % 8 == 0 and l_ctx % RW_CHUNK == 0 and l_lat % RW_CHUNK == 0 and l_lat % GRID_W == 0
    assert s5_gch * S5_CHUNK == 256 and RW_PACK * head == 256 and c_rw % 256 == 0 and gate_rank % 128 == 0
    assert RW_CHUNK == head
    assert c_s5 % IN_PROJ_TN == 0 and (3 * c_rw) % IN_PROJ_TN == 0 and n_small_rw + gate_rank <= IN_PROJ_TN

    tm = _pick(l_lat, _pick(n_ctx, TM))

    def row_of_block(i):
        ctx_blocks = n_ctx // tm
        return jnp.where(i < ctx_blocks, bsz, (i - ctx_blocks) // (l_lat // tm))

    n_cond = ((bsz + 1 + 7) // 8) * 8
    cond = jnp.zeros((n_cond, d), F32).at[:bsz].set(c).at[bsz].set(c_ctx)
    mods_all = _ada_modulation(cond, ada_down, ada_up, ada_b)
    mods_all = jnp.transpose(mods_all.reshape(depth, n_cond, 6, 1, d), (0, 2, 1, 3, 4))

    h = jnp.concatenate([ctx.reshape(n_ctx, d), x.reshape(n_lat, d)], axis=0)

    lane_sel = _lane_select_matrices(s5_gch)
    w_in_bf = _to_bf16(w_in)
    glu_bf = _to_bf16(s5_glu_w)
    proj_a_bf, proj_b_bf, w_out_bf = _to_bf16(w_proj_a), _to_bf16(w_proj_b), _to_bf16(w_out)
    gate_up_bf = _to_bf16(gate_up.reshape(depth, 2 * gate_rank, d)).reshape(gate_up.shape)
    w1_bf, w2_bf = _to_bf16(mlp_w1), _to_bf16(mlp_w2)

    for l in range(depth):
        last = l == depth - 1
        row0 = n_ctx if last else 0
        mods = mods_all[l]
        lp = {"rw_mu": rw_mu[l], "rw_w0": rw_w0[l], "rw_w_up": rw_w_up[l], "rw_a0": rw_a0[l],
              "rw_a_up": rw_a_up[l], "rw_g_up": rw_g_up[l], "rw_k_k": rw_k_k[l], "rw_k_a": rw_k_a[l],
              "rw_r_k": rw_r_k[l]}
        w_all = jnp.concatenate([w_in_bf[l, :, :c_big], w_in_bf[l, :, c_big + n_small_rw:],
                                 w_in_bf[l, :, c_big:c_big + n_small_rw],
                                 jnp.zeros((d, IN_PROJ_TN - n_small_rw - gate_rank), BF16)], axis=1)
        z_u, z_rkv, z_small = _in_proj(h, norm1[l], mods, w_all, row_of_block, tm=tm, tn=IN_PROJ_TN,
                                       c_u=c_s5, c_rkv=3 * c_rw)

        kf, kb, bm, cmt, lam = _s5_matrices(s5_a_re[l], s5_a_im[l], s5_log_dt[l], s5_b_re[l], s5_b_im[l],
                                            s5_c_re[l], s5_c_im[l], s5_d[l])
        u_g = _to_s5_layout(z_u, bsz, l_ctx, l_lat, lane_sel)
        y_g = _s5_scan(u_g, kf, kb, bm, cmt, lam, n_ctx=l_ctx // S5_CHUNK, n_chunks=(l_ctx + l_lat) // S5_CHUNK, bsz=bsz)
        y_ctx, y_lat = _from_s5_layout(y_g, bsz, l_ctx, l_lat, lane_sel)
        ya = _s5_glu(y_ctx, y_lat, glu_bf, l, s5_glu_b[l], tm=tm)

        r, kk, v, lw, kd, beta, g, bonus = _rw_prep(z_rkv, z_small, lp, bsz=bsz, l_ctx=l_ctx, l_lat=l_lat,
                                                    c_rw=c_rw, head=head, gate_rank=gate_rank)
        y_fwd, y_bwd = _rw_scan(r, kk, v, lw, kd, beta, bsz=bsz, l_ctx=l_ctx, l_lat=l_lat, head=head)
        yb = _rw_readout(y_fwd, y_bwd, g, bonus, rw_ln_w[l], rw_ln_b[l], head=head, tm=tm)

        h = _merge_out(ya, yb, z_small, proj_a_bf, proj_b_bf, gate_up_bf, gate_b[l], w_out_bf, l, h, mods,
                       row_of_block, tm=tm, row0=row0)
        h = _mlp(h, norm2[l], mods, w1_bf, w2_bf, l, row_of_block, tm=tm, row0=row0)

    out = _final_norm(h, norm_f, tm=tm, row0=n_ctx)
    return out.reshape(bsz, l_lat, d)
```

```python
import functools

import jax
import jax.numpy as jnp
from jax import lax
from jax.experimental import pallas as pl
from jax.experimental.pallas import tpu as pltpu

F32 = jnp.float32
BF16 = jnp.bfloat16

GRID_W = 64
NORM_EPS = 1e-6
GN_EPS = 64e-5
A_RE_MAX = -1e-4
S5_CHUNK = 16
RW_CHUNK = 64
RW_PACK = 4
RW_PREP_WIDTH = 1024
RW_PREP_ROWS = 512
TM = 512
IN_PROJ_TN = 1024
VMEM_LIMIT = 56 * 1024 * 1024


def _params(sem):
    return pltpu.CompilerParams(dimension_semantics=sem, vmem_limit_bytes=VMEM_LIMIT)


def _pick(n, pref):
    t = pref
    while n % t:
        t //= 2
    return t


def _dot(a, b):
    return jnp.dot(a, b, preferred_element_type=F32)


def _split(x):
    hi = x.astype(BF16)
    lo = (x - hi.astype(F32)).astype(BF16)
    return hi, lo


def _dot3(a, b):
    ah, al = _split(a)
    bh, bl = _split(b)
    return _dot(ah, bh) + (_dot(ah, bl) + _dot(al, bh))


def _dot_exact_rhs(a, b_bf16):
    ah, al = _split(a)
    return _dot(ah, b_bf16) + _dot(al, b_bf16)


def _sigmoid(x):
    return 1.0 / (1.0 + jnp.exp(-x))


def _gelu_tanh(x):
    c = 0.7978845608028654
    return 0.5 * x * (1.0 + jnp.tanh(c * (x + 0.044715 * (x * x * x))))


def _cast_kernel(w_ref, o_ref):
    o_ref[...] = w_ref[...].astype(o_ref.dtype)


def _to_bf16(w):
    nl, r, c = w.shape
    rows_4mib = max(16, (1 << 20) // c)
    tr = _pick(r, 1 << (rows_4mib.bit_length() - 1))
    out = pl.pallas_call(
        _cast_kernel,
        grid=(nl * r // tr,),
        in_specs=[pl.BlockSpec((tr, c), lambda i: (i, 0))],
        out_specs=pl.BlockSpec((tr, c), lambda i: (i, 0)),
        out_shape=jax.ShapeDtypeStruct((nl * r, c), BF16),
        compiler_params=_params(("arbitrary",)),
        name="weights_to_bf16",
    )(w.reshape(nl * r, c))
    return out.reshape(nl, r, c)


def _pack_w_in_kernel(w_ref, o_ref, *, c_big, n_small, rank):
    o_ref[:, :c_big] = w_ref[:, :c_big].astype(BF16)
    o_ref[:, c_big:c_big + rank] = w_ref[:, c_big + n_small:].astype(BF16)
    o_ref[:, c_big + rank:c_big + rank + n_small] = w_ref[:, c_big:c_big + n_small].astype(BF16)
    pad = o_ref.shape[1] - (c_big + rank + n_small)
    if pad:
        o_ref[:, c_big + rank + n_small:] = jnp.zeros((o_ref.shape[0], pad), BF16)


def _pack_w_in(w_in, *, c_big, n_small, rank, tn):
    nl, d, c_in = w_in.shape
    c_out = c_big + tn
    assert c_big % 128 == 0 and n_small % 128 == 0 and rank % 128 == 0 and rank + n_small <= tn
    tr = _pick(d, 64)
    out = pl.pallas_call(
        functools.partial(_pack_w_in_kernel, c_big=c_big, n_small=n_small, rank=rank),
        grid=(nl * d // tr,),
        in_specs=[pl.BlockSpec((tr, c_in), lambda i: (i, 0))],
        out_specs=pl.BlockSpec((tr, c_out), lambda i: (i, 0)),
        out_shape=jax.ShapeDtypeStruct((nl * d, c_out), BF16),
        compiler_params=_params(("arbitrary",)),
        name="pack_w_in",
    )(w_in.reshape(nl * d, c_in))
    return out.reshape(nl, d, c_out)


def _ada_kernel(cond_ref, wd_ref, wu_ref, b_ref, o_ref, mid_scr):
    @pl.when(pl.program_id(1) == 0)
    def _():
        c = cond_ref[...]
        mid_scr[...] = _dot3(c * _sigmoid(c), wd_ref[...])

    o_ref[...] = _dot3(mid_scr[...], wu_ref[...]) + b_ref[...]


def _ada_modulation(cond, w_down, w_up, bias):
    nl, d, rank = w_down.shape
    r = cond.shape[0]
    n_out = w_up.shape[-1]
    tn = _pick(n_out, 2048)
    return pl.pallas_call(
        _ada_kernel,
        grid=(nl, n_out // tn),
        in_specs=[
            pl.BlockSpec((r, d), lambda l, j: (0, 0)),
            pl.BlockSpec((None, d, rank), lambda l, j: (l, 0, 0)),
            pl.BlockSpec((None, rank, tn), lambda l, j: (l, 0, j)),
            pl.BlockSpec((None, 1, tn), lambda l, j: (l, 0, j)),
        ],
        out_specs=pl.BlockSpec((None, r, tn), lambda l, j: (l, 0, j)),
        out_shape=jax.ShapeDtypeStruct((nl, r, n_out), F32),
        scratch_shapes=[pltpu.VMEM((r, rank), F32)],
        compiler_params=_params(("arbitrary", "arbitrary")),
        name="ada_modulation",
    )(cond, w_down, w_up, bias.reshape(nl, 1, n_out))


def _in_proj_kernel(h_ref, gain_ref, shift_ref, scale_ref, w_ref, zu_ref, zrkv_ref, zs_ref, n_scr, *, ju, jr):
    j = pl.program_id(1)

    @pl.when(j == 0)
    def _():
        x = h_ref[...]
        ms = jnp.mean(x * x, axis=-1, keepdims=True)
        xn = x * lax.rsqrt(ms + NORM_EPS) * gain_ref[...]
        n_scr[...] = (xn * (1.0 + scale_ref[...]) + shift_ref[...]).astype(BF16)

    acc = _dot(n_scr[...], w_ref[...])

    @pl.when(j < ju)
    def _():
        for t in range(zu_ref.shape[0]):
            zu_ref[t] = acc[:, t * 256:(t + 1) * 256].astype(zu_ref.dtype)

    @pl.when(jnp.logical_and(j >= ju, j < ju + jr))
    def _():
        zrkv_ref[...] = acc.astype(zrkv_ref.dtype)

    @pl.when(j == ju + jr)
    def _():
        zs_ref[...] = acc


def _in_proj(h, gain, mods, w_all, layer, row_of_block, *, tm, tn, c_u, c_rkv):
    n, d = h.shape
    ju, jr = c_u // tn, c_rkv // tn
    assert w_all.shape[2] == (ju + jr + 1) * tn
    mod = lambda which: pl.BlockSpec((None, None, 1, d), lambda i, j: (which, row_of_block(i), 0, 0))
    sds = jax.ShapeDtypeStruct
    return pl.pallas_call(
        functools.partial(_in_proj_kernel, ju=ju, jr=jr),
        grid=(n // tm, ju + jr + 1),
        in_specs=[
            pl.BlockSpec((tm, d), lambda i, j: (i, 0), pipeline_mode=pl.Buffered(1)),
            pl.BlockSpec((1, d), lambda i, j: (0, 0)),
            mod(0), mod(1),
            pl.BlockSpec((None, d, tn), lambda i, j: (layer, 0, j)),
        ],
        out_specs=[pl.BlockSpec((tn // 256, tm, 256), lambda i, j: (jnp.minimum(j, ju - 1), i, 0)),
                   pl.BlockSpec((tm, tn), lambda i, j: (i, jnp.clip(j - ju, 0, jr - 1))),
                   pl.BlockSpec((tm, tn), lambda i, j: (i, 0))],
        out_shape=[sds((c_u // 256, n, 256), BF16), sds((n, c_rkv), BF16), sds((n, tn), F32)],
        scratch_shapes=[pltpu.VMEM((tm, d), BF16)],
        compiler_params=_params(("arbitrary", "arbitrary")),
        name="in_proj",
    )(h, gain.reshape(1, d), mods, mods, w_all)


def _mlp_kernel(h_ref, gain_ref, shift_ref, scale_ref, gate_ref, w1_ref, w2_ref, o_ref, n_scr):
    j = pl.program_id(1)

    @pl.when(j == 0)
    def _():
        x = h_ref[...]
        ms = jnp.mean(x * x, axis=-1, keepdims=True)
        xn = x * lax.rsqrt(ms + NORM_EPS) * gain_ref[...]
        n_scr[...] = (xn * (1.0 + scale_ref[...]) + shift_ref[...]).astype(BF16)
        o_ref[...] = jnp.zeros_like(o_ref)

    hid = jnp.square(jnp.maximum(_dot(n_scr[...], w1_ref[...]), 0.0)).astype(BF16)
    o_ref[...] += _dot(hid, w2_ref[...])

    @pl.when(j == pl.num_programs(1) - 1)
    def _():
        o_ref[...] = h_ref[...] + gate_ref[...] * o_ref[...]


def _mlp(h, gain, mods, w1, w2, layer, row_of_block, *, tm, row0):
    n, d = h.shape
    d_ff = w1.shape[2]
    tf = _pick(d_ff, 512)
    b0 = row0 // tm
    mod = lambda which: pl.BlockSpec((None, None, 1, d), lambda i, j: (which, row_of_block(i + b0), 0, 0))
    return pl.pallas_call(
        _mlp_kernel,
        grid=((n - row0) // tm, d_ff // tf),
        in_specs=[
            pl.BlockSpec((tm, d), lambda i, j: (i + b0, 0), pipeline_mode=pl.Buffered(1)),
            pl.BlockSpec((1, d), lambda i, j: (0, 0)),
            mod(3), mod(4), mod(5),
            pl.BlockSpec((None, d, tf), lambda i, j: (layer, 0, j)),
            pl.BlockSpec((None, tf, d), lambda i, j: (layer, j, 0)),
        ],
        out_specs=pl.BlockSpec((tm, d), lambda i, j: (i + b0, 0)),
        out_shape=jax.ShapeDtypeStruct((n, d), F32),
        scratch_shapes=[pltpu.VMEM((tm, d), BF16)],
        input_output_aliases={0: 0},
        compiler_params=_params(("arbitrary", "arbitrary")),
        name="mlp",
    )(h, gain.reshape(1, d), mods, mods, mods, w1, w2)


def _merge_out_kernel(ya_ref, yb_ref, gz_ref, wa_ref, wb_ref, ga_ref, gb_ref, ba_ref, bb_ref, wo_ref, h_ref,
                      gate_ref, o_ref):
    j = pl.program_id(1)

    @pl.when(j == 0)
    def _():
        o_ref[...] = jnp.zeros_like(o_ref)

    gz = gz_ref[...].astype(BF16)
    ga = _sigmoid(_dot(gz, ga_ref[...]) + ba_ref[...])
    gb = _sigmoid(_dot(gz, gb_ref[...]) + bb_ref[...])
    mixed = (ga * _dot(ya_ref[...], wa_ref[...]) + gb * _dot(yb_ref[...], wb_ref[...])).astype(BF16)
    o_ref[...] += _dot(mixed, wo_ref[...])

    @pl.when(j == pl.num_programs(1) - 1)
    def _():
        o_ref[...] = h_ref[...] + gate_ref[...] * o_ref[...]


def _merge_out(ya, yb, z_small, wa, wb, gate_up, gate_b, w_out, layer, h, mods, row_of_block, *, tm, row0):
    n, d = h.shape
    ca, cb = ya.shape[1], yb.shape[1]
    d_mid = wa.shape[2]
    rank = gate_up.shape[2]
    tn = _pick(d_mid, 512)
    b0 = row0 // tm
    once = pl.Buffered(1)
    return pl.pallas_call(
        _merge_out_kernel,
        grid=((n - row0) // tm, d_mid // tn),
        in_specs=[
            pl.BlockSpec((tm, ca), lambda i, j: (i + b0, 0), pipeline_mode=once),
            pl.BlockSpec((tm, cb), lambda i, j: (i + b0, 0), pipeline_mode=once),
            pl.BlockSpec((tm, rank), lambda i, j: (i + b0, 0)),
            pl.BlockSpec((None, ca, tn), lambda i, j: (layer, 0, j)),
            pl.BlockSpec((None, cb, tn), lambda i, j: (layer, 0, j)),
            pl.BlockSpec((None, None, rank, tn), lambda i, j: (layer, 0, 0, j)),
            pl.BlockSpec((None, None, rank, tn), lambda i, j: (layer, 1, 0, j)),
            pl.BlockSpec((None, 1, tn), lambda i, j: (0, 0, j)),
            pl.BlockSpec((None, 1, tn), lambda i, j: (1, 0, j)),
            pl.BlockSpec((None, tn, d), lambda i, j: (layer, j, 0)),
            pl.BlockSpec((tm, d), lambda i, j: (i + b0, 0), pipeline_mode=once),
            pl.BlockSpec((None, None, 1, d), lambda i, j: (2, row_of_block(i + b0), 0, 0)),
        ],
        out_specs=pl.BlockSpec((tm, d), lambda i, j: (i + b0, 0)),
        out_shape=jax.ShapeDtypeStruct((n, d), F32),
        input_output_aliases={10: 0},
        compiler_params=_params(("arbitrary", "arbitrary")),
        name="merge_out_proj",
    )(ya, yb, z_small, wa, wb, gate_up, gate_up, gate_b.reshape(2, 1, d_mid), gate_b.reshape(2, 1, d_mid),
      w_out, h, mods)


def _s5_matrices(a_re, a_im, log_dt, b_re, b_im, c_re, c_im, d_skip):
    t_len = S5_CHUNK
    hp = lax.Precision.HIGHEST
    lam_re = jnp.minimum(a_re.astype(F32), A_RE_MAX)
    lam_im = a_im.astype(F32)
    dt = jnp.exp(log_dt.astype(F32))[..., None]
    j = jnp.arange(t_len + 1, dtype=F32)[:, None, None, None]
    mag = jnp.exp(j * (lam_re * dt))
    ang = j * (lam_im * dt)
    pw_re = mag * jnp.cos(ang)
    pw_im = mag * jnp.sin(ang)
    nr, ni = pw_re[1] - 1.0, pw_im[1]
    den = lam_re * lam_re + lam_im * lam_im
    qr = (nr * lam_re + ni * lam_im) / den
    qi = (ni * lam_re - nr * lam_im) / den
    bbr = qr[..., None] * b_re - qi[..., None] * b_im
    bbi = qr[..., None] * b_im + qi[..., None] * b_re
    n_dir, g, p, c = bbr.shape
    clr = c_re[None] * pw_re[:t_len, :, :, None, :] - c_im[None] * pw_im[:t_len, :, :, None, :]
    cli = c_re[None] * pw_im[:t_len, :, :, None, :] + c_im[None] * pw_re[:t_len, :, :, None, :]
    kj = (jnp.einsum("jdgop,dgpi->dgijo", clr, bbr, precision=hp)
          - jnp.einsum("jdgop,dgpi->dgijo", cli, bbi, precision=hp))
    lag0 = (jnp.arange(t_len) == 0).astype(F32)
    skip = d_skip.astype(F32).reshape(g, c)
    kf = kj[0] + skip[:, :, None, None] * lag0[None, None, :, None] * jnp.eye(c, dtype=F32)[None, :, None, :]
    kf = kf.reshape(g, c, t_len * c)
    kb = jnp.flip(kj[1], axis=2).reshape(g, c, t_len * c)
    bbr_t = jnp.transpose(bbr, (0, 1, 3, 2))
    bbi_t = jnp.transpose(bbi, (0, 1, 3, 2))

    def powers(d, e):
        return jnp.transpose(pw_re[e, d], (1, 0, 2)), jnp.transpose(pw_im[e, d], (1, 0, 2))

    lanes4 = lambda a, b, cc, dd: jnp.concatenate([a, b, cc, dd], axis=-1)

    pf_r, pf_i = powers(0, (t_len - 1) - jnp.arange(t_len))
    pb_r, pb_i = powers(1, jnp.arange(t_len))
    bm = (lanes4(pf_r, pb_r, pf_r, pb_r)[:, :, None, :] * lanes4(bbr_t[0], bbr_t[1], bbi_t[0], bbi_t[1])[:, None]
          + lanes4(-pf_i, -pb_i, pf_i, pb_i)[:, :, None, :] * lanes4(bbi_t[0], bbi_t[1], bbr_t[0], bbr_t[1])[:, None])
    bm = bm.reshape(g, t_len * c, 4 * p)

    qf_r, qf_i = powers(0, jnp.arange(t_len) + 1)
    qb_r, qb_i = powers(1, t_len - jnp.arange(t_len))
    cmt = (lanes4(c_re[0], c_re[1], c_re[0], c_re[1])[:, None] * lanes4(qf_r, qb_r, -qf_i, -qb_i)[:, :, None, :]
           + lanes4(c_im[0], c_im[1], c_im[0], c_im[1])[:, None] * lanes4(-qf_i, -qb_i, -qf_r, -qb_r)[:, :, None, :])
    cmt = cmt.reshape(g, t_len * c, 4 * p)
    lam = jnp.stack([jnp.concatenate([pw_re[t_len, 0], pw_re[t_len, 1]], axis=-1),
                     jnp.concatenate([pw_im[t_len, 0], pw_im[t_len, 1]], axis=-1)], axis=1)
    return kf, kb, bm.astype(BF16), cmt.astype(BF16), lam


def _s5_kernel(u_ref, bm_ref, kf_ref, kb_ref, cmt_ref, lam_ref, y_ref, xin_scr, fwd_scr, bwd_scr, km_scr, *,
               n_ctx, n_chunks, bsz, p_state):
    u = u_ref[...]
    xin_scr[...] = _dot(u, bm_ref[...])
    gch, tc = kf_ref.shape
    kf = kf_ref[...]
    kb = kb_ref[...]
    lane_k = lax.broadcasted_iota(jnp.int32, (gch, tc), 1)
    for s in range(tc // gch):
        fwd = kf if s == 0 else jnp.where(lane_k >= gch * s, pltpu.roll(kf, gch * s, axis=1), 0.0)
        shift = (gch * (s + 1)) % tc
        bwd = kb if shift == 0 else jnp.where(lane_k < gch * (s + 1), pltpu.roll(kb, shift, axis=1), 0.0)
        km_scr[s * gch:(s + 1) * gch, :] = (fwd + bwd).astype(BF16)
    two_p = 2 * p_state
    lam_r = jnp.broadcast_to(lam_ref[0:1, :], (bsz, two_p))
    lam_i = jnp.broadcast_to(lam_ref[1:2, :], (bsz, two_p))
    is_fwd = lax.broadcasted_iota(jnp.int32, (bsz, two_p), 1) < p_state

    def body(i, carry):
        re, im = carry
        pb = jnp.where(i < n_ctx, n_ctx - 1 - i, n_chunks + n_ctx - 1 - i)
        rf = pl.multiple_of(i * bsz, bsz)
        rb = pl.multiple_of(pb * bsz, bsz)
        xf = xin_scr[pl.ds(rf, bsz), :]
        xb = xin_scr[pl.ds(rb, bsz), :]
        x_re = jnp.where(is_fwd, xf[:, :two_p], xb[:, :two_p])
        x_im = jnp.where(is_fwd, xf[:, two_p:], xb[:, two_p:])
        state = jnp.concatenate([re, im], axis=1)
        fwd_scr[pl.ds(rf, bsz), :] = state
        bwd_scr[pl.ds(rb, bsz), :] = state
        return (re * lam_r - im * lam_i + x_re, re * lam_i + im * lam_r + x_im)

    zero = jnp.zeros((bsz, two_p), F32)
    lax.fori_loop(0, n_chunks, body, (zero, zero))
    lane = lax.broadcasted_iota(jnp.int32, (1, 2 * two_p), 1)
    fwd_lane = (lane % two_p) < p_state
    xs = jnp.where(fwd_lane, fwd_scr[...], bwd_scr[...]).astype(BF16)
    from_state = lax.dot_general(xs, cmt_ref[...], (((1,), (1,)), ((), ())), preferred_element_type=F32)
    y_ref[...] = (_dot(u, km_scr[...]) + from_state).astype(y_ref.dtype)


def _s5_scan(u_g, kf, kb, bm, cmt, lam, *, n_ctx, n_chunks, bsz):
    g, rows, tc = u_g.shape
    gch = kf.shape[1]
    m = 256 // gch
    p4 = bm.shape[-1]
    kern = functools.partial(_s5_kernel, n_ctx=n_ctx, n_chunks=n_chunks, bsz=bsz, p_state=p4 // 4)
    return pl.pallas_call(
        kern,
        grid=(g,),
        in_specs=[
            pl.BlockSpec((None, rows, tc), lambda i: (i, 0, 0)),
            pl.BlockSpec((None, tc, p4), lambda i: (i, 0, 0)),
            pl.BlockSpec((None, gch, tc), lambda i: (i, 0, 0)),
            pl.BlockSpec((None, gch, tc), lambda i: (i, 0, 0)),
            pl.BlockSpec((None, tc, p4), lambda i: (i, 0, 0)),
            pl.BlockSpec((None, 2, p4 // 2), lambda i: (i, 0, 0)),
        ],
        out_specs=pl.BlockSpec((None, rows, tc), lambda i: (i // m, 0, i % m)),
        out_shape=jax.ShapeDtypeStruct((g // m, rows, m * tc), BF16),
        scratch_shapes=[pltpu.VMEM((rows, p4), F32), pltpu.VMEM((rows, p4), F32), pltpu.VMEM((rows, p4), F32),
                        pltpu.VMEM((tc, tc), BF16)],
        compiler_params=_params(("arbitrary",)),
        name="s5_scan",
    )(u_g, bm, kf, kb, cmt, lam)


def _glu_kernel(yc_ref, yl_ref, w_ref, b_ref, o_ref, *, ctx_blocks):
    def glu(y):
        z = _gelu_tanh(y.astype(F32))
        o_ref[...] = (z * _sigmoid(_dot(z.astype(BF16), w_ref[...]) + b_ref[...])).astype(o_ref.dtype)

    is_ctx = pl.program_id(0) < ctx_blocks

    @pl.when(is_ctx)
    def _():
        glu(yc_ref[...])

    @pl.when(jnp.logical_not(is_ctx))
    def _():
        glu(yl_ref[...])


def _s5_glu(y_ctx, y_lat, w, layer, b, *, tm):
    c = y_ctx.shape[1]
    cb = y_ctx.shape[0] // tm
    n = y_ctx.shape[0] + y_lat.shape[0]
    return pl.pallas_call(
        functools.partial(_glu_kernel, ctx_blocks=cb),
        grid=(n // tm,),
        in_specs=[
            pl.BlockSpec((tm, c), lambda i: (jnp.minimum(i, cb - 1), 0)),
            pl.BlockSpec((tm, c), lambda i: (jnp.maximum(i - cb, 0), 0)),
            pl.BlockSpec((None, c, c), lambda i: (layer, 0, 0)),
            pl.BlockSpec((1, c), lambda i: (0, 0)),
        ],
        out_specs=pl.BlockSpec((tm, c), lambda i: (i, 0)),
        out_shape=jax.ShapeDtypeStruct((n, c), BF16),
        compiler_params=_params(("arbitrary",)),
        name="s5_glu",
    )(y_ctx, y_lat, w, b.reshape(1, c))


def _lane_select_matrices(gch):
    m = 256 // gch
    a = jnp.arange(m)[:, None, None, None]
    b = jnp.arange(m)[None, :, None, None]
    k = jnp.arange(256)[None, None, :, None]
    n = jnp.arange(256)[None, None, None, :]
    sel = (k == a * gch + n % gch) & (n // gch == b)
    return sel.astype(BF16).reshape(m, m * 256, 256)


def _lane_perm_kernel(*refs):
    *x_refs, sel_ref, o_ref = refs
    off = 0
    for x_ref in x_refs:
        rows = x_ref.shape[0]
        o_ref[off:off + rows, :] = _dot(x_ref[...], sel_ref[...]).astype(o_ref.dtype)
        off += rows


def _lane_perm(xs, sel):
    nj = xs[0].shape[0]
    m, kdim, _ = sel.shape
    rows = sum(x.shape[1] for x in xs)
    return pl.pallas_call(
        _lane_perm_kernel,
        grid=(nj, m),
        in_specs=[pl.BlockSpec((None, x.shape[1], kdim), lambda j, a: (j, 0, 0)) for x in xs]
        + [pl.BlockSpec((None, kdim, 256), lambda j, a: (a, 0, 0))],
        out_specs=pl.BlockSpec((None, rows, 256), lambda j, a: (j * m + a, 0, 0)),
        out_shape=jax.ShapeDtypeStruct((nj * m, rows, 256), BF16),
        compiler_params=_params(("arbitrary", "arbitrary")),
        name="s5_lane_perm",
    )(*xs, sel)


def _head_ones(width, head):
    i = jnp.arange(width) // head
    return (i[:, None] == i[None, :]).astype(BF16)


def _rw_prep_kernel(zr_ref, zrp_ref, zrn_ref, zk_ref, zkp_ref, zkn_ref, zv_ref, zvp_ref, zvn_ref,
                    zs_ref, zsp_ref, zsn_ref, mur_ref, muk_ref, muv_ref, mus_ref,
                    w0_ref, wup_ref, a0_ref, aup_ref, gup_ref, kk_ref, ka_ref, rk_ref, ones_ref,
                    r_o, kk_o, v_o, lw_o, kd_o, beta_o, g_o, bonus_o, *,
                    lora_w, lora_a, small0, n_ctx, l_ctx, l_lat, c_rw, quarter):
    tm, width = zr_ref.shape
    halo = zrp_ref.shape[0]
    g0 = pl.program_id(0) * tm
    is_ctx = g0 < n_ctx
    pos = jnp.where(is_ctx, g0 % l_ctx, (g0 - n_ctx) % l_lat) + lax.broadcasted_iota(jnp.int32, (tm, 1), 0)
    row = lax.broadcasted_iota(jnp.int32, (tm, 1), 0)
    ok_m1 = jnp.where(is_ctx, pos, pos % GRID_W) >= 1
    ok_p1 = jnp.where(is_ctx, l_ctx - 1 - pos, GRID_W - 1 - pos % GRID_W) >= 1
    ok_mw = jnp.where(is_ctx, 0, pos) >= GRID_W
    ok_pw = jnp.where(is_ctx, 0, l_lat - GRID_W - pos) >= 1

    def mix(cur, prev, nxt, mu, col0):
        cur, prev, nxt = cur.astype(F32), prev.astype(F32), nxt.astype(F32)
        width = cur.shape[1]
        col = col0 + lax.broadcasted_iota(jnp.int32, (1, width), 1)
        cq = col // quarter
        cls = jnp.where(is_ctx, cq // 2, cq)
        m1 = jnp.where(row == 0, prev[halo - 1:halo, :], pltpu.roll(cur, 1, axis=0))
        p1 = jnp.where(row == tm - 1, nxt[0:1, :], pltpu.roll(cur, tm - 1, axis=0))
        mw = jnp.concatenate([prev, cur[:tm - halo]], axis=0)
        pw = jnp.concatenate([cur[halo:], nxt], axis=0)
        zs = jnp.where(cls == 0, jnp.where(ok_m1, m1, 0.0),
                       jnp.where(cls == 1, jnp.where(ok_p1, p1, 0.0),
                                 jnp.where(cls == 2, jnp.where(ok_mw, mw, 0.0), jnp.where(ok_pw, pw, 0.0))))
        return cur + (zs - cur) * mu

    sm = mix(zs_ref[:, small0:], zsp_ref[:, small0:], zsn_ref[:, small0:], mus_ref[...], 3 * c_rw)
    o_a = 2 * lora_w
    o_g = o_a + 2 * lora_a
    ones = ones_ref[...]
    tn = ones.shape[0]
    gate_in = _sigmoid(sm[:, o_g:]).astype(BF16)
    wd = [jnp.tanh(sm[:, d * lora_w:(d + 1) * lora_w]) for d in range(2)]
    ad = [sm[:, o_a + d * lora_a:o_a + (d + 1) * lora_a] for d in range(2)]
    j0 = pl.program_id(1) * width
    for s in range(width // tn):
        cs = slice(s * tn, (s + 1) * tn)
        r = mix(zr_ref[:, cs], zrp_ref[:, cs], zrn_ref[:, cs], mur_ref[:, cs], j0 + s * tn)
        k = mix(zk_ref[:, cs], zkp_ref[:, cs], zkn_ref[:, cs], muk_ref[:, cs], c_rw + j0 + s * tn)
        v = mix(zv_ref[:, cs], zvp_ref[:, cs], zvn_ref[:, cs], muv_ref[:, cs], 2 * c_rw + j0 + s * tn)
        g_o[:, cs] = _dot(gate_in, gup_ref[:, cs]).astype(g_o.dtype)
        kx = k * kk_ref[:, cs]
        ss = _dot_exact_rhs(kx * kx, ones)
        kk = kx * lax.rsqrt(jnp.maximum(ss, 1e-24))
        k_sum = jnp.zeros_like(k)
        for d in range(2):
            w = w0_ref[d][:, cs] + _dot3(wd[d], wup_ref[d][:, cs])
            w = -(jnp.maximum(-w, 0.0) + jnp.log(1.0 + jnp.exp(-jnp.abs(w)))) - 0.5
            lw_o[d, :, cs] = -jnp.exp(w)
            a = _sigmoid(a0_ref[d][:, cs] + _dot3(ad[d], aup_ref[d][:, cs]))
            kd = k * (1.0 + (a - 1.0) * ka_ref[:, cs])
            kd_o[d, :, cs] = kd.astype(kd_o.dtype)
            beta_o[d, :, cs] = (kk * a).astype(beta_o.dtype)
            k_sum = k_sum + kd
        r_o[:, cs] = r.astype(r_o.dtype)
        kk_o[:, cs] = kk.astype(kk_o.dtype)
        v_o[:, cs] = v.astype(v_o.dtype)
        bonus_o[:, cs] = (_dot_exact_rhs(r * (k_sum * 0.5) * rk_ref[:, cs], ones) * v).astype(bonus_o.dtype)


def _rw_prep(z_rkv, z_small, lp, *, bsz, l_ctx, l_lat, c_rw, head, gate_rank):
    n = z_rkv.shape[0]
    pack = RW_PACK * head
    tn = _pick(c_rw, RW_PREP_WIDTH)
    nj = c_rw // tn
    halo = GRID_W
    tm = _pick(l_lat, _pick(l_ctx, RW_PREP_ROWS))
    assert tm % halo == 0 and tm > halo
    lora_w = lp["rw_w_up"].shape[1]
    lora_a = lp["rw_a_up"].shape[1]
    lora_g = lp["rw_g_up"].shape[0]
    mu = lp["rw_mu"].reshape(1, -1)
    n_small = 2 * lora_w + 2 * lora_a + lora_g
    n_small_all = gate_rank + n_small
    assert mu.shape[1] == 3 * c_rw + n_small and mu.shape[1] % 4 == 0 and n_small_all % 128 == 0
    mu_big = mu[:, :3 * c_rw]
    mu_small = mu[:, 3 * c_rw:]
    row = lambda a: a.reshape(1, c_rw)
    hb = tm // halo
    last_halo = n // halo - 1

    def tiles(off, width_blocks):
        w = tn if width_blocks else n_small_all
        col = (lambda j: j + off) if width_blocks else (lambda j: 0)
        return [pl.BlockSpec((tm, w), lambda i, j: (i, col(j))),
                pl.BlockSpec((halo, w), lambda i, j: (jnp.maximum(i * hb - 1, 0), col(j))),
                pl.BlockSpec((halo, w), lambda i, j: (jnp.minimum((i + 1) * hb, last_halo), col(j)))]

    vec = lambda off: pl.BlockSpec((1, tn), lambda i, j: (0, j + off))
    dirvec = pl.BlockSpec((2, 1, tn), lambda i, j: (0, 0, j))
    out_t = pl.BlockSpec((tm, tn), lambda i, j: (i, j))
    out_d = pl.BlockSpec((2, tm, tn), lambda i, j: (0, i, j))
    sds = jax.ShapeDtypeStruct
    kern = functools.partial(_rw_prep_kernel, lora_w=lora_w, lora_a=lora_a, small0=gate_rank, n_ctx=bsz * l_ctx,
                             l_ctx=l_ctx, l_lat=l_lat, c_rw=c_rw, quarter=mu.shape[1] // 4)
    return pl.pallas_call(
        kern,
        grid=(n // tm, nj),
        in_specs=[
            *tiles(0, True), *tiles(nj, True), *tiles(2 * nj, True), *tiles(0, False),
            vec(0), vec(nj), vec(2 * nj),
            pl.BlockSpec((1, n_small), lambda i, j: (0, 0)),
            dirvec,
            pl.BlockSpec((2, lora_w, tn), lambda i, j: (0, 0, j)),
            dirvec,
            pl.BlockSpec((2, lora_a, tn), lambda i, j: (0, 0, j)),
            pl.BlockSpec((lora_g, tn), lambda i, j: (0, j)),
            vec(0), vec(0), vec(0),
            pl.BlockSpec((pack, pack), lambda i, j: (0, 0)),
        ],
        out_specs=[out_t, out_t, out_t, out_d, out_d, out_d, out_t, out_t],
        out_shape=[sds((n, c_rw), BF16), sds((n, c_rw), BF16), sds((n, c_rw), BF16),
                   sds((2, n, c_rw), F32), sds((2, n, c_rw), BF16), sds((2, n, c_rw), BF16),
                   sds((n, c_rw), BF16), sds((n, c_rw), BF16)],
        compiler_params=_params(("arbitrary", "arbitrary")),
        name="rwkv_prep",
    )(*([z_rkv] * 9), *([z_small] * 3), mu_big, mu_big, mu_big, mu_small,
      lp["rw_w0"].reshape(2, 1, c_rw), lp["rw_w_up"], lp["rw_a0"].reshape(2, 1, c_rw), lp["rw_a_up"],
      lp["rw_g_up"].astype(BF16), row(lp["rw_k_k"]), row(lp["rw_k_a"]), row(lp["rw_r_k"]),
      _head_ones(pack, head))


def _rw_scan_kernel(rf_ref, kkf_ref, vf_ref, lwf_ref, kdf_ref, betaf_ref, rb_ref, kkb_ref, vb_ref, lwb_ref, kdb_ref,
                    betab_ref, yf_ref, yb_ref, st_scr, *, head):
    t_len = rf_ref.shape[0]
    n_pack = RW_PACK
    width = n_pack * head
    n_groups = rf_ref.shape[1] // width

    @pl.when(pl.program_id(1) == 0)
    def _():
        st_scr[...] = jnp.zeros_like(st_scr)

    lane_head = lax.broadcasted_iota(jnp.int32, (1, width), 1) // head
    head_masks = [lane_head == h for h in range(n_pack)]

    def block_diag(x):
        xb = x.astype(BF16)
        zero = jnp.zeros_like(xb)
        return jnp.concatenate([jnp.where(m, xb, zero) for m in head_masks], axis=0)

    def mm(a, bd):
        return _dot(a.astype(BF16), bd)

    def mm_nt(a, bd):
        return lax.dot_general(a.astype(BF16), bd, (((1,), (1,)), ((), ())), preferred_element_type=F32)

    tt = lax.broadcasted_iota(jnp.int32, (t_len, width), 0)
    ss = lax.broadcasted_iota(jnp.int32, (t_len, width), 1) % head
    eye = jnp.where(tt == ss, 1.0, 0.0)
    ti = lax.broadcasted_iota(jnp.int32, (t_len, t_len), 0)
    si = lax.broadcasted_iota(jnp.int32, (t_len, t_len), 1)
    strict_d = [tt > ss, tt < ss]
    incl_d = [tt >= ss, tt <= ss]
    tri_d = [jnp.where(ti >= si, 1.0, 0.0).astype(BF16), jnp.where(ti <= si, 1.0, 0.0).astype(BF16)]

    units = [(d, slice(g * width, (g + 1) * width)) for d in range(2) for g in range(n_groups)]
    each = lambda f, *xs: [f(*a) for a in zip(*xs)]
    pick = lambda fwd_ref, bwd_ref: [(bwd_ref if d else fwd_ref)[:, c] for d, c in units]
    r = pick(rf_ref, rb_ref)
    kk = pick(kkf_ref, kkb_ref)
    v = pick(vf_ref, vb_ref)
    lw = pick(lwf_ref, lwb_ref)
    kd = pick(kdf_ref, kdb_ref)
    beta = pick(betaf_ref, betab_ref)
    strict = [strict_d[d] for d, _ in units]
    incl = [incl_d[d] for d, _ in units]
    tri = [tri_d[d] for d, _ in units]

    def cumulative(x, tri_u):
        hi, lo = _split(x)
        return _dot(tri_u, hi) + _dot(tri_u, lo)

    log_p = each(cumulative, lw, tri)
    log_pt = each(lambda x: jnp.sum(x, axis=0, keepdims=True), lw)
    p_inv = each(lambda x: jnp.exp(-x), log_p)
    a_bar = each(lambda a, p, w: a * jnp.exp(p - w), kk, log_p, lw)
    r_bar = each(lambda a, p: a * jnp.exp(p), r, log_p)
    b_til = each(jnp.multiply, beta, p_inv)
    k_til = each(jnp.multiply, kd, p_inv)
    p_rest = each(lambda t, p: jnp.exp(t - p), log_pt, log_p)
    b_end = each(jnp.multiply, beta, p_rest)
    k_end = each(jnp.multiply, kd, p_rest)

    ar = each(lambda a, b: jnp.concatenate([a, b], axis=0), a_bar, r_bar)
    gram_b = each(lambda a, b: mm_nt(a, block_diag(b)), ar, b_til)
    gram_k = each(lambda a, b: mm_nt(a, block_diag(b)), ar, k_til)
    l_b = each(lambda g, m: jnp.where(m, g[:t_len], 0.0), gram_b, strict)
    m_b = each(lambda g, m: jnp.where(m, g[t_len:], 0.0), gram_b, incl)
    l_k = each(lambda g, m: jnp.where(m, g[:t_len], 0.0), gram_k, strict)
    m_k = each(lambda g, m: jnp.where(m, g[t_len:], 0.0), gram_k, incl)

    stack = lambda a, b: jnp.concatenate([a, b], axis=0)
    t_inv = each(lambda x: eye - x, l_b)
    l_pow = each(lambda x: mm(x, block_diag(x)), l_b)
    span = 4
    while span < t_len:
        both = each(lambda t, x: mm(stack(t, x), block_diag(x)), t_inv, l_pow)
        t_inv = each(lambda t, p: t + p[:t_len], t_inv, both)
        l_pow = each(lambda p: p[t_len:], both)
        span *= 2
    t_inv = each(lambda t, x: t + mm(t, block_diag(x)), t_inv, l_pow)

    def head_t(x):
        return jnp.concatenate([x[:, h * head:(h + 1) * head].T for h in range(n_pack)], axis=1)

    bd_v = each(block_diag, v)
    w_til = each(lambda t, a: mm(t, block_diag(a)), t_inv, a_bar)
    lmv = each(lambda lk, mk, ke, bv: mm(jnp.concatenate([lk, mk, head_t(ke)], axis=0), bv), l_k, m_k, k_end, bd_v)
    u_til = each(lambda t, a: mm(t, block_diag(a[:t_len])), t_inv, lmv)
    mb_bt = each(lambda m, b: stack(m, head_t(b)), m_b, b_end)
    from_w = each(lambda a, w: mm(a, block_diag(w)), mb_bt, w_til)
    from_u = each(lambda a, u: mm(a, block_diag(u)), mb_bt, u_til)
    q_hat = each(lambda a, fw: a - fw[:t_len], r_bar, from_w)
    y_hat = each(lambda a, fu: a[t_len:2 * t_len] - fu[:t_len], lmv, from_u)
    g_mat = each(lambda t, fw: eye * jnp.exp(t) - fw[t_len:], log_pt, from_w)
    h_mat = each(lambda a, fu: a[2 * t_len:] - fu[t_len:], lmv, from_u)

    bd_s = [block_diag(st_scr[d, :, c]) for d, c in units]
    from_state = each(lambda q, g, s: mm(stack(q, g), s), q_hat, g_mat, bd_s)
    for (d, c), fs, y0, h in zip(units, from_state, y_hat, h_mat):
        (yb_ref if d else yf_ref)[:, c] = (fs[:t_len] + y0).astype(yf_ref.dtype)
        st_scr[d, :, c] = fs[t_len:] + h


def _rw_scan(r, kk, v, lw, kd, beta, *, bsz, l_ctx, l_lat, head):
    n, c_rw = r.shape
    t_len = RW_CHUNK
    nc = l_ctx // t_len
    nl = l_lat // t_len
    ctx_blocks = bsz * nc

    def blk(d, b, c):
        cc = nc - 1 - c if d else c
        cl = nl - 1 - (c - nc) if d else c - nc
        return jnp.where(c < nc, b * nc + cc, ctx_blocks + b * nl + cl)

    shared = lambda d: pl.BlockSpec((t_len, c_rw), lambda b, c: (blk(d, b, c), 0))
    per_dir = lambda d: pl.BlockSpec((None, t_len, c_rw), lambda b, c: (d, blk(d, b, c), 0))
    side = lambda d: [shared(d), shared(d), shared(d), per_dir(d), per_dir(d), per_dir(d)]
    kern = functools.partial(_rw_scan_kernel, head=head)
    sds = jax.ShapeDtypeStruct
    return pl.pallas_call(
        kern,
        grid=(bsz, nc + nl),
        in_specs=side(0) + side(1),
        out_specs=[shared(0), shared(1)],
        out_shape=[sds((n, c_rw), BF16), sds((n, c_rw), BF16)],
        scratch_shapes=[pltpu.VMEM((2, head, c_rw), F32)],
        compiler_params=_params(("arbitrary", "arbitrary")),
        name="rwkv_scan",
    )(r, kk, v, lw, kd, beta, r, kk, v, lw, kd, beta)


def _rw_readout_kernel(yf_ref, yb_ref, g_ref, bonus_ref, lnw_ref, lnb_ref, ones_ref, o_ref, *, head):
    ones = ones_ref[...]
    pack = ones.shape[0]
    inv = 1.0 / head
    for s in range(o_ref.shape[1] // pack):
        cs = slice(s * pack, (s + 1) * pack)
        y = yf_ref[:, cs].astype(F32) + yb_ref[:, cs].astype(F32)
        mu = _dot_exact_rhs(y, ones) * inv
        yc = y - mu
        var = _dot_exact_rhs(yc * yc, ones) * inv
        yn = yc * lax.rsqrt(var + GN_EPS) * lnw_ref[:, cs] + lnb_ref[:, cs]
        o_ref[:, cs] = ((yn + bonus_ref[:, cs]) * g_ref[:, cs]).astype(o_ref.dtype)


def _rw_readout(y_fwd, y_bwd, g, bonus, ln_w, ln_b, *, head, tm):
    n, c_rw = y_fwd.shape
    pack = RW_PACK * head
    tn = _pick(c_rw, RW_PREP_WIDTH)
    tile = pl.BlockSpec((tm, tn), lambda i, j: (i, j))
    vec = pl.BlockSpec((1, tn), lambda i, j: (0, j))
    kern = functools.partial(_rw_readout_kernel, head=head)
    return pl.pallas_call(
        kern,
        grid=(n // tm, c_rw // tn),
        in_specs=[tile, tile, tile, tile, vec, vec, pl.BlockSpec((pack, pack), lambda i, j: (0, 0))],
        out_specs=tile,
        out_shape=jax.ShapeDtypeStruct((n, c_rw), BF16),
        compiler_params=_params(("arbitrary", "arbitrary")),
        name="rwkv_readout",
    )(y_fwd, y_bwd, g, bonus, ln_w.reshape(1, c_rw), ln_b.reshape(1, c_rw), _head_ones(pack, head))


def _final_norm_kernel(h_ref, gain_ref, o_ref):
    x = h_ref[...]
    ms = jnp.mean(x * x, axis=-1, keepdims=True)
    o_ref[...] = x * lax.rsqrt(ms + NORM_EPS) * gain_ref[...]


def _final_norm(h, gain, *, tm, row0):
    n, d = h.shape
    b0 = row0 // tm
    return pl.pallas_call(
        _final_norm_kernel,
        grid=((n - row0) // tm,),
        in_specs=[pl.BlockSpec((tm, d), lambda i: (i + b0, 0)), pl.BlockSpec((1, d), lambda i: (0, 0))],
        out_specs=pl.BlockSpec((tm, d), lambda i: (i, 0)),
        out_shape=jax.ShapeDtypeStruct((n - row0, d), F32),
        compiler_params=_params(("arbitrary",)),
        name="final_norm",
    )(h, gain.reshape(1, d))


def _to_s5_layout(u, bsz, l_ctx, l_lat, sel):
    t = S5_CHUNK
    n_ctx = bsz * l_ctx
    nj = u.shape[0]

    def tiles(x, length):
        x = x.reshape(nj, bsz, length // t, t * 256)
        return jnp.transpose(x, (0, 2, 1, 3)).reshape(nj, (length // t) * bsz, t * 256)

    return _lane_perm([tiles(u[:, :n_ctx], l_ctx), tiles(u[:, n_ctx:], l_lat)], sel)


def _from_s5_layout(y, bsz, l_ctx, l_lat, sel):
    t = S5_CHUNK
    nj, rows, _ = y.shape
    out = _lane_perm([y], sel).reshape(nj, t, rows, 256)
    rows_ctx = (l_ctx // t) * bsz

    def untile(o, length):
        o = o.reshape(nj, t, length // t, bsz, 256)
        return jnp.transpose(o, (3, 2, 1, 0, 4)).reshape(bsz * length, nj * 256)

    return untile(out[:, :, :rows_ctx], l_ctx), untile(out[:, :, rows_ctx:], l_lat)


def kernel(x, c, ctx, c_ctx, ada_down, ada_up, ada_b, norm1, w_in, s5_a_re, s5_a_im, s5_log_dt, s5_b_re, s5_b_im, s5_c_re, s5_c_im, s5_d, s5_glu_w, s5_glu_b, rw_mu, rw_w0, rw_w_up, rw_a0, rw_a_up, rw_g_up, rw_k_k, rw_k_a, rw_r_k, rw_ln_w, rw_ln_b, w_proj_a, w_proj_b, gate_up, gate_b, w_out, norm2, mlp_w1, mlp_w2, norm_f):
    bsz, l_lat, d = x.shape
    l_ctx = ctx.shape[1]
    depth = w_in.shape[0]
    n_ctx = bsz * l_ctx
    n_lat = bsz * l_lat
    s5_groups, s5_state = s5_a_re.shape[2], s5_a_re.shape[3]
    s5_gch = s5_b_re.shape[-1]
    c_s5 = s5_groups * s5_gch
    rw_heads, head = rw_r_k.shape[1], rw_r_k.shape[2]
    c_rw = rw_heads * head
    lora_w, lora_a, lora_g = rw_w_up.shape[2], rw_a_up.shape[2], rw_g_up.shape[1]
    gate_rank = gate_up.shape[2]
    n_small_rw = 2 * lora_w + 2 * lora_a + lora_g
    c_big = c_s5 + 3 * c_rw
    assert w_in.shape[2] == c_big + n_small_rw + gate_rank
    assert bsz % 8 == 0 and l_ctx % RW_CHUNK == 0 and l_lat % RW_CHUNK == 0 and l_lat % GRID_W == 0
    assert s5_gch * S5_CHUNK == 256 and RW_PACK * head == 256 and c_rw % 256 == 0 and gate_rank % 128 == 0
    assert RW_CHUNK == head
    assert c_s5 % IN_PROJ_TN == 0 and (3 * c_rw) % IN_PROJ_TN == 0 and n_small_rw + gate_rank <= IN_PROJ_TN

    tm = _pick(l_lat, _pick(n_ctx, TM))

    def row_of_block(i):
        ctx_blocks = n_ctx // tm
        return jnp.where(i < ctx_blocks, bsz, (i - ctx_blocks) // (l_lat // tm))

    n_cond = ((bsz + 1 + 7) // 8) * 8
    cond = jnp.zeros((n_cond, d), F32).at[:bsz].set(c).at[bsz].set(c_ctx)
    mods_all = _ada_modulation(cond, ada_down, ada_up, ada_b)
    mods_all = jnp.transpose(mods_all.reshape(depth, n_cond, 6, 1, d), (0, 2, 1, 3, 4))

    h = jnp.concatenate([ctx.reshape(n_ctx, d), x.reshape(n_lat, d)], axis=0)

    lane_sel = _lane_select_matrices(s5_gch)
    w_in_bf = _pack_w_in(w_in, c_big=c_big, n_small=n_small_rw, rank=gate_rank, tn=IN_PROJ_TN)
    glu_bf = _to_bf16(s5_glu_w)
    proj_a_bf, proj_b_bf, w_out_bf = _to_bf16(w_proj_a), _to_bf16(w_proj_b), _to_bf16(w_out)
    gate_up_bf = _to_bf16(gate_up.reshape(depth, 2 * gate_rank, d)).reshape(gate_up.shape)
    w1_bf, w2_bf = _to_bf16(mlp_w1), _to_bf16(mlp_w2)

    for l in range(depth):
        last = l == depth - 1
        row0 = n_ctx if last else 0
        mods = mods_all[l]
        lp = {"rw_mu": rw_mu[l], "rw_w0": rw_w0[l], "rw_w_up": rw_w_up[l], "rw_a0": rw_a0[l],
              "rw_a_up": rw_a_up[l], "rw_g_up": rw_g_up[l], "rw_k_k": rw_k_k[l], "rw_k_a": rw_k_a[l],
              "rw_r_k": rw_r_k[l]}
        z_u, z_rkv, z_small = _in_proj(h, norm1[l], mods, w_in_bf, l, row_of_block, tm=tm, tn=IN_PROJ_TN,
                                       c_u=c_s5, c_rkv=3 * c_rw)

        kf, kb, bm, cmt, lam = _s5_matrices(s5_a_re[l], s5_a_im[l], s5_log_dt[l], s5_b_re[l], s5_b_im[l],
                                            s5_c_re[l], s5_c_im[l], s5_d[l])
        u_g = _to_s5_layout(z_u, bsz, l_ctx, l_lat, lane_sel)
        y_g = _s5_scan(u_g, kf, kb, bm, cmt, lam, n_ctx=l_ctx // S5_CHUNK, n_chunks=(l_ctx + l_lat) // S5_CHUNK, bsz=bsz)
        y_ctx, y_lat = _from_s5_layout(y_g, bsz, l_ctx, l_lat, lane_sel)
        ya = _s5_glu(y_ctx, y_lat, glu_bf, l, s5_glu_b[l], tm=tm)

        r, kk, v, lw, kd, beta, g, bonus = _rw_prep(z_rkv, z_small, lp, bsz=bsz, l_ctx=l_ctx, l_lat=l_lat,
                                                    c_rw=c_rw, head=head, gate_rank=gate_rank)
        y_fwd, y_bwd = _rw_scan(r, kk, v, lw, kd, beta, bsz=bsz, l_ctx=l_ctx, l_lat=l_lat, head=head)
        yb = _rw_readout(y_fwd, y_bwd, g, bonus, rw_ln_w[l], rw_ln_b[l], head=head, tm=tm)

        h = _merge_out(ya, yb, z_small, proj_a_bf, proj_b_bf, gate_up_bf, gate_b[l], w_out_bf, l, h, mods,
                       row_of_block, tm=tm, row0=row0)
        h = _mlp(h, norm2[l], mods, w1_bf, w2_bf, l, row_of_block, tm=tm, row0=row0)

    out = _final_norm(h, norm_f, tm=tm, row0=n_ctx)
    return out.reshape(bsz, l_lat, d)
```

```python
import functools

import jax
import jax.numpy as jnp
from jax import lax
from jax.experimental import pallas as pl
from jax.experimental.pallas import tpu as pltpu

F32 = jnp.float32
BF16 = jnp.bfloat16

GRID_W = 64
NORM_EPS = 1e-6
GN_EPS = 64e-5
A_RE_MAX = -1e-4
S5_CHUNK = 16
RW_CHUNK = 64
RW_PACK = 4
RW_PREP_WIDTH = 1024
RW_PREP_ROWS = 512
TM = 512
IN_PROJ_TN = 1024
VMEM_LIMIT = 56 * 1024 * 1024


def _params(sem):
    return pltpu.CompilerParams(dimension_semantics=sem, vmem_limit_bytes=VMEM_LIMIT)


def _pick(n, pref):
    t = pref
    while n % t:
        t //= 2
    return t


def _dot(a, b):
    return jnp.dot(a, b, preferred_element_type=F32)


def _split(x):
    hi = x.astype(BF16)
    lo = (x - hi.astype(F32)).astype(BF16)
    return hi, lo


def _dot3(a, b):
    ah, al = _split(a)
    bh, bl = _split(b)
    return _dot(ah, bh) + (_dot(ah, bl) + _dot(al, bh))


def _dot_exact_rhs(a, b_bf16):
    ah, al = _split(a)
    return _dot(ah, b_bf16) + _dot(al, b_bf16)


def _sigmoid(x):
    return 1.0 / (1.0 + jnp.exp(-x))


def _gelu_tanh(x):
    c = 0.7978845608028654
    return 0.5 * x * (1.0 + jnp.tanh(c * (x + 0.044715 * (x * x * x))))


def _cast_kernel(w_ref, o_ref):
    o_ref[...] = w_ref[...].astype(o_ref.dtype)


def _to_bf16(w):
    nl, r, c = w.shape
    rows_4mib = max(16, (1 << 20) // c)
    tr = _pick(r, 1 << (rows_4mib.bit_length() - 1))
    out = pl.pallas_call(
        _cast_kernel,
        grid=(nl * r // tr,),
        in_specs=[pl.BlockSpec((tr, c), lambda i: (i, 0))],
        out_specs=pl.BlockSpec((tr, c), lambda i: (i, 0)),
        out_shape=jax.ShapeDtypeStruct((nl * r, c), BF16),
        compiler_params=_params(("arbitrary",)),
        name="weights_to_bf16",
    )(w.reshape(nl * r, c))
    return out.reshape(nl, r, c)


def _pack_w_in_kernel(w_ref, o_ref, *, c_big, n_small, rank):
    o_ref[:, :c_big] = w_ref[:, :c_big].astype(BF16)
    o_ref[:, c_big:c_big + rank] = w_ref[:, c_big + n_small:].astype(BF16)
    o_ref[:, c_big + rank:c_big + rank + n_small] = w_ref[:, c_big:c_big + n_small].astype(BF16)
    pad = o_ref.shape[1] - (c_big + rank + n_small)
    if pad:
        o_ref[:, c_big + rank + n_small:] = jnp.zeros((o_ref.shape[0], pad), BF16)


def _pack_w_in(w_in, *, c_big, n_small, rank, tn):
    nl, d, c_in = w_in.shape
    c_out = c_big + tn
    assert c_big % 128 == 0 and n_small % 128 == 0 and rank % 128 == 0 and rank + n_small <= tn
    tr = _pick(d, 64)
    out = pl.pallas_call(
        functools.partial(_pack_w_in_kernel, c_big=c_big, n_small=n_small, rank=rank),
        grid=(nl * d // tr,),
        in_specs=[pl.BlockSpec((tr, c_in), lambda i: (i, 0))],
        out_specs=pl.BlockSpec((tr, c_out), lambda i: (i, 0)),
        out_shape=jax.ShapeDtypeStruct((nl * d, c_out), BF16),
        compiler_params=_params(("arbitrary",)),
        name="pack_w_in",
    )(w_in.reshape(nl * d, c_in))
    return out.reshape(nl, d, c_out)


def _ada_kernel(cond_ref, wd_ref, wu_ref, b_ref, o_ref, mid_scr):
    @pl.when(pl.program_id(1) == 0)
    def _():
        c = cond_ref[...]
        mid_scr[...] = _dot3(c * _sigmoid(c), wd_ref[...])

    o_ref[...] = _dot3(mid_scr[...], wu_ref[...]) + b_ref[...]


def _ada_modulation(cond, w_down, w_up, bias):
    nl, d, rank = w_down.shape
    r = cond.shape[0]
    n_out = w_up.shape[-1]
    tn = _pick(n_out, 2048)
    return pl.pallas_call(
        _ada_kernel,
        grid=(nl, n_out // tn),
        in_specs=[
            pl.BlockSpec((r, d), lambda l, j: (0, 0)),
            pl.BlockSpec((None, d, rank), lambda l, j: (l, 0, 0)),
            pl.BlockSpec((None, rank, tn), lambda l, j: (l, 0, j)),
            pl.BlockSpec((None, 1, tn), lambda l, j: (l, 0, j)),
        ],
        out_specs=pl.BlockSpec((None, r, tn), lambda l, j: (l, 0, j)),
        out_shape=jax.ShapeDtypeStruct((nl, r, n_out), F32),
        scratch_shapes=[pltpu.VMEM((r, rank), F32)],
        compiler_params=_params(("arbitrary", "arbitrary")),
        name="ada_modulation",
    )(cond, w_down, w_up, bias.reshape(nl, 1, n_out))


def _in_proj_kernel(h_ref, gain_ref, shift_ref, scale_ref, w_ref, zu_ref, zrkv_ref, zs_ref, n_scr, *, ju, jr):
    j = pl.program_id(1)

    @pl.when(j == 0)
    def _():
        x = h_ref[...]
        ms = jnp.mean(x * x, axis=-1, keepdims=True)
        xn = x * lax.rsqrt(ms + NORM_EPS) * gain_ref[...]
        n_scr[...] = (xn * (1.0 + scale_ref[...]) + shift_ref[...]).astype(BF16)

    acc = _dot(n_scr[...], w_ref[...])

    @pl.when(j < ju)
    def _():
        for t in range(zu_ref.shape[0]):
            zu_ref[t] = acc[:, t * 256:(t + 1) * 256].astype(zu_ref.dtype)

    @pl.when(jnp.logical_and(j >= ju, j < ju + jr))
    def _():
        zrkv_ref[...] = acc.astype(zrkv_ref.dtype)

    @pl.when(j == ju + jr)
    def _():
        zs_ref[...] = acc


def _in_proj(h, gain, mods, w_all, layer, row_of_block, *, tm, tn, c_u, c_rkv):
    n, d = h.shape
    ju, jr = c_u // tn, c_rkv // tn
    assert w_all.shape[2] == (ju + jr + 1) * tn
    mod = lambda which: pl.BlockSpec((None, None, 1, d), lambda i, j: (which, row_of_block(i), 0, 0))
    sds = jax.ShapeDtypeStruct
    return pl.pallas_call(
        functools.partial(_in_proj_kernel, ju=ju, jr=jr),
        grid=(n // tm, ju + jr + 1),
        in_specs=[
            pl.BlockSpec((tm, d), lambda i, j: (i, 0), pipeline_mode=pl.Buffered(1)),
            pl.BlockSpec((1, d), lambda i, j: (0, 0)),
            mod(0), mod(1),
            pl.BlockSpec((None, d, tn), lambda i, j: (layer, 0, j)),
        ],
        out_specs=[pl.BlockSpec((tn // 256, tm, 256), lambda i, j: (jnp.minimum(j, ju - 1), i, 0)),
                   pl.BlockSpec((tm, tn), lambda i, j: (i, jnp.clip(j - ju, 0, jr - 1))),
                   pl.BlockSpec((tm, tn), lambda i, j: (i, 0))],
        out_shape=[sds((c_u // 256, n, 256), BF16), sds((n, c_rkv), BF16), sds((n, tn), F32)],
        scratch_shapes=[pltpu.VMEM((tm, d), BF16)],
        compiler_params=_params(("arbitrary", "arbitrary")),
        name="in_proj",
    )(h, gain.reshape(1, d), mods, mods, w_all)


def _mlp_kernel(h_ref, gain_ref, shift_ref, scale_ref, gate_ref, w1_ref, w2_ref, o_ref, n_scr):
    j = pl.program_id(1)

    @pl.when(j == 0)
    def _():
        x = h_ref[...]
        ms = jnp.mean(x * x, axis=-1, keepdims=True)
        xn = x * lax.rsqrt(ms + NORM_EPS) * gain_ref[...]
        n_scr[...] = (xn * (1.0 + scale_ref[...]) + shift_ref[...]).astype(BF16)
        o_ref[...] = jnp.zeros_like(o_ref)

    hid = jnp.square(jnp.maximum(_dot(n_scr[...], w1_ref[...]), 0.0)).astype(BF16)
    o_ref[...] += _dot(hid, w2_ref[...])

    @pl.when(j == pl.num_programs(1) - 1)
    def _():
        o_ref[...] = h_ref[...] + gate_ref[...] * o_ref[...]


def _mlp(h, gain, mods, w1, w2, layer, row_of_block, *, tm, row0):
    n, d = h.shape
    d_ff = w1.shape[2]
    tf = _pick(d_ff, 512)
    b0 = row0 // tm
    mod = lambda which: pl.BlockSpec((None, None, 1, d), lambda i, j: (which, row_of_block(i + b0), 0, 0))
    return pl.pallas_call(
        _mlp_kernel,
        grid=((n - row0) // tm, d_ff // tf),
        in_specs=[
            pl.BlockSpec((tm, d), lambda i, j: (i + b0, 0), pipeline_mode=pl.Buffered(1)),
            pl.BlockSpec((1, d), lambda i, j: (0, 0)),
            mod(3), mod(4), mod(5),
            pl.BlockSpec((None, d, tf), lambda i, j: (layer, 0, j)),
            pl.BlockSpec((None, tf, d), lambda i, j: (layer, j, 0)),
        ],
        out_specs=pl.BlockSpec((tm, d), lambda i, j: (i + b0, 0)),
        out_shape=jax.ShapeDtypeStruct((n, d), F32),
        scratch_shapes=[pltpu.VMEM((tm, d), BF16)],
        input_output_aliases={0: 0},
        compiler_params=_params(("arbitrary", "arbitrary")),
        name="mlp",
    )(h, gain.reshape(1, d), mods, mods, mods, w1, w2)


def _merge_out_kernel(ya_ref, yb_ref, gz_ref, wa_ref, wb_ref, ga_ref, gb_ref, ba_ref, bb_ref, wo_ref, h_ref,
                      gate_ref, o_ref):
    j = pl.program_id(1)

    @pl.when(j == 0)
    def _():
        o_ref[...] = jnp.zeros_like(o_ref)

    gz = gz_ref[...].astype(BF16)
    ga = _sigmoid(_dot(gz, ga_ref[...]) + ba_ref[...])
    gb = _sigmoid(_dot(gz, gb_ref[...]) + bb_ref[...])
    mixed = (ga * _dot(ya_ref[...], wa_ref[...]) + gb * _dot(yb_ref[...], wb_ref[...])).astype(BF16)
    o_ref[...] += _dot(mixed, wo_ref[...])

    @pl.when(j == pl.num_programs(1) - 1)
    def _():
        o_ref[...] = h_ref[...] + gate_ref[...] * o_ref[...]


def _merge_out(ya, yb, z_small, wa, wb, gate_up, gate_b, w_out, layer, h, mods, row_of_block, *, tm, row0):
    n, d = h.shape
    ca, cb = ya.shape[1], yb.shape[1]
    d_mid = wa.shape[2]
    rank = gate_up.shape[2]
    tn = _pick(d_mid, 512)
    b0 = row0 // tm
    once = pl.Buffered(1)
    return pl.pallas_call(
        _merge_out_kernel,
        grid=((n - row0) // tm, d_mid // tn),
        in_specs=[
            pl.BlockSpec((tm, ca), lambda i, j: (i + b0, 0), pipeline_mode=once),
            pl.BlockSpec((tm, cb), lambda i, j: (i + b0, 0), pipeline_mode=once),
            pl.BlockSpec((tm, rank), lambda i, j: (i + b0, 0)),
            pl.BlockSpec((None, ca, tn), lambda i, j: (layer, 0, j)),
            pl.BlockSpec((None, cb, tn), lambda i, j: (layer, 0, j)),
            pl.BlockSpec((None, None, rank, tn), lambda i, j: (layer, 0, 0, j)),
            pl.BlockSpec((None, None, rank, tn), lambda i, j: (layer, 1, 0, j)),
            pl.BlockSpec((None, 1, tn), lambda i, j: (0, 0, j)),
            pl.BlockSpec((None, 1, tn), lambda i, j: (1, 0, j)),
            pl.BlockSpec((None, tn, d), lambda i, j: (layer, j, 0)),
            pl.BlockSpec((tm, d), lambda i, j: (i + b0, 0), pipeline_mode=once),
            pl.BlockSpec((None, None, 1, d), lambda i, j: (2, row_of_block(i + b0), 0, 0)),
        ],
        out_specs=pl.BlockSpec((tm, d), lambda i, j: (i + b0, 0)),
        out_shape=jax.ShapeDtypeStruct((n, d), F32),
        input_output_aliases={10: 0},
        compiler_params=_params(("arbitrary", "arbitrary")),
        name="merge_out_proj",
    )(ya, yb, z_small, wa, wb, gate_up, gate_up, gate_b.reshape(2, 1, d_mid), gate_b.reshape(2, 1, d_mid),
      w_out, h, mods)


def _s5_matrices(a_re, a_im, log_dt, b_re, b_im, c_re, c_im, d_skip):
    t_len = S5_CHUNK
    lam_re = jnp.minimum(a_re.astype(F32), A_RE_MAX)
    lam_im = a_im.astype(F32)
    dt = jnp.exp(log_dt.astype(F32))[..., None]
    j = jnp.arange(t_len + 1, dtype=F32)[:, None, None, None]
    mag = jnp.exp(j * (lam_re * dt))
    ang = j * (lam_im * dt)
    pw_re = mag * jnp.cos(ang)
    pw_im = mag * jnp.sin(ang)
    nr, ni = pw_re[1] - 1.0, pw_im[1]
    den = lam_re * lam_re + lam_im * lam_im
    qr = (nr * lam_re + ni * lam_im) / den
    qi = (ni * lam_re - nr * lam_im) / den
    bbr = qr[..., None] * b_re - qi[..., None] * b_im
    bbi = qr[..., None] * b_im + qi[..., None] * b_re
    n_dir, g, p, c = bbr.shape
    bbr_t = jnp.transpose(bbr, (0, 1, 3, 2))
    bbi_t = jnp.transpose(bbi, (0, 1, 3, 2))
    per_group = lambda z: jnp.transpose(z, (1, 0, 2, 3))
    lag = {"c_re": per_group(c_re.astype(F32)), "c_im": per_group(c_im.astype(F32)),
           "pw_re": jnp.transpose(pw_re[:t_len], (2, 1, 0, 3)), "pw_im": jnp.transpose(pw_im[:t_len], (2, 1, 0, 3)),
           "b_re": per_group(bbr_t), "b_im": per_group(bbi_t)}
    skip = d_skip.astype(F32).reshape(g, c)
    lag["skip"] = skip[:, :, None] * (jnp.arange(t_len * c)[None, None, :] == jnp.arange(c)[None, :, None])

    def powers(d, e):
        return jnp.transpose(pw_re[e, d], (1, 0, 2)), jnp.transpose(pw_im[e, d], (1, 0, 2))

    lanes4 = lambda a, b, cc, dd: jnp.concatenate([a, b, cc, dd], axis=-1)

    pf_r, pf_i = powers(0, (t_len - 1) - jnp.arange(t_len))
    pb_r, pb_i = powers(1, jnp.arange(t_len))
    bm = (lanes4(pf_r, pb_r, pf_r, pb_r)[:, :, None, :] * lanes4(bbr_t[0], bbr_t[1], bbi_t[0], bbi_t[1])[:, None]
          + lanes4(-pf_i, -pb_i, pf_i, pb_i)[:, :, None, :] * lanes4(bbi_t[0], bbi_t[1], bbr_t[0], bbr_t[1])[:, None])
    bm = bm.reshape(g, t_len * c, 4 * p)

    qf_r, qf_i = powers(0, jnp.arange(t_len) + 1)
    qb_r, qb_i = powers(1, t_len - jnp.arange(t_len))
    cmt = (lanes4(c_re[0], c_re[1], c_re[0], c_re[1])[:, None] * lanes4(qf_r, qb_r, -qf_i, -qb_i)[:, :, None, :]
           + lanes4(c_im[0], c_im[1], c_im[0], c_im[1])[:, None] * lanes4(-qf_i, -qb_i, -qf_r, -qb_r)[:, :, None, :])
    cmt = cmt.reshape(g, t_len * c, 4 * p)
    lam = jnp.stack([jnp.concatenate([pw_re[t_len, 0], pw_re[t_len, 1]], axis=-1),
                     jnp.concatenate([pw_im[t_len, 0], pw_im[t_len, 1]], axis=-1)], axis=1)
    return lag, bm.astype(BF16), cmt.astype(BF16), lam


def _s5_kernel(u_ref, bm_ref, cre_ref, cim_ref, pwr_ref, pwi_ref, bre_ref, bim_ref, skip_ref, cmt_ref, lam_ref, y_ref,
               xin_scr, fwd_scr, bwd_scr, km_scr, *, n_ctx, n_chunks, bsz, p_state):
    u = u_ref[...]
    xin_scr[...] = _dot(u, bm_ref[...])
    gch, tc = skip_ref.shape
    n_lag = tc // gch

    def nt3(a, b):
        nt = lambda x, y: lax.dot_general(x, y, (((1,), (1,)), ((), ())), preferred_element_type=F32)
        ah, al = _split(a)
        bh, bl = _split(b)
        return nt(ah, bh) + (nt(ah, bl) + nt(al, bh))

    def lag_kernels(d, reverse):
        cr, ci = cre_ref[d], cim_ref[d]
        rows_r, rows_i = [], []
        for m in range(n_lag):
            j = n_lag - 1 - m if reverse else m
            pr, pi = pwr_ref[d, j:j + 1, :], pwi_ref[d, j:j + 1, :]
            rows_r.append(cr * pr - ci * pi)
            rows_i.append(cr * pi + ci * pr)
        return (nt3(bre_ref[d], jnp.concatenate(rows_r, axis=0))
                - nt3(bim_ref[d], jnp.concatenate(rows_i, axis=0)))

    kf = lag_kernels(0, False) + skip_ref[...]
    kb = lag_kernels(1, True)
    lane_k = lax.broadcasted_iota(jnp.int32, (gch, tc), 1)
    for s in range(tc // gch):
        fwd = kf if s == 0 else jnp.where(lane_k >= gch * s, pltpu.roll(kf, gch * s, axis=1), 0.0)
        shift = (gch * (s + 1)) % tc
        bwd = kb if shift == 0 else jnp.where(lane_k < gch * (s + 1), pltpu.roll(kb, shift, axis=1), 0.0)
        km_scr[s * gch:(s + 1) * gch, :] = (fwd + bwd).astype(BF16)
    two_p = 2 * p_state
    lam_r = jnp.broadcast_to(lam_ref[0:1, :], (bsz, two_p))
    lam_i = jnp.broadcast_to(lam_ref[1:2, :], (bsz, two_p))
    is_fwd = lax.broadcasted_iota(jnp.int32, (bsz, two_p), 1) < p_state

    def body(i, carry):
        re, im = carry
        pb = jnp.where(i < n_ctx, n_ctx - 1 - i, n_chunks + n_ctx - 1 - i)
        rf = pl.multiple_of(i * bsz, bsz)
        rb = pl.multiple_of(pb * bsz, bsz)
        xf = xin_scr[pl.ds(rf, bsz), :]
        xb = xin_scr[pl.ds(rb, bsz), :]
        x_re = jnp.where(is_fwd, xf[:, :two_p], xb[:, :two_p])
        x_im = jnp.where(is_fwd, xf[:, two_p:], xb[:, two_p:])
        state = jnp.concatenate([re, im], axis=1)
        fwd_scr[pl.ds(rf, bsz), :] = state
        bwd_scr[pl.ds(rb, bsz), :] = state
        return (re * lam_r - im * lam_i + x_re, re * lam_i + im * lam_r + x_im)

    zero = jnp.zeros((bsz, two_p), F32)
    lax.fori_loop(0, n_chunks, body, (zero, zero))
    lane = lax.broadcasted_iota(jnp.int32, (1, 2 * two_p), 1)
    fwd_lane = (lane % two_p) < p_state
    xs = jnp.where(fwd_lane, fwd_scr[...], bwd_scr[...]).astype(BF16)
    from_state = lax.dot_general(xs, cmt_ref[...], (((1,), (1,)), ((), ())), preferred_element_type=F32)
    y_ref[...] = (_dot(u, km_scr[...]) + from_state).astype(y_ref.dtype)


def _s5_scan(u_g, lag, bm, cmt, lam, *, n_ctx, n_chunks, bsz):
    g, rows, tc = u_g.shape
    gch = lag["skip"].shape[1]
    m = 256 // gch
    p4 = bm.shape[-1]
    factor = lambda a: pl.BlockSpec((None,) + a.shape[1:], lambda i: (i, 0, 0, 0))
    kern = functools.partial(_s5_kernel, n_ctx=n_ctx, n_chunks=n_chunks, bsz=bsz, p_state=p4 // 4)
    return pl.pallas_call(
        kern,
        grid=(g,),
        in_specs=[
            pl.BlockSpec((None, rows, tc), lambda i: (i, 0, 0)),
            pl.BlockSpec((None, tc, p4), lambda i: (i, 0, 0)),
            factor(lag["c_re"]), factor(lag["c_im"]), factor(lag["pw_re"]), factor(lag["pw_im"]),
            factor(lag["b_re"]), factor(lag["b_im"]),
            pl.BlockSpec((None, gch, tc), lambda i: (i, 0, 0)),
            pl.BlockSpec((None, tc, p4), lambda i: (i, 0, 0)),
            pl.BlockSpec((None, 2, p4 // 2), lambda i: (i, 0, 0)),
        ],
        out_specs=pl.BlockSpec((None, rows, tc), lambda i: (i // m, 0, i % m)),
        out_shape=jax.ShapeDtypeStruct((g // m, rows, m * tc), BF16),
        scratch_shapes=[pltpu.VMEM((rows, p4), F32), pltpu.VMEM((rows, p4), F32), pltpu.VMEM((rows, p4), F32),
                        pltpu.VMEM((tc, tc), BF16)],
        compiler_params=_params(("arbitrary",)),
        name="s5_scan",
    )(u_g, bm, lag["c_re"], lag["c_im"], lag["pw_re"], lag["pw_im"], lag["b_re"], lag["b_im"], lag["skip"], cmt, lam)


def _glu_kernel(yc_ref, yl_ref, w_ref, b_ref, o_ref, *, ctx_blocks):
    def glu(y):
        z = _gelu_tanh(y.astype(F32))
        o_ref[...] = (z * _sigmoid(_dot(z.astype(BF16), w_ref[...]) + b_ref[...])).astype(o_ref.dtype)

    is_ctx = pl.program_id(0) < ctx_blocks

    @pl.when(is_ctx)
    def _():
        glu(yc_ref[...])

    @pl.when(jnp.logical_not(is_ctx))
    def _():
        glu(yl_ref[...])


def _s5_glu(y_ctx, y_lat, w, layer, b, *, tm):
    c = y_ctx.shape[1]
    cb = y_ctx.shape[0] // tm
    n = y_ctx.shape[0] + y_lat.shape[0]
    return pl.pallas_call(
        functools.partial(_glu_kernel, ctx_blocks=cb),
        grid=(n // tm,),
        in_specs=[
            pl.BlockSpec((tm, c), lambda i: (jnp.minimum(i, cb - 1), 0)),
            pl.BlockSpec((tm, c), lambda i: (jnp.maximum(i - cb, 0), 0)),
            pl.BlockSpec((None, c, c), lambda i: (layer, 0, 0)),
            pl.BlockSpec((1, c), lambda i: (0, 0)),
        ],
        out_specs=pl.BlockSpec((tm, c), lambda i: (i, 0)),
        out_shape=jax.ShapeDtypeStruct((n, c), BF16),
        compiler_params=_params(("arbitrary",)),
        name="s5_glu",
    )(y_ctx, y_lat, w, b.reshape(1, c))


def _lane_select_matrices(gch):
    m = 256 // gch
    a = jnp.arange(m)[:, None, None, None]
    b = jnp.arange(m)[None, :, None, None]
    k = jnp.arange(256)[None, None, :, None]
    n = jnp.arange(256)[None, None, None, :]
    sel = (k == a * gch + n % gch) & (n // gch == b)
    return sel.astype(BF16).reshape(m, m * 256, 256)


def _lane_perm_kernel(*refs):
    *x_refs, sel_ref, o_ref = refs
    off = 0
    for x_ref in x_refs:
        rows = x_ref.shape[0]
        o_ref[off:off + rows, :] = _dot(x_ref[...], sel_ref[...]).astype(o_ref.dtype)
        off += rows


def _lane_perm(xs, sel):
    nj = xs[0].shape[0]
    m, kdim, _ = sel.shape
    rows = sum(x.shape[1] for x in xs)
    return pl.pallas_call(
        _lane_perm_kernel,
        grid=(nj, m),
        in_specs=[pl.BlockSpec((None, x.shape[1], kdim), lambda j, a: (j, 0, 0)) for x in xs]
        + [pl.BlockSpec((None, kdim, 256), lambda j, a: (a, 0, 0))],
        out_specs=pl.BlockSpec((None, rows, 256), lambda j, a: (j * m + a, 0, 0)),
        out_shape=jax.ShapeDtypeStruct((nj * m, rows, 256), BF16),
        compiler_params=_params(("arbitrary", "arbitrary")),
        name="s5_lane_perm",
    )(*xs, sel)


def _head_ones(width, head):
    i = jnp.arange(width) // head
    return (i[:, None] == i[None, :]).astype(BF16)


def _rw_prep_kernel(zr_ref, zrp_ref, zrn_ref, zk_ref, zkp_ref, zkn_ref, zv_ref, zvp_ref, zvn_ref,
                    zs_ref, zsp_ref, zsn_ref, mur_ref, muk_ref, muv_ref, mus_ref,
                    w0_ref, wup_ref, a0_ref, aup_ref, gup_ref, kk_ref, ka_ref, rk_ref, ones_ref,
                    r_o, kk_o, v_o, lw_o, kd_o, beta_o, g_o, bonus_o, *,
                    lora_w, lora_a, small0, n_ctx, l_ctx, l_lat, c_rw, quarter):
    tm, width = zr_ref.shape
    halo = zrp_ref.shape[0]
    g0 = pl.program_id(0) * tm
    is_ctx = g0 < n_ctx
    pos = jnp.where(is_ctx, g0 % l_ctx, (g0 - n_ctx) % l_lat) + lax.broadcasted_iota(jnp.int32, (tm, 1), 0)
    row = lax.broadcasted_iota(jnp.int32, (tm, 1), 0)
    ok_m1 = jnp.where(is_ctx, pos, pos % GRID_W) >= 1
    ok_p1 = jnp.where(is_ctx, l_ctx - 1 - pos, GRID_W - 1 - pos % GRID_W) >= 1
    ok_mw = jnp.where(is_ctx, 0, pos) >= GRID_W
    ok_pw = jnp.where(is_ctx, 0, l_lat - GRID_W - pos) >= 1

    def mix(cur, prev, nxt, mu, col0):
        cur, prev, nxt = cur.astype(F32), prev.astype(F32), nxt.astype(F32)
        width = cur.shape[1]
        col = col0 + lax.broadcasted_iota(jnp.int32, (1, width), 1)
        cq = col // quarter
        cls = jnp.where(is_ctx, cq // 2, cq)
        m1 = jnp.where(row == 0, prev[halo - 1:halo, :], pltpu.roll(cur, 1, axis=0))
        p1 = jnp.where(row == tm - 1, nxt[0:1, :], pltpu.roll(cur, tm - 1, axis=0))
        mw = jnp.concatenate([prev, cur[:tm - halo]], axis=0)
        pw = jnp.concatenate([cur[halo:], nxt], axis=0)
        zs = jnp.where(cls == 0, jnp.where(ok_m1, m1, 0.0),
                       jnp.where(cls == 1, jnp.where(ok_p1, p1, 0.0),
                                 jnp.where(cls == 2, jnp.where(ok_mw, mw, 0.0), jnp.where(ok_pw, pw, 0.0))))
        return cur + (zs - cur) * mu

    sm = mix(zs_ref[:, small0:], zsp_ref[:, small0:], zsn_ref[:, small0:], mus_ref[...], 3 * c_rw)
    o_a = 2 * lora_w
    o_g = o_a + 2 * lora_a
    ones = ones_ref[...]
    tn = ones.shape[0]
    gate_in = _sigmoid(sm[:, o_g:]).astype(BF16)
    wd = [jnp.tanh(sm[:, d * lora_w:(d + 1) * lora_w]) for d in range(2)]
    ad = [sm[:, o_a + d * lora_a:o_a + (d + 1) * lora_a] for d in range(2)]
    j0 = pl.program_id(1) * width
    for s in range(width // tn):
        cs = slice(s * tn, (s + 1) * tn)
        r = mix(zr_ref[:, cs], zrp_ref[:, cs], zrn_ref[:, cs], mur_ref[:, cs], j0 + s * tn)
        k = mix(zk_ref[:, cs], zkp_ref[:, cs], zkn_ref[:, cs], muk_ref[:, cs], c_rw + j0 + s * tn)
        v = mix(zv_ref[:, cs], zvp_ref[:, cs], zvn_ref[:, cs], muv_ref[:, cs], 2 * c_rw + j0 + s * tn)
        g_o[:, cs] = _dot(gate_in, gup_ref[:, cs]).astype(g_o.dtype)
        kx = k * kk_ref[:, cs]
        ss = _dot_exact_rhs(kx * kx, ones)
        kk = kx * lax.rsqrt(jnp.maximum(ss, 1e-24))
        k_sum = jnp.zeros_like(k)
        for d in range(2):
            w = w0_ref[d][:, cs] + _dot3(wd[d], wup_ref[d][:, cs])
            w = -(jnp.maximum(-w, 0.0) + jnp.log(1.0 + jnp.exp(-jnp.abs(w)))) - 0.5
            lw_o[d, :, cs] = -jnp.exp(w)
            a = _sigmoid(a0_ref[d][:, cs] + _dot3(ad[d], aup_ref[d][:, cs]))
            kd = k * (1.0 + (a - 1.0) * ka_ref[:, cs])
            kd_o[d, :, cs] = kd.astype(kd_o.dtype)
            beta_o[d, :, cs] = (kk * a).astype(beta_o.dtype)
            k_sum = k_sum + kd
        r_o[:, cs] = r.astype(r_o.dtype)
        kk_o[:, cs] = kk.astype(kk_o.dtype)
        v_o[:, cs] = v.astype(v_o.dtype)
        bonus_o[:, cs] = (_dot_exact_rhs(r * (k_sum * 0.5) * rk_ref[:, cs], ones) * v).astype(bonus_o.dtype)


def _rw_prep(z_rkv, z_small, lp, *, bsz, l_ctx, l_lat, c_rw, head, gate_rank):
    n = z_rkv.shape[0]
    pack = RW_PACK * head
    tn = _pick(c_rw, RW_PREP_WIDTH)
    nj = c_rw // tn
    halo = GRID_W
    tm = _pick(l_lat, _pick(l_ctx, RW_PREP_ROWS))
    assert tm % halo == 0 and tm > halo
    lora_w = lp["rw_w_up"].shape[1]
    lora_a = lp["rw_a_up"].shape[1]
    lora_g = lp["rw_g_up"].shape[0]
    mu = lp["rw_mu"].reshape(1, -1)
    n_small = 2 * lora_w + 2 * lora_a + lora_g
    n_small_all = gate_rank + n_small
    assert mu.shape[1] == 3 * c_rw + n_small and mu.shape[1] % 4 == 0 and n_small_all % 128 == 0
    mu_big = mu[:, :3 * c_rw]
    mu_small = mu[:, 3 * c_rw:]
    row = lambda a: a.reshape(1, c_rw)
    hb = tm // halo
    last_halo = n // halo - 1

    def tiles(off, width_blocks):
        w = tn if width_blocks else n_small_all
        col = (lambda j: j + off) if width_blocks else (lambda j: 0)
        return [pl.BlockSpec((tm, w), lambda i, j: (i, col(j))),
                pl.BlockSpec((halo, w), lambda i, j: (jnp.maximum(i * hb - 1, 0), col(j))),
                pl.BlockSpec((halo, w), lambda i, j: (jnp.minimum((i + 1) * hb, last_halo), col(j)))]

    vec = lambda off: pl.BlockSpec((1, tn), lambda i, j: (0, j + off))
    dirvec = pl.BlockSpec((2, 1, tn), lambda i, j: (0, 0, j))
    out_t = pl.BlockSpec((tm, tn), lambda i, j: (i, j))
    out_d = pl.BlockSpec((2, tm, tn), lambda i, j: (0, i, j))
    sds = jax.ShapeDtypeStruct
    kern = functools.partial(_rw_prep_kernel, lora_w=lora_w, lora_a=lora_a, small0=gate_rank, n_ctx=bsz * l_ctx,
                             l_ctx=l_ctx, l_lat=l_lat, c_rw=c_rw, quarter=mu.shape[1] // 4)
    return pl.pallas_call(
        kern,
        grid=(n // tm, nj),
        in_specs=[
            *tiles(0, True), *tiles(nj, True), *tiles(2 * nj, True), *tiles(0, False),
            vec(0), vec(nj), vec(2 * nj),
            pl.BlockSpec((1, n_small), lambda i, j: (0, 0)),
            dirvec,
            pl.BlockSpec((2, lora_w, tn), lambda i, j: (0, 0, j)),
            dirvec,
            pl.BlockSpec((2, lora_a, tn), lambda i, j: (0, 0, j)),
            pl.BlockSpec((lora_g, tn), lambda i, j: (0, j)),
            vec(0), vec(0), vec(0),
            pl.BlockSpec((pack, pack), lambda i, j: (0, 0)),
        ],
        out_specs=[out_t, out_t, out_t, out_d, out_d, out_d, out_t, out_t],
        out_shape=[sds((n, c_rw), BF16), sds((n, c_rw), BF16), sds((n, c_rw), BF16),
                   sds((2, n, c_rw), F32), sds((2, n, c_rw), BF16), sds((2, n, c_rw), BF16),
                   sds((n, c_rw), BF16), sds((n, c_rw), BF16)],
        compiler_params=_params(("arbitrary", "arbitrary")),
        name="rwkv_prep",
    )(*([z_rkv] * 9), *([z_small] * 3), mu_big, mu_big, mu_big, mu_small,
      lp["rw_w0"].reshape(2, 1, c_rw), lp["rw_w_up"], lp["rw_a0"].reshape(2, 1, c_rw), lp["rw_a_up"],
      lp["rw_g_up"].astype(BF16), row(lp["rw_k_k"]), row(lp["rw_k_a"]), row(lp["rw_r_k"]),
      _head_ones(pack, head))


def _rw_scan_kernel(rf_ref, kkf_ref, vf_ref, lwf_ref, kdf_ref, betaf_ref, rb_ref, kkb_ref, vb_ref, lwb_ref, kdb_ref,
                    betab_ref, yf_ref, yb_ref, st_scr, *, head):
    t_len = rf_ref.shape[0]
    n_pack = RW_PACK
    width = n_pack * head
    n_groups = rf_ref.shape[1] // width

    @pl.when(pl.program_id(1) == 0)
    def _():
        st_scr[...] = jnp.zeros_like(st_scr)

    lane_head = lax.broadcasted_iota(jnp.int32, (1, width), 1) // head
    head_masks = [lane_head == h for h in range(n_pack)]

    def block_diag(x):
        xb = x.astype(BF16)
        zero = jnp.zeros_like(xb)
        return jnp.concatenate([jnp.where(m, xb, zero) for m in head_masks], axis=0)

    def mm(a, bd):
        return _dot(a.astype(BF16), bd)

    def mm_nt(a, bd):
        return lax.dot_general(a.astype(BF16), bd, (((1,), (1,)), ((), ())), preferred_element_type=F32)

    tt = lax.broadcasted_iota(jnp.int32, (t_len, width), 0)
    ss = lax.broadcasted_iota(jnp.int32, (t_len, width), 1) % head
    eye = jnp.where(tt == ss, 1.0, 0.0)
    ti = lax.broadcasted_iota(jnp.int32, (t_len, t_len), 0)
    si = lax.broadcasted_iota(jnp.int32, (t_len, t_len), 1)
    strict_d = [tt > ss, tt < ss]
    incl_d = [tt >= ss, tt <= ss]
    tri_d = [jnp.where(ti >= si, 1.0, 0.0).astype(BF16), jnp.where(ti <= si, 1.0, 0.0).astype(BF16)]

    units = [(d, slice(g * width, (g + 1) * width)) for d in range(2) for g in range(n_groups)]
    each = lambda f, *xs: [f(*a) for a in zip(*xs)]
    pick = lambda fwd_ref, bwd_ref: [(bwd_ref if d else fwd_ref)[:, c] for d, c in units]
    r = pick(rf_ref, rb_ref)
    kk = pick(kkf_ref, kkb_ref)
    v = pick(vf_ref, vb_ref)
    lw = pick(lwf_ref, lwb_ref)
    kd = pick(kdf_ref, kdb_ref)
    beta = pick(betaf_ref, betab_ref)
    strict = [strict_d[d] for d, _ in units]
    incl = [incl_d[d] for d, _ in units]
    tri = [tri_d[d] for d, _ in units]

    def cumulative(x, tri_u):
        hi, lo = _split(x)
        return _dot(tri_u, hi) + _dot(tri_u, lo)

    log_p = each(cumulative, lw, tri)
    log_pt = each(lambda x: jnp.sum(x, axis=0, keepdims=True), lw)
    p_inv = each(lambda x: jnp.exp(-x), log_p)
    a_bar = each(lambda a, p, w: a * jnp.exp(p - w), kk, log_p, lw)
    r_bar = each(lambda a, p: a * jnp.exp(p), r, log_p)
    b_til = each(jnp.multiply, beta, p_inv)
    k_til = each(jnp.multiply, kd, p_inv)
    p_rest = each(lambda t, p: jnp.exp(t - p), log_pt, log_p)
    b_end = each(jnp.multiply, beta, p_rest)
    k_end = each(jnp.multiply, kd, p_rest)

    ar = each(lambda a, b: jnp.concatenate([a, b], axis=0), a_bar, r_bar)
    gram_b = each(lambda a, b: mm_nt(a, block_diag(b)), ar, b_til)
    gram_k = each(lambda a, b: mm_nt(a, block_diag(b)), ar, k_til)
    l_b = each(lambda g, m: jnp.where(m, g[:t_len], 0.0), gram_b, strict)
    m_b = each(lambda g, m: jnp.where(m, g[t_len:], 0.0), gram_b, incl)
    l_k = each(lambda g, m: jnp.where(m, g[:t_len], 0.0), gram_k, strict)
    m_k = each(lambda g, m: jnp.where(m, g[t_len:], 0.0), gram_k, incl)

    stack = lambda a, b: jnp.concatenate([a, b], axis=0)
    t_inv = each(lambda x: eye - x, l_b)
    l_pow = each(lambda x: mm(x, block_diag(x)), l_b)
    span = 4
    while span < t_len:
        both = each(lambda t, x: mm(stack(t, x), block_diag(x)), t_inv, l_pow)
        t_inv = each(lambda t, p: t + p[:t_len], t_inv, both)
        l_pow = each(lambda p: p[t_len:], both)
        span *= 2
    t_inv = each(lambda t, x: t + mm(t, block_diag(x)), t_inv, l_pow)

    def head_t(x):
        return jnp.concatenate([x[:, h * head:(h + 1) * head].T for h in range(n_pack)], axis=1)

    bd_v = each(block_diag, v)
    w_til = each(lambda t, a: mm(t, block_diag(a)), t_inv, a_bar)
    lmv = each(lambda lk, mk, ke, bv: mm(jnp.concatenate([lk, mk, head_t(ke)], axis=0), bv), l_k, m_k, k_end, bd_v)
    u_til = each(lambda t, a: mm(t, block_diag(a[:t_len])), t_inv, lmv)
    mb_bt = each(lambda m, b: stack(m, head_t(b)), m_b, b_end)
    from_w = each(lambda a, w: mm(a, block_diag(w)), mb_bt, w_til)
    from_u = each(lambda a, u: mm(a, block_diag(u)), mb_bt, u_til)
    q_hat = each(lambda a, fw: a - fw[:t_len], r_bar, from_w)
    y_hat = each(lambda a, fu: a[t_len:2 * t_len] - fu[:t_len], lmv, from_u)
    g_mat = each(lambda t, fw: eye * jnp.exp(t) - fw[t_len:], log_pt, from_w)
    h_mat = each(lambda a, fu: a[2 * t_len:] - fu[t_len:], lmv, from_u)

    bd_s = [block_diag(st_scr[d, :, c]) for d, c in units]
    from_state = each(lambda q, g, s: mm(stack(q, g), s), q_hat, g_mat, bd_s)
    for (d, c), fs, y0, h in zip(units, from_state, y_hat, h_mat):
        (yb_ref if d else yf_ref)[:, c] = (fs[:t_len] + y0).astype(yf_ref.dtype)
        st_scr[d, :, c] = fs[t_len:] + h


def _rw_scan(r, kk, v, lw, kd, beta, *, bsz, l_ctx, l_lat, head):
    n, c_rw = r.shape
    t_len = RW_CHUNK
    nc = l_ctx // t_len
    nl = l_lat // t_len
    ctx_blocks = bsz * nc

    def blk(d, b, c):
        cc = nc - 1 - c if d else c
        cl = nl - 1 - (c - nc) if d else c - nc
        return jnp.where(c < nc, b * nc + cc, ctx_blocks + b * nl + cl)

    shared = lambda d: pl.BlockSpec((t_len, c_rw), lambda b, c: (blk(d, b, c), 0))
    per_dir = lambda d: pl.BlockSpec((None, t_len, c_rw), lambda b, c: (d, blk(d, b, c), 0))
    side = lambda d: [shared(d), shared(d), shared(d), per_dir(d), per_dir(d), per_dir(d)]
    kern = functools.partial(_rw_scan_kernel, head=head)
    sds = jax.ShapeDtypeStruct
    return pl.pallas_call(
        kern,
        grid=(bsz, nc + nl),
        in_specs=side(0) + side(1),
        out_specs=[shared(0), shared(1)],
        out_shape=[sds((n, c_rw), BF16), sds((n, c_rw), BF16)],
        scratch_shapes=[pltpu.VMEM((2, head, c_rw), F32)],
        compiler_params=_params(("arbitrary", "arbitrary")),
        name="rwkv_scan",
    )(r, kk, v, lw, kd, beta, r, kk, v, lw, kd, beta)


def _rw_readout_kernel(yf_ref, yb_ref, g_ref, bonus_ref, lnw_ref, lnb_ref, ones_ref, o_ref, *, head):
    ones = ones_ref[...]
    pack = ones.shape[0]
    inv = 1.0 / head
    for s in range(o_ref.shape[1] // pack):
        cs = slice(s * pack, (s + 1) * pack)
        y = yf_ref[:, cs].astype(F32) + yb_ref[:, cs].astype(F32)
        mu = _dot_exact_rhs(y, ones) * inv
        yc = y - mu
        var = _dot_exact_rhs(yc * yc, ones) * inv
        yn = yc * lax.rsqrt(var + GN_EPS) * lnw_ref[:, cs] + lnb_ref[:, cs]
        o_ref[:, cs] = ((yn + bonus_ref[:, cs]) * g_ref[:, cs]).astype(o_ref.dtype)


def _rw_readout(y_fwd, y_bwd, g, bonus, ln_w, ln_b, *, head, tm):
    n, c_rw = y_fwd.shape
    pack = RW_PACK * head
    tn = _pick(c_rw, RW_PREP_WIDTH)
    tile = pl.BlockSpec((tm, tn), lambda i, j: (i, j))
    vec = pl.BlockSpec((1, tn), lambda i, j: (0, j))
    kern = functools.partial(_rw_readout_kernel, head=head)
    return pl.pallas_call(
        kern,
        grid=(n // tm, c_rw // tn),
        in_specs=[tile, tile, tile, tile, vec, vec, pl.BlockSpec((pack, pack), lambda i, j: (0, 0))],
        out_specs=tile,
        out_shape=jax.ShapeDtypeStruct((n, c_rw), BF16),
        compiler_params=_params(("arbitrary", "arbitrary")),
        name="rwkv_readout",
    )(y_fwd, y_bwd, g, bonus, ln_w.reshape(1, c_rw), ln_b.reshape(1, c_rw), _head_ones(pack, head))


def _final_norm_kernel(h_ref, gain_ref, o_ref):
    x = h_ref[...]
    ms = jnp.mean(x * x, axis=-1, keepdims=True)
    o_ref[...] = x * lax.rsqrt(ms + NORM_EPS) * gain_ref[...]


def _final_norm(h, gain, *, tm, row0):
    n, d = h.shape
    b0 = row0 // tm
    return pl.pallas_call(
        _final_norm_kernel,
        grid=((n - row0) // tm,),
        in_specs=[pl.BlockSpec((tm, d), lambda i: (i + b0, 0)), pl.BlockSpec((1, d), lambda i: (0, 0))],
        out_specs=pl.BlockSpec((tm, d), lambda i: (i, 0)),
        out_shape=jax.ShapeDtypeStruct((n - row0, d), F32),
        compiler_params=_params(("arbitrary",)),
        name="final_norm",
    )(h, gain.reshape(1, d))


def _to_s5_layout(u, bsz, l_ctx, l_lat, sel):
    t = S5_CHUNK
    n_ctx = bsz * l_ctx
    nj = u.shape[0]

    def tiles(x, length):
        x = x.reshape(nj, bsz, length // t, t * 256)
        return jnp.transpose(x, (0, 2, 1, 3)).reshape(nj, (length // t) * bsz, t * 256)

    return _lane_perm([tiles(u[:, :n_ctx], l_ctx), tiles(u[:, n_ctx:], l_lat)], sel)


def _from_s5_layout(y, bsz, l_ctx, l_lat, sel):
    t = S5_CHUNK
    nj, rows, _ = y.shape
    out = _lane_perm([y], sel).reshape(nj, t, rows, 256)
    rows_ctx = (l_ctx // t) * bsz

    def untile(o, length):
        o = o.reshape(nj, t, length // t, bsz, 256)
        return jnp.transpose(o, (3, 2, 1, 0, 4)).reshape(bsz * length, nj * 256)

    return untile(out[:, :, :rows_ctx], l_ctx), untile(out[:, :, rows_ctx:], l_lat)


def kernel(x, c, ctx, c_ctx, ada_down, ada_up, ada_b, norm1, w_in, s5_a_re, s5_a_im, s5_log_dt, s5_b_re, s5_b_im, s5_c_re, s5_c_im, s5_d, s5_glu_w, s5_glu_b, rw_mu, rw_w0, rw_w_up, rw_a0, rw_a_up, rw_g_up, rw_k_k, rw_k_a, rw_r_k, rw_ln_w, rw_ln_b, w_proj_a, w_proj_b, gate_up, gate_b, w_out, norm2, mlp_w1, mlp_w2, norm_f):
    bsz, l_lat, d = x.shape
    l_ctx = ctx.shape[1]
    depth = w_in.shape[0]
    n_ctx = bsz * l_ctx
    n_lat = bsz * l_lat
    s5_groups, s5_state = s5_a_re.shape[2], s5_a_re.shape[3]
    s5_gch = s5_b_re.shape[-1]
    c_s5 = s5_groups * s5_gch
    rw_heads, head = rw_r_k.shape[1], rw_r_k.shape[2]
    c_rw = rw_heads * head
    lora_w, lora_a, lora_g = rw_w_up.shape[2], rw_a_up.shape[2], rw_g_up.shape[1]
    gate_rank = gate_up.shape[2]
    n_small_rw = 2 * lora_w + 2 * lora_a + lora_g
    c_big = c_s5 + 3 * c_rw
    assert w_in.shape[2] == c_big + n_small_rw + gate_rank
    assert bsz % 8 == 0 and l_ctx % RW_CHUNK == 0 and l_lat % RW_CHUNK == 0 and l_lat % GRID_W == 0
    assert s5_gch * S5_CHUNK == 256 and RW_PACK * head == 256 and c_rw % 256 == 0 and gate_rank % 128 == 0
    assert RW_CHUNK == head
    assert c_s5 % IN_PROJ_TN == 0 and (3 * c_rw) % IN_PROJ_TN == 0 and n_small_rw + gate_rank <= IN_PROJ_TN

    tm = _pick(l_lat, _pick(n_ctx, TM))

    def row_of_block(i):
        ctx_blocks = n_ctx // tm
        return jnp.where(i < ctx_blocks, bsz, (i - ctx_blocks) // (l_lat // tm))

    n_cond = ((bsz + 1 + 7) // 8) * 8
    cond = jnp.zeros((n_cond, d), F32).at[:bsz].set(c).at[bsz].set(c_ctx)
    mods_all = _ada_modulation(cond, ada_down, ada_up, ada_b)
    mods_all = jnp.transpose(mods_all.reshape(depth, n_cond, 6, 1, d), (0, 2, 1, 3, 4))

    h = jnp.concatenate([ctx.reshape(n_ctx, d), x.reshape(n_lat, d)], axis=0)

    lane_sel = _lane_select_matrices(s5_gch)
    w_in_bf = _pack_w_in(w_in, c_big=c_big, n_small=n_small_rw, rank=gate_rank, tn=IN_PROJ_TN)
    glu_bf = _to_bf16(s5_glu_w)
    proj_a_bf, proj_b_bf, w_out_bf = _to_bf16(w_proj_a), _to_bf16(w_proj_b), _to_bf16(w_out)
    gate_up_bf = _to_bf16(gate_up.reshape(depth, 2 * gate_rank, d)).reshape(gate_up.shape)
    w1_bf, w2_bf = _to_bf16(mlp_w1), _to_bf16(mlp_w2)

    for l in range(depth):
        last = l == depth - 1
        row0 = n_ctx if last else 0
        mods = mods_all[l]
        lp = {"rw_mu": rw_mu[l], "rw_w0": rw_w0[l], "rw_w_up": rw_w_up[l], "rw_a0": rw_a0[l],
              "rw_a_up": rw_a_up[l], "rw_g_up": rw_g_up[l], "rw_k_k": rw_k_k[l], "rw_k_a": rw_k_a[l],
              "rw_r_k": rw_r_k[l]}
        z_u, z_rkv, z_small = _in_proj(h, norm1[l], mods, w_in_bf, l, row_of_block, tm=tm, tn=IN_PROJ_TN,
                                       c_u=c_s5, c_rkv=3 * c_rw)

        lag, bm, cmt, lam = _s5_matrices(s5_a_re[l], s5_a_im[l], s5_log_dt[l], s5_b_re[l], s5_b_im[l],
                                            s5_c_re[l], s5_c_im[l], s5_d[l])
        u_g = _to_s5_layout(z_u, bsz, l_ctx, l_lat, lane_sel)
        y_g = _s5_scan(u_g, lag, bm, cmt, lam, n_ctx=l_ctx // S5_CHUNK, n_chunks=(l_ctx + l_lat) // S5_CHUNK, bsz=bsz)
        y_ctx, y_lat = _from_s5_layout(y_g, bsz, l_ctx, l_lat, lane_sel)
        ya = _s5_glu(y_ctx, y_lat, glu_bf, l, s5_glu_b[l], tm=tm)

        r, kk, v, lw, kd, beta, g, bonus = _rw_prep(z_rkv, z_small, lp, bsz=bsz, l_ctx=l_ctx, l_lat=l_lat,
                                                    c_rw=c_rw, head=head, gate_rank=gate_rank)
        y_fwd, y_bwd = _rw_scan(r, kk, v, lw, kd, beta, bsz=bsz, l_ctx=l_ctx, l_lat=l_lat, head=head)
        yb = _rw_readout(y_fwd, y_bwd, g, bonus, rw_ln_w[l], rw_ln_b[l], head=head, tm=tm)

        h = _merge_out(ya, yb, z_small, proj_a_bf, proj_b_bf, gate_up_bf, gate_b[l], w_out_bf, l, h, mods,
                       row_of_block, tm=tm, row0=row0)
        h = _mlp(h, norm2[l], mods, w1_bf, w2_bf, l, row_of_block, tm=tm, row0=row0)

    out = _final_norm(h, norm_f, tm=tm, row0=n_ctx)
    return out.reshape(bsz, l_lat, d)
```

```python
import functools

import jax
import jax.numpy as jnp
from jax import lax
from jax.experimental import pallas as pl
from jax.experimental.pallas import tpu as pltpu

F32 = jnp.float32
BF16 = jnp.bfloat16

GRID_W = 64
NORM_EPS = 1e-6
GN_EPS = 64e-5
A_RE_MAX = -1e-4
S5_CHUNK = 16
RW_CHUNK = 64
RW_PACK = 4
RW_PREP_WIDTH = 1024
RW_PREP_ROWS = 512
TM = 512
IN_PROJ_TN = 1024
VMEM_LIMIT = 56 * 1024 * 1024


def _params(sem):
    return pltpu.CompilerParams(dimension_semantics=sem, vmem_limit_bytes=VMEM_LIMIT)


def _pick(n, pref):
    t = pref
    while n % t:
        t //= 2
    return t


def _dot(a, b):
    return jnp.dot(a, b, preferred_element_type=F32)


def _split(x):
    hi = x.astype(BF16)
    lo = (x - hi.astype(F32)).astype(BF16)
    return hi, lo


def _dot3(a, b):
    ah, al = _split(a)
    bh, bl = _split(b)
    return _dot(ah, bh) + (_dot(ah, bl) + _dot(al, bh))


def _dot_exact_rhs(a, b_bf16):
    ah, al = _split(a)
    return _dot(ah, b_bf16) + _dot(al, b_bf16)


def _sigmoid(x):
    return 1.0 / (1.0 + jnp.exp(-x))


def _gelu_tanh(x):
    c = 0.7978845608028654
    return 0.5 * x * (1.0 + jnp.tanh(c * (x + 0.044715 * (x * x * x))))


def _cast_kernel(w_ref, o_ref):
    o_ref[...] = w_ref[...].astype(o_ref.dtype)


def _to_bf16(w):
    nl, r, c = w.shape
    rows_4mib = max(16, (1 << 20) // c)
    tr = _pick(r, 1 << (rows_4mib.bit_length() - 1))
    out = pl.pallas_call(
        _cast_kernel,
        grid=(nl * r // tr,),
        in_specs=[pl.BlockSpec((tr, c), lambda i: (i, 0))],
        out_specs=pl.BlockSpec((tr, c), lambda i: (i, 0)),
        out_shape=jax.ShapeDtypeStruct((nl * r, c), BF16),
        compiler_params=_params(("arbitrary",)),
        name="weights_to_bf16",
    )(w.reshape(nl * r, c))
    return out.reshape(nl, r, c)


def _pack_w_in_kernel(w_ref, o_ref, *, c_big, n_small, rank):
    o_ref[:, :c_big] = w_ref[:, :c_big].astype(BF16)
    o_ref[:, c_big:c_big + rank] = w_ref[:, c_big + n_small:].astype(BF16)
    o_ref[:, c_big + rank:c_big + rank + n_small] = w_ref[:, c_big:c_big + n_small].astype(BF16)
    pad = o_ref.shape[1] - (c_big + rank + n_small)
    if pad:
        o_ref[:, c_big + rank + n_small:] = jnp.zeros((o_ref.shape[0], pad), BF16)


def _pack_w_in(w_in, *, c_big, n_small, rank, tn):
    nl, d, c_in = w_in.shape
    c_out = c_big + tn
    assert c_big % 128 == 0 and n_small % 128 == 0 and rank % 128 == 0 and rank + n_small <= tn
    tr = _pick(d, 64)
    out = pl.pallas_call(
        functools.partial(_pack_w_in_kernel, c_big=c_big, n_small=n_small, rank=rank),
        grid=(nl * d // tr,),
        in_specs=[pl.BlockSpec((tr, c_in), lambda i: (i, 0))],
        out_specs=pl.BlockSpec((tr, c_out), lambda i: (i, 0)),
        out_shape=jax.ShapeDtypeStruct((nl * d, c_out), BF16),
        compiler_params=_params(("arbitrary",)),
        name="pack_w_in",
    )(w_in.reshape(nl * d, c_in))
    return out.reshape(nl, d, c_out)


def _ada_kernel(cond_ref, wd_ref, wu_ref, b_ref, o_ref, mid_scr):
    @pl.when(pl.program_id(1) == 0)
    def _():
        c = cond_ref[...]
        mid_scr[...] = _dot3(c * _sigmoid(c), wd_ref[...])

    o_ref[...] = _dot3(mid_scr[...], wu_ref[...]) + b_ref[...]


def _ada_modulation(cond, w_down, w_up, bias):
    nl, d, rank = w_down.shape
    r = cond.shape[0]
    n_out = w_up.shape[-1]
    tn = _pick(n_out, 2048)
    return pl.pallas_call(
        _ada_kernel,
        grid=(nl, n_out // tn),
        in_specs=[
            pl.BlockSpec((r, d), lambda l, j: (0, 0)),
            pl.BlockSpec((None, d, rank), lambda l, j: (l, 0, 0)),
            pl.BlockSpec((None, rank, tn), lambda l, j: (l, 0, j)),
            pl.BlockSpec((None, 1, tn), lambda l, j: (l, 0, j)),
        ],
        out_specs=pl.BlockSpec((None, r, tn), lambda l, j: (l, 0, j)),
        out_shape=jax.ShapeDtypeStruct((nl, r, n_out), F32),
        scratch_shapes=[pltpu.VMEM((r, rank), F32)],
        compiler_params=_params(("arbitrary", "arbitrary")),
        name="ada_modulation",
    )(cond, w_down, w_up, bias.reshape(nl, 1, n_out))


def _in_proj_kernel(h_ref, gain_ref, shift_ref, scale_ref, w_ref, zu_ref, zrkv_ref, zs_ref, n_scr, *, ju, jr):
    j = pl.program_id(1)

    @pl.when(j == 0)
    def _():
        x = h_ref[...]
        ms = jnp.mean(x * x, axis=-1, keepdims=True)
        xn = x * lax.rsqrt(ms + NORM_EPS) * gain_ref[...]
        n_scr[...] = (xn * (1.0 + scale_ref[...]) + shift_ref[...]).astype(BF16)

    acc = _dot(n_scr[...], w_ref[...])

    @pl.when(j < ju)
    def _():
        for t in range(zu_ref.shape[0]):
            zu_ref[t] = acc[:, t * 256:(t + 1) * 256].astype(zu_ref.dtype)

    @pl.when(jnp.logical_and(j >= ju, j < ju + jr))
    def _():
        zrkv_ref[...] = acc.astype(zrkv_ref.dtype)

    @pl.when(j == ju + jr)
    def _():
        zs_ref[...] = acc


def _in_proj(h, gain, mods, w_all, layer, row_of_block, *, tm, tn, c_u, c_rkv):
    n, d = h.shape
    ju, jr = c_u // tn, c_rkv // tn
    assert w_all.shape[2] == (ju + jr + 1) * tn
    mod = lambda which: pl.BlockSpec((None, None, 1, d), lambda i, j: (which, row_of_block(i), 0, 0))
    sds = jax.ShapeDtypeStruct
    return pl.pallas_call(
        functools.partial(_in_proj_kernel, ju=ju, jr=jr),
        grid=(n // tm, ju + jr + 1),
        in_specs=[
            pl.BlockSpec((tm, d), lambda i, j: (i, 0), pipeline_mode=pl.Buffered(1)),
            pl.BlockSpec((1, d), lambda i, j: (0, 0)),
            mod(0), mod(1),
            pl.BlockSpec((None, d, tn), lambda i, j: (layer, 0, j)),
        ],
        out_specs=[pl.BlockSpec((tn // 256, tm, 256), lambda i, j: (jnp.minimum(j, ju - 1), i, 0)),
                   pl.BlockSpec((None, tm, tn), lambda i, j: (jnp.clip(j - ju, 0, jr - 1) // (jr // 3), i,
                                                              jnp.clip(j - ju, 0, jr - 1) % (jr // 3))),
                   pl.BlockSpec((tm, tn), lambda i, j: (i, 0))],
        out_shape=[sds((c_u // 256, n, 256), BF16), sds((3, n, c_rkv // 3), BF16), sds((n, tn), F32)],
        scratch_shapes=[pltpu.VMEM((tm, d), BF16)],
        compiler_params=_params(("arbitrary", "arbitrary")),
        name="in_proj",
    )(h, gain.reshape(1, d), mods, mods, w_all)


def _mlp_kernel(h_ref, gain_ref, shift_ref, scale_ref, gate_ref, w1_ref, w2_ref, o_ref, n_scr):
    j = pl.program_id(1)

    @pl.when(j == 0)
    def _():
        x = h_ref[...]
        ms = jnp.mean(x * x, axis=-1, keepdims=True)
        xn = x * lax.rsqrt(ms + NORM_EPS) * gain_ref[...]
        n_scr[...] = (xn * (1.0 + scale_ref[...]) + shift_ref[...]).astype(BF16)
        o_ref[...] = jnp.zeros_like(o_ref)

    hid = jnp.square(jnp.maximum(_dot(n_scr[...], w1_ref[...]), 0.0)).astype(BF16)
    o_ref[...] += _dot(hid, w2_ref[...])

    @pl.when(j == pl.num_programs(1) - 1)
    def _():
        o_ref[...] = h_ref[...] + gate_ref[...] * o_ref[...]


def _mlp(h, gain, mods, w1, w2, layer, row_of_block, *, tm, row0):
    n, d = h.shape
    d_ff = w1.shape[2]
    tf = _pick(d_ff, 512)
    b0 = row0 // tm
    mod = lambda which: pl.BlockSpec((None, None, 1, d), lambda i, j: (which, row_of_block(i + b0), 0, 0))
    return pl.pallas_call(
        _mlp_kernel,
        grid=((n - row0) // tm, d_ff // tf),
        in_specs=[
            pl.BlockSpec((tm, d), lambda i, j: (i + b0, 0), pipeline_mode=pl.Buffered(1)),
            pl.BlockSpec((1, d), lambda i, j: (0, 0)),
            mod(3), mod(4), mod(5),
            pl.BlockSpec((None, d, tf), lambda i, j: (layer, 0, j)),
            pl.BlockSpec((None, tf, d), lambda i, j: (layer, j, 0)),
        ],
        out_specs=pl.BlockSpec((tm, d), lambda i, j: (i + b0, 0)),
        out_shape=jax.ShapeDtypeStruct((n, d), F32),
        scratch_shapes=[pltpu.VMEM((tm, d), BF16)],
        input_output_aliases={0: 0},
        compiler_params=_params(("arbitrary", "arbitrary")),
        name="mlp",
    )(h, gain.reshape(1, d), mods, mods, mods, w1, w2)


def _merge_out_kernel(ya_ref, yb_ref, gz_ref, wa_ref, wb_ref, ga_ref, gb_ref, ba_ref, bb_ref, wo_ref, h_ref,
                      gate_ref, o_ref):
    j = pl.program_id(1)

    @pl.when(j == 0)
    def _():
        o_ref[...] = jnp.zeros_like(o_ref)

    gz = gz_ref[...].astype(BF16)
    ga = _sigmoid(_dot(gz, ga_ref[...]) + ba_ref[...])
    gb = _sigmoid(_dot(gz, gb_ref[...]) + bb_ref[...])
    mixed = (ga * _dot(ya_ref[...], wa_ref[...]) + gb * _dot(yb_ref[...], wb_ref[...])).astype(BF16)
    o_ref[...] += _dot(mixed, wo_ref[...])

    @pl.when(j == pl.num_programs(1) - 1)
    def _():
        o_ref[...] = h_ref[...] + gate_ref[...] * o_ref[...]


def _merge_out(ya, yb, z_small, wa, wb, gate_up, gate_b, w_out, layer, h, mods, row_of_block, *, tm, row0):
    n, d = h.shape
    ca, cb = ya.shape[1], yb.shape[1]
    d_mid = wa.shape[2]
    rank = gate_up.shape[2]
    tn = _pick(d_mid, 512)
    b0 = row0 // tm
    once = pl.Buffered(1)
    return pl.pallas_call(
        _merge_out_kernel,
        grid=((n - row0) // tm, d_mid // tn),
        in_specs=[
            pl.BlockSpec((tm, ca), lambda i, j: (i + b0, 0), pipeline_mode=once),
            pl.BlockSpec((tm, cb), lambda i, j: (i + b0, 0), pipeline_mode=once),
            pl.BlockSpec((tm, rank), lambda i, j: (i + b0, 0)),
            pl.BlockSpec((None, ca, tn), lambda i, j: (layer, 0, j)),
            pl.BlockSpec((None, cb, tn), lambda i, j: (layer, 0, j)),
            pl.BlockSpec((None, None, rank, tn), lambda i, j: (layer, 0, 0, j)),
            pl.BlockSpec((None, None, rank, tn), lambda i, j: (layer, 1, 0, j)),
            pl.BlockSpec((None, 1, tn), lambda i, j: (0, 0, j)),
            pl.BlockSpec((None, 1, tn), lambda i, j: (1, 0, j)),
            pl.BlockSpec((None, tn, d), lambda i, j: (layer, j, 0)),
            pl.BlockSpec((tm, d), lambda i, j: (i + b0, 0), pipeline_mode=once),
            pl.BlockSpec((None, None, 1, d), lambda i, j: (2, row_of_block(i + b0), 0, 0)),
        ],
        out_specs=pl.BlockSpec((tm, d), lambda i, j: (i + b0, 0)),
        out_shape=jax.ShapeDtypeStruct((n, d), F32),
        input_output_aliases={10: 0},
        compiler_params=_params(("arbitrary", "arbitrary")),
        name="merge_out_proj",
    )(ya, yb, z_small, wa, wb, gate_up, gate_up, gate_b.reshape(2, 1, d_mid), gate_b.reshape(2, 1, d_mid),
      w_out, h, mods)


def _s5_matrices(a_re, a_im, log_dt, b_re, b_im, c_re, c_im, d_skip):
    t_len = S5_CHUNK
    lam_re = jnp.minimum(a_re.astype(F32), A_RE_MAX)
    lam_im = a_im.astype(F32)
    dt = jnp.exp(log_dt.astype(F32))[..., None]
    j = jnp.arange(t_len + 1, dtype=F32)[:, None, None, None]
    mag = jnp.exp(j * (lam_re * dt))
    ang = j * (lam_im * dt)
    pw_re = mag * jnp.cos(ang)
    pw_im = mag * jnp.sin(ang)
    nr, ni = pw_re[1] - 1.0, pw_im[1]
    den = lam_re * lam_re + lam_im * lam_im
    qr = (nr * lam_re + ni * lam_im) / den
    qi = (ni * lam_re - nr * lam_im) / den
    bbr = qr[..., None] * b_re - qi[..., None] * b_im
    bbi = qr[..., None] * b_im + qi[..., None] * b_re
    n_dir, g, p, c = bbr.shape
    bbr_t = jnp.transpose(bbr, (0, 1, 3, 2))
    bbi_t = jnp.transpose(bbi, (0, 1, 3, 2))
    per_group = lambda z: jnp.transpose(z, (1, 0, 2, 3))
    lag = {"c_re": per_group(c_re.astype(F32)), "c_im": per_group(c_im.astype(F32)),
           "pw_re": jnp.transpose(pw_re[:t_len], (2, 1, 0, 3)), "pw_im": jnp.transpose(pw_im[:t_len], (2, 1, 0, 3)),
           "b_re": per_group(bbr_t), "b_im": per_group(bbi_t)}
    skip = d_skip.astype(F32).reshape(g, c)
    lag["skip"] = skip[:, :, None] * (jnp.arange(t_len * c)[None, None, :] == jnp.arange(c)[None, :, None])

    def powers(d, e):
        return jnp.transpose(pw_re[e, d], (1, 0, 2)), jnp.transpose(pw_im[e, d], (1, 0, 2))

    lanes4 = lambda a, b, cc, dd: jnp.concatenate([a, b, cc, dd], axis=-1)

    pf_r, pf_i = powers(0, (t_len - 1) - jnp.arange(t_len))
    pb_r, pb_i = powers(1, jnp.arange(t_len))
    bm = (lanes4(pf_r, pb_r, pf_r, pb_r)[:, :, None, :] * lanes4(bbr_t[0], bbr_t[1], bbi_t[0], bbi_t[1])[:, None]
          + lanes4(-pf_i, -pb_i, pf_i, pb_i)[:, :, None, :] * lanes4(bbi_t[0], bbi_t[1], bbr_t[0], bbr_t[1])[:, None])
    bm = bm.reshape(g, t_len * c, 4 * p)

    qf_r, qf_i = powers(0, jnp.arange(t_len) + 1)
    qb_r, qb_i = powers(1, t_len - jnp.arange(t_len))
    cmt = (lanes4(c_re[0], c_re[1], c_re[0], c_re[1])[:, None] * lanes4(qf_r, qb_r, -qf_i, -qb_i)[:, :, None, :]
           + lanes4(c_im[0], c_im[1], c_im[0], c_im[1])[:, None] * lanes4(-qf_i, -qb_i, -qf_r, -qb_r)[:, :, None, :])
    cmt = cmt.reshape(g, t_len * c, 4 * p)
    lam = jnp.stack([jnp.concatenate([pw_re[t_len, 0], pw_re[t_len, 1]], axis=-1),
                     jnp.concatenate([pw_im[t_len, 0], pw_im[t_len, 1]], axis=-1)], axis=1)
    return lag, bm.astype(BF16), cmt.astype(BF16), lam


def _s5_kernel(u_ref, bm_ref, cre_ref, cim_ref, pwr_ref, pwi_ref, bre_ref, bim_ref, skip_ref, cmt_ref, lam_ref, y_ref,
               xin_scr, fwd_scr, bwd_scr, km_scr, *, n_ctx, n_chunks, bsz, p_state):
    u = u_ref[...]
    xin_scr[...] = _dot(u, bm_ref[...])
    gch, tc = skip_ref.shape
    n_lag = tc // gch

    def nt3(a, b):
        nt = lambda x, y: lax.dot_general(x, y, (((1,), (1,)), ((), ())), preferred_element_type=F32)
        ah, al = _split(a)
        bh, bl = _split(b)
        return nt(ah, bh) + (nt(ah, bl) + nt(al, bh))

    def lag_kernels(d, reverse):
        cr, ci = cre_ref[d], cim_ref[d]
        rows_r, rows_i = [], []
        for m in range(n_lag):
            j = n_lag - 1 - m if reverse else m
            pr, pi = pwr_ref[d, j:j + 1, :], pwi_ref[d, j:j + 1, :]
            rows_r.append(cr * pr - ci * pi)
            rows_i.append(cr * pi + ci * pr)
        return (nt3(bre_ref[d], jnp.concatenate(rows_r, axis=0))
                - nt3(bim_ref[d], jnp.concatenate(rows_i, axis=0)))

    kf = lag_kernels(0, False) + skip_ref[...]
    kb = lag_kernels(1, True)
    lane_k = lax.broadcasted_iota(jnp.int32, (gch, tc), 1)
    for s in range(tc // gch):
        fwd = kf if s == 0 else jnp.where(lane_k >= gch * s, pltpu.roll(kf, gch * s, axis=1), 0.0)
        shift = (gch * (s + 1)) % tc
        bwd = kb if shift == 0 else jnp.where(lane_k < gch * (s + 1), pltpu.roll(kb, shift, axis=1), 0.0)
        km_scr[s * gch:(s + 1) * gch, :] = (fwd + bwd).astype(BF16)
    two_p = 2 * p_state
    lam_r = jnp.broadcast_to(lam_ref[0:1, :], (bsz, two_p))
    lam_i = jnp.broadcast_to(lam_ref[1:2, :], (bsz, two_p))
    is_fwd = lax.broadcasted_iota(jnp.int32, (bsz, two_p), 1) < p_state

    def body(i, carry):
        re, im = carry
        pb = jnp.where(i < n_ctx, n_ctx - 1 - i, n_chunks + n_ctx - 1 - i)
        rf = pl.multiple_of(i * bsz, bsz)
        rb = pl.multiple_of(pb * bsz, bsz)
        xf = xin_scr[pl.ds(rf, bsz), :]
        xb = xin_scr[pl.ds(rb, bsz), :]
        x_re = jnp.where(is_fwd, xf[:, :two_p], xb[:, :two_p])
        x_im = jnp.where(is_fwd, xf[:, two_p:], xb[:, two_p:])
        state = jnp.concatenate([re, im], axis=1)
        fwd_scr[pl.ds(rf, bsz), :] = state
        bwd_scr[pl.ds(rb, bsz), :] = state
        return (re * lam_r - im * lam_i + x_re, re * lam_i + im * lam_r + x_im)

    zero = jnp.zeros((bsz, two_p), F32)
    lax.fori_loop(0, n_chunks, body, (zero, zero))
    lane = lax.broadcasted_iota(jnp.int32, (1, 2 * two_p), 1)
    fwd_lane = (lane % two_p) < p_state
    xs = jnp.where(fwd_lane, fwd_scr[...], bwd_scr[...]).astype(BF16)
    from_state = lax.dot_general(xs, cmt_ref[...], (((1,), (1,)), ((), ())), preferred_element_type=F32)
    y_ref[...] = (_dot(u, km_scr[...]) + from_state).astype(y_ref.dtype)


def _s5_scan(u_g, lag, bm, cmt, lam, *, n_ctx, n_chunks, bsz):
    g, rows, tc = u_g.shape
    gch = lag["skip"].shape[1]
    m = 256 // gch
    p4 = bm.shape[-1]
    factor = lambda a: pl.BlockSpec((None,) + a.shape[1:], lambda i: (i, 0, 0, 0))
    kern = functools.partial(_s5_kernel, n_ctx=n_ctx, n_chunks=n_chunks, bsz=bsz, p_state=p4 // 4)
    return pl.pallas_call(
        kern,
        grid=(g,),
        in_specs=[
            pl.BlockSpec((None, rows, tc), lambda i: (i, 0, 0)),
            pl.BlockSpec((None, tc, p4), lambda i: (i, 0, 0)),
            factor(lag["c_re"]), factor(lag["c_im"]), factor(lag["pw_re"]), factor(lag["pw_im"]),
            factor(lag["b_re"]), factor(lag["b_im"]),
            pl.BlockSpec((None, gch, tc), lambda i: (i, 0, 0)),
            pl.BlockSpec((None, tc, p4), lambda i: (i, 0, 0)),
            pl.BlockSpec((None, 2, p4 // 2), lambda i: (i, 0, 0)),
        ],
        out_specs=pl.BlockSpec((None, rows, tc), lambda i: (i // m, 0, i % m)),
        out_shape=jax.ShapeDtypeStruct((g // m, rows, m * tc), BF16),
        scratch_shapes=[pltpu.VMEM((rows, p4), F32), pltpu.VMEM((rows, p4), F32), pltpu.VMEM((rows, p4), F32),
                        pltpu.VMEM((tc, tc), BF16)],
        compiler_params=_params(("arbitrary",)),
        name="s5_scan",
    )(u_g, bm, lag["c_re"], lag["c_im"], lag["pw_re"], lag["pw_im"], lag["b_re"], lag["b_im"], lag["skip"], cmt, lam)


def _glu_kernel(yc_ref, yl_ref, w_ref, b_ref, o_ref, *, ctx_blocks):
    def glu(y):
        z = _gelu_tanh(y.astype(F32))
        o_ref[...] = (z * _sigmoid(_dot(z.astype(BF16), w_ref[...]) + b_ref[...])).astype(o_ref.dtype)

    is_ctx = pl.program_id(0) < ctx_blocks

    @pl.when(is_ctx)
    def _():
        glu(yc_ref[...])

    @pl.when(jnp.logical_not(is_ctx))
    def _():
        glu(yl_ref[...])


def _s5_glu(y_ctx, y_lat, w, layer, b, *, tm):
    c = y_ctx.shape[1]
    cb = y_ctx.shape[0] // tm
    n = y_ctx.shape[0] + y_lat.shape[0]
    return pl.pallas_call(
        functools.partial(_glu_kernel, ctx_blocks=cb),
        grid=(n // tm,),
        in_specs=[
            pl.BlockSpec((tm, c), lambda i: (jnp.minimum(i, cb - 1), 0)),
            pl.BlockSpec((tm, c), lambda i: (jnp.maximum(i - cb, 0), 0)),
            pl.BlockSpec((None, c, c), lambda i: (layer, 0, 0)),
            pl.BlockSpec((1, c), lambda i: (0, 0)),
        ],
        out_specs=pl.BlockSpec((tm, c), lambda i: (i, 0)),
        out_shape=jax.ShapeDtypeStruct((n, c), BF16),
        compiler_params=_params(("arbitrary",)),
        name="s5_glu",
    )(y_ctx, y_lat, w, b.reshape(1, c))


def _lane_select_matrices(gch):
    m = 256 // gch
    a = jnp.arange(m)[:, None, None, None]
    b = jnp.arange(m)[None, :, None, None]
    k = jnp.arange(256)[None, None, :, None]
    n = jnp.arange(256)[None, None, None, :]
    sel = (k == a * gch + n % gch) & (n // gch == b)
    return sel.astype(BF16).reshape(m, m * 256, 256)


def _lane_perm_kernel(*refs):
    *x_refs, sel_ref, o_ref = refs
    off = 0
    for x_ref in x_refs:
        rows = x_ref.shape[0]
        o_ref[off:off + rows, :] = _dot(x_ref[...], sel_ref[...]).astype(o_ref.dtype)
        off += rows


def _lane_perm(xs, sel):
    nj = xs[0].shape[0]
    m, kdim, _ = sel.shape
    rows = sum(x.shape[1] for x in xs)
    return pl.pallas_call(
        _lane_perm_kernel,
        grid=(nj, m),
        in_specs=[pl.BlockSpec((None, x.shape[1], kdim), lambda j, a: (j, 0, 0)) for x in xs]
        + [pl.BlockSpec((None, kdim, 256), lambda j, a: (a, 0, 0))],
        out_specs=pl.BlockSpec((None, rows, 256), lambda j, a: (j * m + a, 0, 0)),
        out_shape=jax.ShapeDtypeStruct((nj * m, rows, 256), BF16),
        compiler_params=_params(("arbitrary", "arbitrary")),
        name="s5_lane_perm",
    )(*xs, sel)


def _head_ones(width, head):
    i = jnp.arange(width) // head
    return (i[:, None] == i[None, :]).astype(BF16)


def _rw_prep_kernel(z3_ref, z3p_ref, z3n_ref, zs_ref, zsp_ref, zsn_ref, mur_ref, muk_ref, muv_ref, mus_ref,
                    w0_ref, wup_ref, a0_ref, aup_ref, gup_ref, kk_ref, ka_ref, rk_ref, ones_ref,
                    r_o, kk_o, v_o, lw_o, kd_o, beta_o, g_o, bonus_o, *,
                    lora_w, lora_a, small0, n_ctx, l_ctx, l_lat, c_rw, quarter):
    _, tm, width = z3_ref.shape
    halo = z3p_ref.shape[1]
    g0 = pl.program_id(0) * tm
    is_ctx = g0 < n_ctx
    pos = jnp.where(is_ctx, g0 % l_ctx, (g0 - n_ctx) % l_lat) + lax.broadcasted_iota(jnp.int32, (tm, 1), 0)
    row = lax.broadcasted_iota(jnp.int32, (tm, 1), 0)
    ok_m1 = jnp.where(is_ctx, pos, pos % GRID_W) >= 1
    ok_p1 = jnp.where(is_ctx, l_ctx - 1 - pos, GRID_W - 1 - pos % GRID_W) >= 1
    ok_mw = jnp.where(is_ctx, 0, pos) >= GRID_W
    ok_pw = jnp.where(is_ctx, 0, l_lat - GRID_W - pos) >= 1

    def mix(cur, prev, nxt, mu, col0):
        cur, prev, nxt = cur.astype(F32), prev.astype(F32), nxt.astype(F32)
        width = cur.shape[1]
        col = col0 + lax.broadcasted_iota(jnp.int32, (1, width), 1)
        cq = col // quarter
        cls = jnp.where(is_ctx, cq // 2, cq)
        m1 = jnp.where(row == 0, prev[halo - 1:halo, :], pltpu.roll(cur, 1, axis=0))
        p1 = jnp.where(row == tm - 1, nxt[0:1, :], pltpu.roll(cur, tm - 1, axis=0))
        mw = jnp.concatenate([prev, cur[:tm - halo]], axis=0)
        pw = jnp.concatenate([cur[halo:], nxt], axis=0)
        zs = jnp.where(cls == 0, jnp.where(ok_m1, m1, 0.0),
                       jnp.where(cls == 1, jnp.where(ok_p1, p1, 0.0),
                                 jnp.where(cls == 2, jnp.where(ok_mw, mw, 0.0), jnp.where(ok_pw, pw, 0.0))))
        return cur + (zs - cur) * mu

    sm = mix(zs_ref[:, small0:], zsp_ref[:, small0:], zsn_ref[:, small0:], mus_ref[...], 3 * c_rw)
    o_a = 2 * lora_w
    o_g = o_a + 2 * lora_a
    ones = ones_ref[...]
    tn = ones.shape[0]
    gate_in = _sigmoid(sm[:, o_g:]).astype(BF16)
    wd = [jnp.tanh(sm[:, d * lora_w:(d + 1) * lora_w]) for d in range(2)]
    ad = [sm[:, o_a + d * lora_a:o_a + (d + 1) * lora_a] for d in range(2)]
    j0 = pl.program_id(1) * width
    for s in range(width // tn):
        cs = slice(s * tn, (s + 1) * tn)
        r = mix(z3_ref[0, :, cs], z3p_ref[0, :, cs], z3n_ref[0, :, cs], mur_ref[:, cs], j0 + s * tn)
        k = mix(z3_ref[1, :, cs], z3p_ref[1, :, cs], z3n_ref[1, :, cs], muk_ref[:, cs], c_rw + j0 + s * tn)
        v = mix(z3_ref[2, :, cs], z3p_ref[2, :, cs], z3n_ref[2, :, cs], muv_ref[:, cs], 2 * c_rw + j0 + s * tn)
        g_o[:, cs] = _dot(gate_in, gup_ref[:, cs]).astype(g_o.dtype)
        kx = k * kk_ref[:, cs]
        ss = _dot_exact_rhs(kx * kx, ones)
        kk = kx * lax.rsqrt(jnp.maximum(ss, 1e-24))
        k_sum = jnp.zeros_like(k)
        for d in range(2):
            w = w0_ref[d][:, cs] + _dot3(wd[d], wup_ref[d][:, cs])
            w = -(jnp.maximum(-w, 0.0) + jnp.log(1.0 + jnp.exp(-jnp.abs(w)))) - 0.5
            lw_o[d, :, cs] = -jnp.exp(w)
            a = _sigmoid(a0_ref[d][:, cs] + _dot3(ad[d], aup_ref[d][:, cs]))
            kd = k * (1.0 + (a - 1.0) * ka_ref[:, cs])
            kd_o[d, :, cs] = kd.astype(kd_o.dtype)
            beta_o[d, :, cs] = (kk * a).astype(beta_o.dtype)
            k_sum = k_sum + kd
        r_o[:, cs] = r.astype(r_o.dtype)
        kk_o[:, cs] = kk.astype(kk_o.dtype)
        v_o[:, cs] = v.astype(v_o.dtype)
        bonus_o[:, cs] = (_dot_exact_rhs(r * (k_sum * 0.5) * rk_ref[:, cs], ones) * v).astype(bonus_o.dtype)


def _rw_prep(z_rkv, z_small, lp, *, bsz, l_ctx, l_lat, c_rw, head, gate_rank):
    n = z_rkv.shape[1]
    pack = RW_PACK * head
    tn = _pick(c_rw, RW_PREP_WIDTH)
    nj = c_rw // tn
    halo = GRID_W
    tm = _pick(l_lat, _pick(l_ctx, RW_PREP_ROWS))
    assert tm % halo == 0 and tm > halo
    lora_w = lp["rw_w_up"].shape[1]
    lora_a = lp["rw_a_up"].shape[1]
    lora_g = lp["rw_g_up"].shape[0]
    mu = lp["rw_mu"].reshape(1, -1)
    n_small = 2 * lora_w + 2 * lora_a + lora_g
    n_small_all = gate_rank + n_small
    assert mu.shape[1] == 3 * c_rw + n_small and mu.shape[1] % 4 == 0 and n_small_all % 128 == 0
    mu_big = mu[:, :3 * c_rw]
    mu_small = mu[:, 3 * c_rw:]
    row = lambda a: a.reshape(1, c_rw)
    hb = tm // halo
    last_halo = n // halo - 1

    def tiles(stacked):
        prev = lambda i: jnp.maximum(i * hb - 1, 0)
        nxt = lambda i: jnp.minimum((i + 1) * hb, last_halo)
        if stacked:
            return [pl.BlockSpec((3, tm, tn), lambda i, j: (0, i, j)),
                    pl.BlockSpec((3, halo, tn), lambda i, j: (0, prev(i), j)),
                    pl.BlockSpec((3, halo, tn), lambda i, j: (0, nxt(i), j))]
        return [pl.BlockSpec((tm, n_small_all), lambda i, j: (i, 0)),
                pl.BlockSpec((halo, n_small_all), lambda i, j: (prev(i), 0)),
                pl.BlockSpec((halo, n_small_all), lambda i, j: (nxt(i), 0))]

    vec = lambda off: pl.BlockSpec((1, tn), lambda i, j: (0, j + off))
    dirvec = pl.BlockSpec((2, 1, tn), lambda i, j: (0, 0, j))
    out_t = pl.BlockSpec((tm, tn), lambda i, j: (i, j))
    out_d = pl.BlockSpec((2, tm, tn), lambda i, j: (0, i, j))
    sds = jax.ShapeDtypeStruct
    kern = functools.partial(_rw_prep_kernel, lora_w=lora_w, lora_a=lora_a, small0=gate_rank, n_ctx=bsz * l_ctx,
                             l_ctx=l_ctx, l_lat=l_lat, c_rw=c_rw, quarter=mu.shape[1] // 4)
    return pl.pallas_call(
        kern,
        grid=(n // tm, nj),
        in_specs=[
            *tiles(True), *tiles(False),
            vec(0), vec(nj), vec(2 * nj),
            pl.BlockSpec((1, n_small), lambda i, j: (0, 0)),
            dirvec,
            pl.BlockSpec((2, lora_w, tn), lambda i, j: (0, 0, j)),
            dirvec,
            pl.BlockSpec((2, lora_a, tn), lambda i, j: (0, 0, j)),
            pl.BlockSpec((lora_g, tn), lambda i, j: (0, j)),
            vec(0), vec(0), vec(0),
            pl.BlockSpec((pack, pack), lambda i, j: (0, 0)),
        ],
        out_specs=[out_t, out_t, out_t, out_d, out_d, out_d, out_t, out_t],
        out_shape=[sds((n, c_rw), BF16), sds((n, c_rw), BF16), sds((n, c_rw), BF16),
                   sds((2, n, c_rw), F32), sds((2, n, c_rw), BF16), sds((2, n, c_rw), BF16),
                   sds((n, c_rw), BF16), sds((n, c_rw), BF16)],
        compiler_params=_params(("arbitrary", "arbitrary")),
        name="rwkv_prep",
    )(*([z_rkv] * 3), *([z_small] * 3), mu_big, mu_big, mu_big, mu_small,
      lp["rw_w0"].reshape(2, 1, c_rw), lp["rw_w_up"], lp["rw_a0"].reshape(2, 1, c_rw), lp["rw_a_up"],
      lp["rw_g_up"].astype(BF16), row(lp["rw_k_k"]), row(lp["rw_k_a"]), row(lp["rw_r_k"]),
      _head_ones(pack, head))


def _rw_scan_kernel(rf_ref, kkf_ref, vf_ref, lwf_ref, kdf_ref, betaf_ref, rb_ref, kkb_ref, vb_ref, lwb_ref, kdb_ref,
                    betab_ref, yf_ref, yb_ref, st_scr, *, head):
    t_len = rf_ref.shape[0]
    n_pack = RW_PACK
    width = n_pack * head
    n_groups = rf_ref.shape[1] // width

    @pl.when(pl.program_id(1) == 0)
    def _():
        st_scr[...] = jnp.zeros_like(st_scr)

    lane_head = lax.broadcasted_iota(jnp.int32, (1, width), 1) // head
    head_masks = [lane_head == h for h in range(n_pack)]

    def block_diag(x):
        xb = x.astype(BF16)
        zero = jnp.zeros_like(xb)
        return jnp.concatenate([jnp.where(m, xb, zero) for m in head_masks], axis=0)

    def mm(a, bd):
        return _dot(a.astype(BF16), bd)

    def mm_nt(a, bd):
        return lax.dot_general(a.astype(BF16), bd, (((1,), (1,)), ((), ())), preferred_element_type=F32)

    tt = lax.broadcasted_iota(jnp.int32, (t_len, width), 0)
    ss = lax.broadcasted_iota(jnp.int32, (t_len, width), 1) % head
    eye = jnp.where(tt == ss, 1.0, 0.0)
    ti = lax.broadcasted_iota(jnp.int32, (t_len, t_len), 0)
    si = lax.broadcasted_iota(jnp.int32, (t_len, t_len), 1)
    strict_d = [tt > ss, tt < ss]
    incl_d = [tt >= ss, tt <= ss]
    tri_d = [jnp.where(ti >= si, 1.0, 0.0).astype(BF16), jnp.where(ti <= si, 1.0, 0.0).astype(BF16)]

    units = [(d, slice(g * width, (g + 1) * width)) for d in range(2) for g in range(n_groups)]
    each = lambda f, *xs: [f(*a) for a in zip(*xs)]
    pick = lambda fwd_ref, bwd_ref: [(bwd_ref if d else fwd_ref)[:, c] for d, c in units]
    r = pick(rf_ref, rb_ref)
    kk = pick(kkf_ref, kkb_ref)
    v = pick(vf_ref, vb_ref)
    lw = pick(lwf_ref, lwb_ref)
    kd = pick(kdf_ref, kdb_ref)
    beta = pick(betaf_ref, betab_ref)
    strict = [strict_d[d] for d, _ in units]
    incl = [incl_d[d] for d, _ in units]
    tri = [tri_d[d] for d, _ in units]

    def cumulative(x, tri_u):
        hi, lo = _split(x)
        return _dot(tri_u, hi) + _dot(tri_u, lo)

    log_p = each(cumulative, lw, tri)
    log_pt = each(lambda x: jnp.sum(x, axis=0, keepdims=True), lw)
    p_inv = each(lambda x: jnp.exp(-x), log_p)
    a_bar = each(lambda a, p, w: a * jnp.exp(p - w), kk, log_p, lw)
    r_bar = each(lambda a, p: a * jnp.exp(p), r, log_p)
    b_til = each(jnp.multiply, beta, p_inv)
    k_til = each(jnp.multiply, kd, p_inv)
    p_rest = each(lambda t, p: jnp.exp(t - p), log_pt, log_p)
    b_end = each(jnp.multiply, beta, p_rest)
    k_end = each(jnp.multiply, kd, p_rest)

    ar = each(lambda a, b: jnp.concatenate([a, b], axis=0), a_bar, r_bar)
    gram_b = each(lambda a, b: mm_nt(a, block_diag(b)), ar, b_til)
    gram_k = each(lambda a, b: mm_nt(a, block_diag(b)), ar, k_til)
    l_b = each(lambda g, m: jnp.where(m, g[:t_len], 0.0), gram_b, strict)
    m_b = each(lambda g, m: jnp.where(m, g[t_len:], 0.0), gram_b, incl)
    l_k = each(lambda g, m: jnp.where(m, g[:t_len], 0.0), gram_k, strict)
    m_k = each(lambda g, m: jnp.where(m, g[t_len:], 0.0), gram_k, incl)

    stack = lambda a, b: jnp.concatenate([a, b], axis=0)
    t_inv = each(lambda x: eye - x, l_b)
    l_pow = each(lambda x: mm(x, block_diag(x)), l_b)
    span = 4
    while span < t_len:
        both = each(lambda t, x: mm(stack(t, x), block_diag(x)), t_inv, l_pow)
        t_inv = each(lambda t, p: t + p[:t_len], t_inv, both)
        l_pow = each(lambda p: p[t_len:], both)
        span *= 2
    t_inv = each(lambda t, x: t + mm(t, block_diag(x)), t_inv, l_pow)

    def head_t(x):
        return jnp.concatenate([x[:, h * head:(h + 1) * head].T for h in range(n_pack)], axis=1)

    bd_v = each(block_diag, v)
    w_til = each(lambda t, a: mm(t, block_diag(a)), t_inv, a_bar)
    lmv = each(lambda lk, mk, ke, bv: mm(jnp.concatenate([lk, mk, head_t(ke)], axis=0), bv), l_k, m_k, k_end, bd_v)
    u_til = each(lambda t, a: mm(t, block_diag(a[:t_len])), t_inv, lmv)
    mb_bt = each(lambda m, b: stack(m, head_t(b)), m_b, b_end)
    from_w = each(lambda a, w: mm(a, block_diag(w)), mb_bt, w_til)
    from_u = each(lambda a, u: mm(a, block_diag(u)), mb_bt, u_til)
    q_hat = each(lambda a, fw: a - fw[:t_len], r_bar, from_w)
    y_hat = each(lambda a, fu: a[t_len:2 * t_len] - fu[:t_len], lmv, from_u)
    g_mat = each(lambda t, fw: eye * jnp.exp(t) - fw[t_len:], log_pt, from_w)
    h_mat = each(lambda a, fu: a[2 * t_len:] - fu[t_len:], lmv, from_u)

    bd_s = [block_diag(st_scr[d, :, c]) for d, c in units]
    from_state = each(lambda q, g, s: mm(stack(q, g), s), q_hat, g_mat, bd_s)
    for (d, c), fs, y0, h in zip(units, from_state, y_hat, h_mat):
        (yb_ref if d else yf_ref)[:, c] = (fs[:t_len] + y0).astype(yf_ref.dtype)
        st_scr[d, :, c] = fs[t_len:] + h


def _rw_scan(r, kk, v, lw, kd, beta, *, bsz, l_ctx, l_lat, head):
    n, c_rw = r.shape
    t_len = RW_CHUNK
    nc = l_ctx // t_len
    nl = l_lat // t_len
    ctx_blocks = bsz * nc

    def blk(d, b, c):
        cc = nc - 1 - c if d else c
        cl = nl - 1 - (c - nc) if d else c - nc
        return jnp.where(c < nc, b * nc + cc, ctx_blocks + b * nl + cl)

    shared = lambda d: pl.BlockSpec((t_len, c_rw), lambda b, c: (blk(d, b, c), 0))
    per_dir = lambda d: pl.BlockSpec((None, t_len, c_rw), lambda b, c: (d, blk(d, b, c), 0))
    side = lambda d: [shared(d), shared(d), shared(d), per_dir(d), per_dir(d), per_dir(d)]
    kern = functools.partial(_rw_scan_kernel, head=head)
    sds = jax.ShapeDtypeStruct
    return pl.pallas_call(
        kern,
        grid=(bsz, nc + nl),
        in_specs=side(0) + side(1),
        out_specs=[shared(0), shared(1)],
        out_shape=[sds((n, c_rw), BF16), sds((n, c_rw), BF16)],
        scratch_shapes=[pltpu.VMEM((2, head, c_rw), F32)],
        compiler_params=_params(("arbitrary", "arbitrary")),
        name="rwkv_scan",
    )(r, kk, v, lw, kd, beta, r, kk, v, lw, kd, beta)


def _rw_readout_kernel(yf_ref, yb_ref, g_ref, bonus_ref, lnw_ref, lnb_ref, ones_ref, o_ref, *, head):
    ones = ones_ref[...]
    pack = ones.shape[0]
    inv = 1.0 / head
    for s in range(o_ref.shape[1] // pack):
        cs = slice(s * pack, (s + 1) * pack)
        y = yf_ref[:, cs].astype(F32) + yb_ref[:, cs].astype(F32)
        mu = _dot_exact_rhs(y, ones) * inv
        yc = y - mu
        var = _dot_exact_rhs(yc * yc, ones) * inv
        yn = yc * lax.rsqrt(var + GN_EPS) * lnw_ref[:, cs] + lnb_ref[:, cs]
        o_ref[:, cs] = ((yn + bonus_ref[:, cs]) * g_ref[:, cs]).astype(o_ref.dtype)


def _rw_readout(y_fwd, y_bwd, g, bonus, ln_w, ln_b, *, head, tm):
    n, c_rw = y_fwd.shape
    pack = RW_PACK * head
    tn = _pick(c_rw, RW_PREP_WIDTH)
    tile = pl.BlockSpec((tm, tn), lambda i, j: (i, j))
    vec = pl.BlockSpec((1, tn), lambda i, j: (0, j))
    kern = functools.partial(_rw_readout_kernel, head=head)
    return pl.pallas_call(
        kern,
        grid=(n // tm, c_rw // tn),
        in_specs=[tile, tile, tile, tile, vec, vec, pl.BlockSpec((pack, pack), lambda i, j: (0, 0))],
        out_specs=tile,
        out_shape=jax.ShapeDtypeStruct((n, c_rw), BF16),
        compiler_params=_params(("arbitrary", "arbitrary")),
        name="rwkv_readout",
    )(y_fwd, y_bwd, g, bonus, ln_w.reshape(1, c_rw), ln_b.reshape(1, c_rw), _head_ones(pack, head))


def _final_norm_kernel(h_ref, gain_ref, o_ref):
    x = h_ref[...]
    ms = jnp.mean(x * x, axis=-1, keepdims=True)
    o_ref[...] = x * lax.rsqrt(ms + NORM_EPS) * gain_ref[...]


def _final_norm(h, gain, *, tm, row0):
    n, d = h.shape
    b0 = row0 // tm
    return pl.pallas_call(
        _final_norm_kernel,
        grid=((n - row0) // tm,),
        in_specs=[pl.BlockSpec((tm, d), lambda i: (i + b0, 0)), pl.BlockSpec((1, d), lambda i: (0, 0))],
        out_specs=pl.BlockSpec((tm, d), lambda i: (i, 0)),
        out_shape=jax.ShapeDtypeStruct((n - row0, d), F32),
        compiler_params=_params(("arbitrary",)),
        name="final_norm",
    )(h, gain.reshape(1, d))


def _to_s5_layout(u, bsz, l_ctx, l_lat, sel):
    t = S5_CHUNK
    n_ctx = bsz * l_ctx
    nj = u.shape[0]

    def tiles(x, length):
        x = x.reshape(nj, bsz, length // t, t * 256)
        return jnp.transpose(x, (0, 2, 1, 3)).reshape(nj, (length // t) * bsz, t * 256)

    return _lane_perm([tiles(u[:, :n_ctx], l_ctx), tiles(u[:, n_ctx:], l_lat)], sel)


def _from_s5_layout(y, bsz, l_ctx, l_lat, sel):
    t = S5_CHUNK
    nj, rows, _ = y.shape
    out = _lane_perm([y], sel).reshape(nj, t, rows, 256)
    rows_ctx = (l_ctx // t) * bsz

    def untile(o, length):
        o = o.reshape(nj, t, length // t, bsz, 256)
        return jnp.transpose(o, (3, 2, 1, 0, 4)).reshape(bsz * length, nj * 256)

    return untile(out[:, :, :rows_ctx], l_ctx), untile(out[:, :, rows_ctx:], l_lat)


def kernel(x, c, ctx, c_ctx, ada_down, ada_up, ada_b, norm1, w_in, s5_a_re, s5_a_im, s5_log_dt, s5_b_re, s5_b_im, s5_c_re, s5_c_im, s5_d, s5_glu_w, s5_glu_b, rw_mu, rw_w0, rw_w_up, rw_a0, rw_a_up, rw_g_up, rw_k_k, rw_k_a, rw_r_k, rw_ln_w, rw_ln_b, w_proj_a, w_proj_b, gate_up, gate_b, w_out, norm2, mlp_w1, mlp_w2, norm_f):
    bsz, l_lat, d = x.shape
    l_ctx = ctx.shape[1]
    depth = w_in.shape[0]
    n_ctx = bsz * l_ctx
    n_lat = bsz * l_lat
    s5_groups, s5_state = s5_a_re.shape[2], s5_a_re.shape[3]
    s5_gch = s5_b_re.shape[-1]
    c_s5 = s5_groups * s5_gch
    rw_heads, head = rw_r_k.shape[1], rw_r_k.shape[2]
    c_rw = rw_heads * head
    lora_w, lora_a, lora_g = rw_w_up.shape[2], rw_a_up.shape[2], rw_g_up.shape[1]
    gate_rank = gate_up.shape[2]
    n_small_rw = 2 * lora_w + 2 * lora_a + lora_g
    c_big = c_s5 + 3 * c_rw
    assert w_in.shape[2] == c_big + n_small_rw + gate_rank
    assert bsz % 8 == 0 and l_ctx % RW_CHUNK == 0 and l_lat % RW_CHUNK == 0 and l_lat % GRID_W == 0
    assert s5_gch * S5_CHUNK == 256 and RW_PACK * head == 256 and c_rw % 256 == 0 and gate_rank % 128 == 0
    assert RW_CHUNK == head
    assert c_s5 % IN_PROJ_TN == 0 and (3 * c_rw) % IN_PROJ_TN == 0 and n_small_rw + gate_rank <= IN_PROJ_TN

    tm = _pick(l_lat, _pick(n_ctx, TM))

    def row_of_block(i):
        ctx_blocks = n_ctx // tm
        return jnp.where(i < ctx_blocks, bsz, (i - ctx_blocks) // (l_lat // tm))

    n_cond = ((bsz + 1 + 7) // 8) * 8
    cond = jnp.zeros((n_cond, d), F32).at[:bsz].set(c).at[bsz].set(c_ctx)
    mods_all = _ada_modulation(cond, ada_down, ada_up, ada_b)
    mods_all = jnp.transpose(mods_all.reshape(depth, n_cond, 6, 1, d), (0, 2, 1, 3, 4))

    h = jnp.concatenate([ctx.reshape(n_ctx, d), x.reshape(n_lat, d)], axis=0)

    lane_sel = _lane_select_matrices(s5_gch)
    w_in_bf = _pack_w_in(w_in, c_big=c_big, n_small=n_small_rw, rank=gate_rank, tn=IN_PROJ_TN)
    glu_bf = _to_bf16(s5_glu_w)
    proj_a_bf, proj_b_bf, w_out_bf = _to_bf16(w_proj_a), _to_bf16(w_proj_b), _to_bf16(w_out)
    gate_up_bf = _to_bf16(gate_up.reshape(depth, 2 * gate_rank, d)).reshape(gate_up.shape)
    w1_bf, w2_bf = _to_bf16(mlp_w1), _to_bf16(mlp_w2)

    for l in range(depth):
        last = l == depth - 1
        row0 = n_ctx if last else 0
        mods = mods_all[l]
        lp = {"rw_mu": rw_mu[l], "rw_w0": rw_w0[l], "rw_w_up": rw_w_up[l], "rw_a0": rw_a0[l],
              "rw_a_up": rw_a_up[l], "rw_g_up": rw_g_up[l], "rw_k_k": rw_k_k[l], "rw_k_a": rw_k_a[l],
              "rw_r_k": rw_r_k[l]}
        z_u, z_rkv, z_small = _in_proj(h, norm1[l], mods, w_in_bf, l, row_of_block, tm=tm, tn=IN_PROJ_TN,
                                       c_u=c_s5, c_rkv=3 * c_rw)

        lag, bm, cmt, lam = _s5_matrices(s5_a_re[l], s5_a_im[l], s5_log_dt[l], s5_b_re[l], s5_b_im[l],
                                            s5_c_re[l], s5_c_im[l], s5_d[l])
        u_g = _to_s5_layout(z_u, bsz, l_ctx, l_lat, lane_sel)
        y_g = _s5_scan(u_g, lag, bm, cmt, lam, n_ctx=l_ctx // S5_CHUNK, n_chunks=(l_ctx + l_lat) // S5_CHUNK, bsz=bsz)
        y_ctx, y_lat = _from_s5_layout(y_g, bsz, l_ctx, l_lat, lane_sel)
        ya = _s5_glu(y_ctx, y_lat, glu_bf, l, s5_glu_b[l], tm=tm)

        r, kk, v, lw, kd, beta, g, bonus = _rw_prep(z_rkv, z_small, lp, bsz=bsz, l_ctx=l_ctx, l_lat=l_lat,
                                                    c_rw=c_rw, head=head, gate_rank=gate_rank)
        y_fwd, y_bwd = _rw_scan(r, kk, v, lw, kd, beta, bsz=bsz, l_ctx=l_ctx, l_lat=l_lat, head=head)
        yb = _rw_readout(y_fwd, y_bwd, g, bonus, rw_ln_w[l], rw_ln_b[l], head=head, tm=tm)

        h = _merge_out(ya, yb, z_small, proj_a_bf, proj_b_bf, gate_up_bf, gate_b[l], w_out_bf, l, h, mods,
                       row_of_block, tm=tm, row0=row0)
        h = _mlp(h, norm2[l], mods, w1_bf, w2_bf, l, row_of_block, tm=tm, row0=row0)

    out = _final_norm(h, norm_f, tm=tm, row0=n_ctx)
    return out.reshape(bsz, l_lat, d)
```
